```python
import math
import jax, jax.numpy as jnp
from jax import lax
import numpy as np


D_MODEL = 1024
BATCH = 1
SEQ = 16384
DEPTH = 4

N_META = 16
SSM_WIDTH = D_MODEL // 2
SSM_GROUP = 16
SSM_GROUPS = SSM_WIDTH // SSM_GROUP
SSM_STATE = 64
MLSTM_WIDTH = D_MODEL // 2
MLSTM_HEADS = 4
MLSTM_HEAD_DIM = MLSTM_WIDTH // MLSTM_HEADS
MLSTM_CHUNK = 64
QK_CONV = 4
FFN_DIM = 2816
FFN_CONV = 3
NORM_EPS = 1e-6
PAD_LOG_INPUT_GATE = -1e4
IN_WIDTHS = (SSM_WIDTH, MLSTM_WIDTH, MLSTM_WIDTH, MLSTM_WIDTH, MLSTM_WIDTH, MLSTM_HEADS, MLSTM_HEADS, D_MODEL, D_MODEL)
IN_SPLITS = tuple(int(s) for s in np.cumsum(IN_WIDTHS)[:-1])
N_IN = int(sum(IN_WIDTHS))

kernel_name = 'hybrid_s5_mlstm_gated_trunk'


def rmsnorm(x, g):
    xf = x.astype(jnp.float32)
    y = xf * lax.rsqrt(jnp.mean(xf * xf, axis=-1, keepdims=True) + NORM_EPS)
    return (y * g.astype(jnp.float32)).astype(x.dtype)


def causal_dwconv(x, w, b):
    k = w.shape[0]
    y = lax.conv_general_dilated(x, w[:, None, :].astype(x.dtype), (1,), [(k - 1, 0)],
                                 dimension_numbers=('NWC', 'WIO', 'NWC'),
                                 feature_group_count=x.shape[-1])
    return y + b.astype(x.dtype)


def _complex_affine_combine(e1, e2):
    a1r, a1i, b1r, b1i = e1
    a2r, a2i, b2r, b2i = e2
    return (a2r * a1r - a2i * a1i,
            a2r * a1i + a2i * a1r,
            a2r * b1r - a2i * b1i + b2r,
            a2r * b1i + a2i * b1r + b2i)


def s5_ssm(u, lam_re, lam_im, b_re, b_im, c_re, c_im, d, log_dt):
    bsz, L, _ = u.shape
    u = u.astype(jnp.float32).reshape(bsz, L, SSM_GROUPS, SSM_GROUP)
    lam_re = lam_re.astype(jnp.float32)
    lam_im = lam_im.astype(jnp.float32)
    dt = jnp.exp(log_dt.astype(jnp.float32))[:, None]
    mag = jnp.exp(lam_re * dt)
    ab_re = mag * jnp.cos(lam_im * dt)
    ab_im = mag * jnp.sin(lam_im * dt)
    nr, ni = ab_re - 1.0, ab_im
    den = lam_re * lam_re + lam_im * lam_im
    z_re = (nr * lam_re + ni * lam_im) / den
    z_im = (ni * lam_re - nr * lam_im) / den
    b_re = b_re.astype(jnp.float32)
    b_im = b_im.astype(jnp.float32)
    bb_re = z_re[..., None] * b_re - z_im[..., None] * b_im
    bb_im = z_re[..., None] * b_im + z_im[..., None] * b_re
    bu_re = jnp.einsum('blgh,gph->blgp', u, bb_re)
    bu_im = jnp.einsum('blgh,gph->blgp', u, bb_im)
    a_re = jnp.broadcast_to(ab_re, bu_re.shape)
    a_im = jnp.broadcast_to(ab_im, bu_im.shape)
    _, _, x_re, x_im = lax.associative_scan(_complex_affine_combine, (a_re, a_im, bu_re, bu_im), axis=1)
    y = (jnp.einsum('blgp,ghp->blgh', x_re, c_re.astype(jnp.float32))
         - jnp.einsum('blgp,ghp->blgh', x_im, c_im.astype(jnp.float32))
         + d.astype(jnp.float32) * u)
    return y.reshape(bsz, L, SSM_WIDTH)


def mlstm_chunkwise(q, k, v, log_i, log_f):
    bsz, lp, nh, dh = q.shape
    nc = lp // MLSTM_CHUNK

    def to_chunks(t):
        return jnp.moveaxis(t.reshape(bsz, nc, MLSTM_CHUNK, nh, -1), 3, 1)

    q, k, v = to_chunks(q), to_chunks(k), to_chunks(v)
    log_i = to_chunks(log_i[..., None])[..., 0]
    log_f = to_chunks(log_f[..., None])[..., 0]
    b = jnp.cumsum(log_f, axis=-1)
    g = b[..., -1]
    causal = jnp.tril(jnp.ones((MLSTM_CHUNK, MLSTM_CHUNK), dtype=bool))
    log_d = jnp.where(causal, b[..., :, None] - b[..., None, :] + log_i[..., None, :], -jnp.inf)
    a = g[..., None] - b + log_i

    def step(carry, xs):
        c_st, n_st, m_st = carry
        k_c, v_c, a_c, g_c = xs
        m_new = jnp.maximum(g_c + m_st, jnp.max(a_c, axis=-1))
        decay = jnp.exp(g_c + m_st - m_new)
        w = jnp.exp(a_c - m_new[..., None])
        c_new = decay[..., None, None] * c_st + jnp.einsum('bhs,bhsd,bhse->bhde', w, v_c, k_c)
        n_new = decay[..., None] * n_st + jnp.einsum('bhs,bhse->bhe', w, k_c)
        return (c_new, n_new, m_new), (c_st, n_st, m_st)

    init = (jnp.zeros((bsz, nh, dh, dh), jnp.float32),
            jnp.zeros((bsz, nh, dh), jnp.float32),
            jnp.zeros((bsz, nh), jnp.float32))
    xs = (jnp.moveaxis(k, 2, 0), jnp.moveaxis(v, 2, 0), jnp.moveaxis(a, 2, 0), jnp.moveaxis(g, 2, 0))
    _, (c_prev, n_prev, m_prev) = lax.scan(step, init, xs)
    c_prev = jnp.moveaxis(c_prev, 0, 2)
    n_prev = jnp.moveaxis(n_prev, 0, 2)
    m_prev = jnp.moveaxis(m_prev, 0, 2)
    log_inter = b + m_prev[..., None]
    m_t = jnp.maximum(log_inter, jnp.max(log_d, axis=-1))
    s = jnp.einsum('bhnte,bhnse->bhnts', q, k) * jnp.exp(log_d - m_t[..., None])
    w_inter = jnp.exp(log_inter - m_t)
    num = (jnp.einsum('bhnts,bhnsd->bhntd', s, v)
           + w_inter[..., None] * jnp.einsum('bhnde,bhnte->bhntd', c_prev, q))
    den = jnp.sum(s, axis=-1) + w_inter * jnp.einsum('bhne,bhnte->bhnt', n_prev, q)
    h = num / jnp.maximum(jnp.abs(den), jnp.exp(-m_t))[..., None]
    return jnp.moveaxis(h, 1, 3).reshape(bsz, lp, nh, dh)


def token_mixer(x, g_pre, g_post, w_in, b_gates, lam_re, lam_im, b_re, b_im, c_re, c_im, d, log_dt,
                w_glu, w_qk, b_qk, g_head, w_a, w_b, w_out):
    bsz, L, _ = x.shape
    dtype = x.dtype
    h = rmsnorm(x, g_pre)
    proj = h @ w_in
    u, q, k, v, o, gi, gf, ga, gb = jnp.split(proj, IN_SPLITS, axis=-1)

    y_a = jax.nn.gelu(s5_ssm(u, lam_re, lam_im, b_re, b_im, c_re, c_im, d, log_dt))
    y_a = y_a * jax.nn.sigmoid(y_a @ w_glu.astype(jnp.float32))

    qk = jax.nn.silu(causal_dwconv(jnp.concatenate([q, k], axis=-1), w_qk, b_qk)).astype(jnp.float32)
    q, k = jnp.split(qk, 2, axis=-1)
    heads = lambda t: t.reshape(bsz, L, MLSTM_HEADS, MLSTM_HEAD_DIM)
    q = heads(q)
    k = heads(k) * (MLSTM_HEAD_DIM ** -0.5)
    v = heads(v.astype(jnp.float32))
    gates = (jnp.concatenate([gi, gf], axis=-1) + b_gates.astype(dtype)).astype(jnp.float32)
    log_i, f_pre = jnp.split(gates, 2, axis=-1)
    log_f = jax.nn.log_sigmoid(f_pre)
    pad = MLSTM_CHUNK - N_META
    padseq = lambda t, val: jnp.pad(t, [(0, 0), (pad, 0)] + [(0, 0)] * (t.ndim - 2), constant_values=val)
    hb = mlstm_chunkwise(padseq(q, 0.0), padseq(k, 0.0), padseq(v, 0.0),
                         padseq(log_i, PAD_LOG_INPUT_GATE), padseq(log_f, 0.0))[:, pad:]
    hb = hb * lax.rsqrt(jnp.mean(hb * hb, axis=-1, keepdims=True) + NORM_EPS)
    hb = hb * g_head.astype(jnp.float32).reshape(MLSTM_HEADS, MLSTM_HEAD_DIM)
    y_b = hb.reshape(bsz, L, MLSTM_WIDTH) * jax.nn.sigmoid(o.astype(jnp.float32))

    merged = (jax.nn.sigmoid(ga) * (y_a.astype(dtype) @ w_a)
              + jax.nn.sigmoid(gb) * (y_b.astype(dtype) @ w_b))
    return x + rmsnorm(merged @ w_out, g_post)


def channel_mixer(x, g_pre, g_post, w_gate, w_up, w_conv, b_conv, w_down):
    h = rmsnorm(x, g_pre)
    a = causal_dwconv(h @ w_gate, w_conv, b_conv)
    y = (jax.nn.gelu(a, approximate=True) * (h @ w_up)) @ w_down
    return x + rmsnorm(y, g_post)


def setup_inputs(seed: int = 0) -> dict:
    key = jax.random.key(seed)
    ks = iter(jax.random.split(key, 40))
    f32 = jnp.float32
    nrm = lambda shape, scale: scale * jax.random.normal(next(ks), shape, f32)
    gain = lambda: 1.0 + nrm((DEPTH, D_MODEL), 0.02)
    G, P, GH, H = SSM_GROUPS, SSM_STATE, SSM_GROUP, MLSTM_HEADS
    x = nrm((BATCH, SEQ, D_MODEL), 1.0)
    meta_tokens = nrm((N_META, D_MODEL), 1.0)
    g_mix_pre = gain()
    g_mix_post = gain()
    w_in = nrm((DEPTH, D_MODEL, N_IN), D_MODEL ** -0.5)
    b_gates = jnp.concatenate([nrm((DEPTH, H), 0.1),
                               jnp.linspace(3.0, 6.0, H, dtype=f32)[None, :] + nrm((DEPTH, H), 0.1)], axis=-1)
    n_idx = jnp.arange(P, dtype=f32)
    ssm_lambda_re = -0.5 + nrm((DEPTH, G, P), 0.01)
    ssm_lambda_im = jnp.pi * n_idx[None, None, :] + nrm((DEPTH, G, P), 0.01)
    ssm_b_re = nrm((DEPTH, G, P, GH), (2 * GH) ** -0.5)
    ssm_b_im = nrm((DEPTH, G, P, GH), (2 * GH) ** -0.5)
    ssm_c_re = nrm((DEPTH, G, GH, P), P ** -0.5)
    ssm_c_im = nrm((DEPTH, G, GH, P), P ** -0.5)
    ssm_d = nrm((DEPTH, G, GH), 1.0)
    ssm_log_dt = jax.random.uniform(next(ks), (DEPTH, G), f32, math.log(1e-3), math.log(1e-1))
    w_ssm_glu = nrm((DEPTH, SSM_WIDTH, SSM_WIDTH), SSM_WIDTH ** -0.5)
    w_qk_conv = nrm((DEPTH, QK_CONV, 2 * MLSTM_WIDTH), QK_CONV ** -0.5)
    b_qk_conv = nrm((DEPTH, 2 * MLSTM_WIDTH), 0.02)
    g_head_norm = 1.0 + nrm((DEPTH, MLSTM_WIDTH), 0.02)
    w_branch_ssm = nrm((DEPTH, SSM_WIDTH, D_MODEL), SSM_WIDTH ** -0.5)
    w_branch_mlstm = nrm((DEPTH, MLSTM_WIDTH, D_MODEL), MLSTM_WIDTH ** -0.5)
    w_out = nrm((DEPTH, D_MODEL, D_MODEL), D_MODEL ** -0.5)
    g_ffn_pre = gain()
    g_ffn_post = gain()
    w_ffn_gate = nrm((DEPTH, D_MODEL, FFN_DIM), D_MODEL ** -0.5)
    w_ffn_up = nrm((DEPTH, D_MODEL, FFN_DIM), D_MODEL ** -0.5)
    w_ffn_conv = nrm((DEPTH, FFN_CONV, FFN_DIM), FFN_CONV ** -0.5)
    b_ffn_conv = nrm((DEPTH, FFN_DIM), 0.02)
    w_ffn_down = nrm((DEPTH, FFN_DIM, D_MODEL), FFN_DIM ** -0.5)
    return {'x': x, 'meta_tokens': meta_tokens, 'g_mix_pre': g_mix_pre, 'g_mix_post': g_mix_post,
            'w_in': w_in, 'b_gates': b_gates, 'ssm_lambda_re': ssm_lambda_re, 'ssm_lambda_im': ssm_lambda_im,
            'ssm_b_re': ssm_b_re, 'ssm_b_im': ssm_b_im, 'ssm_c_re': ssm_c_re, 'ssm_c_im': ssm_c_im,
            'ssm_d': ssm_d, 'ssm_log_dt': ssm_log_dt, 'w_ssm_glu': w_ssm_glu, 'w_qk_conv': w_qk_conv,
            'b_qk_conv': b_qk_conv, 'g_head_norm': g_head_norm, 'w_branch_ssm': w_branch_ssm,
            'w_branch_mlstm': w_branch_mlstm, 'w_out': w_out, 'g_ffn_pre': g_ffn_pre, 'g_ffn_post': g_ffn_post,
            'w_ffn_gate': w_ffn_gate, 'w_ffn_up': w_ffn_up, 'w_ffn_conv': w_ffn_conv, 'b_ffn_conv': b_ffn_conv,
            'w_ffn_down': w_ffn_down}


def reference(x, meta_tokens, g_mix_pre, g_mix_post, w_in, b_gates, ssm_lambda_re, ssm_lambda_im,
              ssm_b_re, ssm_b_im, ssm_c_re, ssm_c_im, ssm_d, ssm_log_dt, w_ssm_glu, w_qk_conv,
              b_qk_conv, g_head_norm, w_branch_ssm, w_branch_mlstm, w_out, g_ffn_pre, g_ffn_post,
              w_ffn_gate, w_ffn_up, w_ffn_conv, b_ffn_conv, w_ffn_down):
    bsz = x.shape[0]
    meta = jnp.broadcast_to(meta_tokens.astype(x.dtype)[None], (bsz, N_META, D_MODEL))
    h = jnp.concatenate([meta, x], axis=1)
    for l in range(DEPTH):
        h = token_mixer(h, g_mix_pre[l], g_mix_post[l], w_in[l], b_gates[l],
                        ssm_lambda_re[l], ssm_lambda_im[l], ssm_b_re[l], ssm_b_im[l],
                        ssm_c_re[l], ssm_c_im[l], ssm_d[l], ssm_log_dt[l], w_ssm_glu[l],
                        w_qk_conv[l], b_qk_conv[l], g_head_norm[l],
                        w_branch_ssm[l], w_branch_mlstm[l], w_out[l])
        h = channel_mixer(h, g_ffn_pre[l], g_ffn_post[l], w_ffn_gate[l], w_ffn_up[l],
                          w_ffn_conv[l], b_ffn_conv[l], w_ffn_down[l])
    return h[:, N_META:]
```

```python
import functools
import math

import jax
import jax.numpy as jnp
from jax import lax
from jax.experimental import pallas as pl
from jax.experimental.pallas import tpu as pltpu

D_MODEL = 1024
N_META = 16
SSM_WIDTH = 512
SSM_GROUP = 16
SSM_GROUPS = 32
SSM_STATE = 64
MLSTM_WIDTH = 512
MLSTM_HEADS = 4
MLSTM_HEAD_DIM = 128
MLSTM_REF_CHUNK = 64
QK_CONV = 4
FFN_DIM = 2816
FFN_CONV = 3
NORM_EPS = 1e-6
PAD_LOG_INPUT_GATE = -1e4

LANES = 128
SUBLANES = 8
TL = 512
CH = 128
SS = 8
PAIRS = SSM_GROUPS // 2
PAIR_IN = 2 * SSM_GROUP
PAIR_W = SS * PAIR_IN
PAIR_STATE = 2 * SSM_STATE
HALF = PAIRS * PAIR_STATE
FRONT_PAD = MLSTM_REF_CHUNK - N_META
FFN_CHUNK = 256
N_FFN_CHUNKS = FFN_DIM // FFN_CHUNK
GATE_PAD = LANES
VMEM_LIMIT = 56 * 1024 * 1024

_BF16 = jnp.bfloat16
_F32 = jnp.float32


def _rms(x, g):
    return x * lax.rsqrt(jnp.mean(x * x, axis=-1, keepdims=True) + NORM_EPS) * g


def _gelu_tanh(x):
    return 0.5 * x * (1.0 + jnp.tanh(math.sqrt(2.0 / math.pi) * (x + 0.044715 * (x * x * x))))


def _sigmoid(x):
    return 1.0 / (1.0 + jnp.exp(-x))


def _log_sigmoid(x):
    return jnp.minimum(x, 0.0) - jnp.log(1.0 + jnp.exp(-jnp.abs(x)))


def _dot(a, b):
    return jnp.dot(a, b, preferred_element_type=_F32)


def _s5_in_kernel(x_ref, g_ref, w_ref, u_ref):
    hn = _rms(x_ref[...], g_ref[...]).astype(_BF16)
    u = _dot(hn, w_ref[...])
    for p in range(PAIRS):
        u_ref[p] = u[:, p * PAIR_IN:(p + 1) * PAIR_IN]


def _s5_in(x, g, w):
    lp = x.shape[0]
    return pl.pallas_call(
        _s5_in_kernel,
        grid=(lp // TL,),
        in_specs=[
            pl.BlockSpec((TL, D_MODEL), lambda i: (i, 0)),
            pl.BlockSpec((1, D_MODEL), lambda i: (0, 0)),
            pl.BlockSpec((D_MODEL, SSM_WIDTH), lambda i: (0, 0)),
        ],
        out_specs=pl.BlockSpec((PAIRS, TL, PAIR_IN), lambda i: (0, i, 0)),
        out_shape=jax.ShapeDtypeStruct((PAIRS, lp, PAIR_IN), _F32),
        compiler_params=pltpu.CompilerParams(dimension_semantics=("parallel",),
                                             vmem_limit_bytes=VMEM_LIMIT),
        name="s5_in",
    )(x, g, w)


def _s5_core_kernel(rt, u_ref, bs_ref, cs_ref, ds_ref, a_ref, y_ref, vx_ref, st_ref):
    rb = rt // SUBLANES

    @pl.when(pl.program_id(0) == 0)
    def _():
        st_ref[...] = jnp.zeros_like(st_ref)

    for p in range(PAIRS):
        v = _dot(u_ref[p].astype(_BF16), bs_ref[p])
        vx_ref[:, :, p * PAIR_STATE:(p + 1) * PAIR_STATE] = (
            v[:, :PAIR_STATE].reshape(rb, SUBLANES, PAIR_STATE))
        vx_ref[:, :, HALF + p * PAIR_STATE:HALF + (p + 1) * PAIR_STATE] = (
            v[:, PAIR_STATE:].reshape(rb, SUBLANES, PAIR_STATE))

    a_re = a_ref[:, :HALF]
    a_im = a_ref[:, HALF:]

    def body(i, carry):
        s_re, s_im = carry
        for j in range(SUBLANES):
            v_re = vx_ref[i, j:j + 1, :HALF]
            v_im = vx_ref[i, j:j + 1, HALF:]
            vx_ref[i, j:j + 1, :HALF] = s_re[j:j + 1]
            vx_ref[i, j:j + 1, HALF:] = s_im[j:j + 1]
            n_re = a_re * s_re - a_im * s_im + v_re
            n_im = a_re * s_im + a_im * s_re + v_im
            s_re, s_im = n_re, n_im
        return s_re, s_im

    s_re, s_im = lax.fori_loop(0, rb, body, (st_ref[:, :HALF], st_ref[:, HALF:]))
    st_ref[:, :HALF] = s_re
    st_ref[:, HALF:] = s_im

    for p in range(PAIRS):
        x_re = vx_ref[:, :, p * PAIR_STATE:(p + 1) * PAIR_STATE].reshape(rt, PAIR_STATE)
        x_im = vx_ref[:, :, HALF + p * PAIR_STATE:HALF + (p + 1) * PAIR_STATE].reshape(rt, PAIR_STATE)
        xp = jnp.concatenate([x_re, x_im], axis=-1).astype(_BF16)
        y_ref[p] = _dot(xp, cs_ref[p]) + _dot(u_ref[p].astype(_BF16), ds_ref[p])


def _s5_core(u, bs, cs, ds, a8):
    rows = u.shape[1]
    rt = rows // 8 if (rows // 8) % SUBLANES == 0 else rows
    wspec = pl.BlockSpec((PAIRS, PAIR_W, PAIR_W), lambda i: (0, 0, 0))
    return pl.pallas_call(
        functools.partial(_s5_core_kernel, rt),
        grid=(rows // rt,),
        in_specs=[
            pl.BlockSpec((PAIRS, rt, PAIR_W), lambda i: (0, i, 0)),
            wspec, wspec, wspec,
            pl.BlockSpec((SUBLANES, 2 * HALF), lambda i: (0, 0)),
        ],
        out_specs=pl.BlockSpec((PAIRS, rt, PAIR_W), lambda i: (0, i, 0)),
        out_shape=jax.ShapeDtypeStruct((PAIRS, rows, PAIR_W), _F32),
        scratch_shapes=[
            pltpu.VMEM((rt // SUBLANES, SUBLANES, 2 * HALF), _F32),
            pltpu.VMEM((SUBLANES, 2 * HALF), _F32),
        ],
        compiler_params=pltpu.CompilerParams(dimension_semantics=("arbitrary",),
                                             vmem_limit_bytes=VMEM_LIMIT),
        name="s5_core",
    )(u, bs, cs, ds, a8)


def _mixer_kernel(x_ref, y5_ref, gpre_ref, gpost_ref, wqk_ref, wv_ref, wo_ref, wga_ref, wgb_ref,
                  wgc_ref, wgr_ref, bgc_ref, bgr_ref, wcv_ref, ghead_ref, wglu_ref, wa_ref, wb_ref,
                  wout_ref, o_ref, cv_ref, ct_ref, n_ref, m_ref, yb_ref):
    pid = pl.program_id(0)

    @pl.when(pid == 0)
    def _():
        cv_ref[0:SUBLANES, :] = jnp.zeros((SUBLANES, 2 * MLSTM_WIDTH), _F32)
        ct_ref[...] = jnp.zeros_like(ct_ref)
        n_ref[...] = jnp.zeros_like(n_ref)
        m_ref[...] = jnp.zeros_like(m_ref)

    x = x_ref[...]
    hn = _rms(x, gpre_ref[...]).astype(_BF16)

    row_c = pid * TL + lax.broadcasted_iota(jnp.int32, (TL, 1), 0)
    valid_c = row_c >= FRONT_PAD
    row_r = pid * TL + lax.broadcasted_iota(jnp.int32, (1, TL), 1)
    valid_r = row_r >= FRONT_PAD

    cv_ref[SUBLANES:SUBLANES + TL, :] = _dot(hn, wqk_ref[...])
    acc = wcv_ref[QK_CONV:QK_CONV + 1, :]
    for j in range(QK_CONV):
        off = SUBLANES - (QK_CONV - 1) + j
        acc = acc + wcv_ref[j:j + 1, :] * cv_ref[off:off + TL, :]
    cv_ref[0:SUBLANES, :] = cv_ref[TL:TL + SUBLANES, :]
    qk = jnp.where(valid_c, acc * _sigmoid(acc), 0.0)
    q_all = qk[:, :MLSTM_WIDTH].astype(_BF16)
    k_all = qk[:, MLSTM_WIDTH:] * (MLSTM_HEAD_DIM ** -0.5)
    v_all = jnp.where(valid_c, _dot(hn, wv_ref[...]), 0.0)
    o_all = _dot(hn, wo_ref[...])

    lane = lax.broadcasted_iota(jnp.int32, (1, GATE_PAD), 1)
    gc = _dot(hn, wgc_ref[...]) + bgc_ref[...]
    gc = jnp.where(lane < MLSTM_HEADS, gc, _log_sigmoid(gc))
    gc = jnp.where(valid_c, gc, jnp.where(lane < MLSTM_HEADS, PAD_LOG_INPUT_GATE, 0.0))
    sub = lax.broadcasted_iota(jnp.int32, (2 * MLSTM_HEADS, 1), 0)
    gr = lax.dot_general(wgr_ref[...], hn, (((1,), (1,)), ((), ())),
                         preferred_element_type=_F32) + bgr_ref[:, 0:1]
    gr = jnp.where(sub < MLSTM_HEADS, gr, _log_sigmoid(gr))
    gr = jnp.where(valid_r, gr, jnp.where(sub < MLSTM_HEADS, PAD_LOG_INPUT_GATE, 0.0))

    it = lax.broadcasted_iota(jnp.int32, (CH, CH), 0)
    js = lax.broadcasted_iota(jnp.int32, (CH, CH), 1)
    causal = it >= js
    tri_l = causal.astype(_F32)
    tri_u = (it <= js).astype(_F32)

    for c in range(TL // CH):
        r0 = c * CH
        b_c = jnp.dot(tri_l, gc[r0:r0 + CH, :], preferred_element_type=_F32,
                      precision=lax.Precision.HIGHEST)
        b_r = jnp.dot(gr[:, r0:r0 + CH], tri_u, preferred_element_type=_F32,
                      precision=lax.Precision.HIGHEST)
        for h in range(MLSTM_HEADS):
            hs = slice(h * MLSTM_HEAD_DIM, (h + 1) * MLSTM_HEAD_DIM)
            fl = MLSTM_HEADS + h
            bcol = b_c[:, fl:fl + 1]
            rcol = gc[r0:r0 + CH, h:h + 1] - bcol
            rrow = gr[h:h + 1, r0:r0 + CH] - b_r[fl:fl + 1, :]
            m_st = m_ref[h:h + 1, 0:1]
            rmat = jnp.where(causal, rrow, -jnp.inf)
            mcol = jnp.maximum(m_st, jnp.max(rmat, axis=-1, keepdims=True))
            pmat = jnp.exp(rmat - mcol)
            qc = q_all[r0:r0 + CH, hs]
            kc = k_all[r0:r0 + CH, hs]
            vc = v_all[r0:r0 + CH, hs]
            kcb = kc.astype(_BF16)
            vcb = vc.astype(_BF16)
            s = lax.dot_general(qc, kcb, (((1,), (1,)), ((), ())),
                                preferred_element_type=_F32) * pmat
            w_inter = jnp.exp(m_st - mcol)
            ct = ct_ref[h]
            nrow = n_ref[h:h + 1, :]
            num = _dot(s.astype(_BF16), vcb) + w_inter * _dot(qc, ct.astype(_BF16))
            qn = jnp.sum(qc.astype(_F32) * nrow, axis=-1, keepdims=True)
            den = jnp.sum(s, axis=-1, keepdims=True) + w_inter * qn
            hout = num / jnp.maximum(jnp.abs(den), jnp.exp(-(bcol + mcol)))
            hout = hout * lax.rsqrt(jnp.mean(hout * hout, axis=-1, keepdims=True) + NORM_EPS)
            hout = hout * ghead_ref[:, hs] * _sigmoid(o_all[r0:r0 + CH, hs])
            yb_ref[r0:r0 + CH, hs] = hout
            m_last = mcol[CH - 1:CH, :]
            g_tot = bcol[CH - 1:CH, :]
            decay = jnp.exp(m_st - m_last)
            kw = kc * jnp.exp(rcol - m_last)
            ct_ref[h] = decay * ct + _dot(kw.T.astype(_BF16), vcb)
            n_ref[h:h + 1, :] = decay * nrow + jnp.sum(kw, axis=0, keepdims=True)
            m_ref[h:h + 1, :] = jnp.broadcast_to(g_tot + m_last, (1, LANES))

    y5 = jnp.concatenate([y5_ref[p] for p in range(PAIRS)], axis=-1)
    ya = _gelu_tanh(y5)
    ya = ya * _sigmoid(_dot(ya.astype(_BF16), wglu_ref[...]))

    merged = (_sigmoid(_dot(hn, wga_ref[...])) * _dot(ya.astype(_BF16), wa_ref[...])
              + _sigmoid(_dot(hn, wgb_ref[...])) * _dot(yb_ref[...].astype(_BF16), wb_ref[...]))
    o_ref[...] = x + _rms(_dot(merged.astype(_BF16), wout_ref[...]), gpost_ref[...])


def _const_spec(shape):
    nd = len(shape)
    return pl.BlockSpec(shape, lambda i, _nd=nd: (0,) * _nd)


def _mixer(x, y5, gpre, gpost, wqk, wv, wo, wga, wgb, wgc, wgr, bgc, bgr, wcv, ghead, wglu, wa, wb, wout):
    lp = x.shape[0]
    consts = (gpre, gpost, wqk, wv, wo, wga, wgb, wgc, wgr, bgc, bgr, wcv, ghead, wglu, wa, wb, wout)
    return pl.pallas_call(
        _mixer_kernel,
        grid=(lp // TL,),
        in_specs=[
            pl.BlockSpec((TL, D_MODEL), lambda i: (i, 0)),
            pl.BlockSpec((PAIRS, TL, PAIR_IN), lambda i: (0, i, 0)),
        ] + [_const_spec(c.shape) for c in consts],
        out_specs=pl.BlockSpec((TL, D_MODEL), lambda i: (i, 0)),
        out_shape=jax.ShapeDtypeStruct((lp, D_MODEL), _F32),
        scratch_shapes=[
            pltpu.VMEM((TL + SUBLANES, 2 * MLSTM_WIDTH), _F32),
            pltpu.VMEM((MLSTM_HEADS, MLSTM_HEAD_DIM, MLSTM_HEAD_DIM), _F32),
            pltpu.VMEM((SUBLANES, MLSTM_HEAD_DIM), _F32),
            pltpu.VMEM((SUBLANES, LANES), _F32),
            pltpu.VMEM((TL, MLSTM_WIDTH), _F32),
        ],
        compiler_params=pltpu.CompilerParams(dimension_semantics=("arbitrary",),
                                             vmem_limit_bytes=VMEM_LIMIT),
        name="mixer",
    )(x, y5, *consts)


def _ffn_kernel(x_ref, gpre_ref, gpost_ref, wg_ref, wu_ref, wd_ref, wcv_ref, o_ref,
                cv_ref, hn_ref, acc_ref):
    @pl.when(pl.program_id(0) == 0)
    def _():
        cv_ref[:, 0:SUBLANES, :] = jnp.zeros((N_FFN_CHUNKS, SUBLANES, FFN_CHUNK), _F32)

    x = x_ref[...]
    hn_ref[...] = _rms(x, gpre_ref[...]).astype(_BF16)
    acc_ref[...] = jnp.zeros_like(acc_ref)

    def body(c, carry):
        hn = hn_ref[...]
        cv = cv_ref.at[c]
        cv[SUBLANES:SUBLANES + TL, :] = _dot(hn, wg_ref[c])
        wc = wcv_ref[c]
        conv = wc[FFN_CONV:FFN_CONV + 1, :]
        for j in range(FFN_CONV):
            off = SUBLANES - (FFN_CONV - 1) + j
            conv = conv + wc[j:j + 1, :] * cv[off:off + TL, :]
        cv[0:SUBLANES, :] = cv[TL:TL + SUBLANES, :]
        act = _gelu_tanh(conv) * _dot(hn, wu_ref[c])
        acc_ref[...] += _dot(act.astype(_BF16), wd_ref[c])
        return carry

    lax.fori_loop(0, N_FFN_CHUNKS, body, 0)
    o_ref[...] = x + _rms(acc_ref[...], gpost_ref[...])


def _ffn(x, gpre, gpost, wg, wu, wd, wcv):
    lp = x.shape[0]
    consts = (gpre, gpost, wg, wu, wd, wcv)
    return pl.pallas_call(
        _ffn_kernel,
        grid=(lp // TL,),
        in_specs=[pl.BlockSpec((TL, D_MODEL), lambda i: (i, 0))] + [_const_spec(c.shape) for c in consts],
        out_specs=pl.BlockSpec((TL, D_MODEL), lambda i: (i, 0)),
        out_shape=jax.ShapeDtypeStruct((lp, D_MODEL), _F32),
        scratch_shapes=[
            pltpu.VMEM((N_FFN_CHUNKS, TL + SUBLANES, FFN_CHUNK), _F32),
            pltpu.VMEM((TL, D_MODEL), _BF16),
            pltpu.VMEM((TL, D_MODEL), _F32),
        ],
        compiler_params=pltpu.CompilerParams(dimension_semantics=("arbitrary",),
                                             vmem_limit_bytes=VMEM_LIMIT),
        name="ffn",
    )(x, *consts)


def _s5_prep(lam_re, lam_im, b_re, b_im, c_re, c_im, d, log_dt):
    hp = lax.Precision.HIGHEST
    g_, p_, gh = SSM_GROUPS, SSM_STATE, SSM_GROUP
    dt = jnp.exp(log_dt)[:, None]
    ks = jnp.arange(SS + 1, dtype=_F32)[:, None, None]
    mag = jnp.exp(ks * (lam_re * dt)[None])
    ang = ks * (lam_im * dt)[None]
    ak_re, ak_im = mag * jnp.cos(ang), mag * jnp.sin(ang)
    nr, ni = ak_re[1] - 1.0, ak_im[1]
    den = lam_re * lam_re + lam_im * lam_im
    z_re = (nr * lam_re + ni * lam_im) / den
    z_im = (ni * lam_re - nr * lam_im) / den
    bb_re = z_re[..., None] * b_re - z_im[..., None] * b_im
    bb_im = z_re[..., None] * b_im + z_im[..., None] * b_re
    wk_re = ak_re[:SS, :, :, None] * bb_re[None] - ak_im[:SS, :, :, None] * bb_im[None]
    wk_im = ak_re[:SS, :, :, None] * bb_im[None] + ak_im[:SS, :, :, None] * bb_re[None]
    ca_re = c_re[None] * ak_re[:, :, None, :] - c_im[None] * ak_im[:, :, None, :]
    ca_im = c_re[None] * ak_im[:, :, None, :] + c_im[None] * ak_re[:, :, None, :]
    eye2 = jnp.eye(2, dtype=_F32)

    wrev_re = wk_re[::-1]
    wrev_im = wk_im[::-1]
    blk = jnp.stack([wrev_re, wrev_im], axis=0)
    blk = blk.reshape(2, SS, PAIRS, 2, p_, gh)
    bs = jnp.einsum('cigqph,ge->qighcep', blk.transpose(0, 1, 3, 2, 4, 5), eye2)
    bs = bs.reshape(PAIRS, PAIR_W, PAIR_W)

    cblk = jnp.stack([ca_re[1:], -ca_im[1:]], axis=0)
    cblk = cblk.reshape(2, SS, PAIRS, 2, gh, p_)
    cs = jnp.einsum('cjqghp,ge->qcgpjeh', cblk, eye2)
    cs = cs.reshape(PAIRS, PAIR_W, PAIR_W)

    kt = (jnp.einsum('tghp,gpk->tghk', ca_re[:SS], bb_re, precision=hp)
          - jnp.einsum('tghp,gpk->tghk', ca_im[:SS], bb_im, precision=hp))
    kt = kt.at[0].add(d[:, :, None] * jnp.eye(gh, dtype=_F32)[None])
    ii = jnp.arange(SS)
    tau = ii[None, :] - ii[:, None]
    kfull = jnp.where((tau >= 0)[:, :, None, None, None], kt[jnp.clip(tau, 0, SS - 1)], 0.0)
    kfull = kfull.reshape(SS, SS, PAIRS, 2, gh, gh)
    ds = jnp.einsum('ijqghk,ge->qigkjeh', kfull, eye2)
    ds = ds.reshape(PAIRS, PAIR_W, PAIR_W)

    a8_re = ak_re[SS].reshape(PAIRS * PAIR_STATE)
    a8_im = ak_im[SS].reshape(PAIRS * PAIR_STATE)
    a8 = jnp.broadcast_to(jnp.concatenate([a8_re, a8_im])[None], (SUBLANES, 2 * HALF))
    return bs.astype(_BF16), cs.astype(_BF16), ds.astype(_BF16), a8


def kernel(x, meta_tokens, g_mix_pre, g_mix_post, w_in, b_gates, ssm_lambda_re, ssm_lambda_im, ssm_b_re,
           ssm_b_im, ssm_c_re, ssm_c_im, ssm_d, ssm_log_dt, w_ssm_glu, w_qk_conv, b_qk_conv, g_head_norm,
           w_branch_ssm, w_branch_mlstm, w_out, g_ffn_pre, g_ffn_post, w_ffn_gate, w_ffn_up, w_ffn_conv,
           b_ffn_conv, w_ffn_down):
    bsz, seq, _ = x.shape
    assert bsz == 1
    depth = w_in.shape[0]
    used = FRONT_PAD + N_META + seq
    lp = -(-used // TL) * TL
    assert (lp // SS) % SUBLANES == 0

    h = jnp.concatenate([jnp.zeros((FRONT_PAD, D_MODEL), _F32), meta_tokens.astype(_F32), x[0],
                         jnp.zeros((lp - used, D_MODEL), _F32)], axis=0)

    c0 = SSM_WIDTH
    w = MLSTM_WIDTH
    for l in range(depth):
        wl = w_in[l]
        w_u = wl[:, :c0].astype(_BF16)
        w_qk = wl[:, c0:c0 + 2 * w].astype(_BF16)
        w_v = wl[:, c0 + 2 * w:c0 + 3 * w].astype(_BF16)
        w_o = wl[:, c0 + 3 * w:c0 + 4 * w].astype(_BF16)
        g0 = c0 + 4 * w
        w_g = wl[:, g0:g0 + 2 * MLSTM_HEADS]
        w_gc = jnp.pad(w_g, ((0, 0), (0, GATE_PAD - 2 * MLSTM_HEADS))).astype(_BF16)
        w_gr = w_g.T.astype(_BF16)
        w_ga = wl[:, g0 + 2 * MLSTM_HEADS:g0 + 2 * MLSTM_HEADS + D_MODEL].astype(_BF16)
        w_gb = wl[:, g0 + 2 * MLSTM_HEADS + D_MODEL:].astype(_BF16)
        b_gc = jnp.pad(b_gates[l], (0, GATE_PAD - 2 * MLSTM_HEADS))[None]
        b_gr = jnp.broadcast_to(b_gates[l][:, None], (2 * MLSTM_HEADS, LANES))
        w_cv = jnp.concatenate([w_qk_conv[l], b_qk_conv[l][None],
                                jnp.zeros((SUBLANES - QK_CONV - 1, 2 * w), _F32)], axis=0)

        bs, cs, ds, a8 = _s5_prep(ssm_lambda_re[l], ssm_lambda_im[l], ssm_b_re[l], ssm_b_im[l],
                                  ssm_c_re[l], ssm_c_im[l], ssm_d[l], ssm_log_dt[l])

        u = _s5_in(h, g_mix_pre[l][None], w_u)
        y5 = _s5_core(u.reshape(PAIRS, lp // SS, PAIR_W), bs, cs, ds, a8)
        y5 = y5.reshape(PAIRS, lp, PAIR_IN)
        h = _mixer(h, y5, g_mix_pre[l][None], g_mix_post[l][None], w_qk, w_v, w_o, w_ga, w_gb, w_gc, w_gr,
                   b_gc, b_gr, w_cv, g_head_norm[l][None], w_ssm_glu[l].astype(_BF16),
                   w_branch_ssm[l].astype(_BF16), w_branch_mlstm[l].astype(_BF16), w_out[l].astype(_BF16))

        wg = w_ffn_gate[l].reshape(D_MODEL, N_FFN_CHUNKS, FFN_CHUNK).transpose(1, 0, 2).astype(_BF16)
        wu = w_ffn_up[l].reshape(D_MODEL, N_FFN_CHUNKS, FFN_CHUNK).transpose(1, 0, 2).astype(_BF16)
        wd = w_ffn_down[l].reshape(N_FFN_CHUNKS, FFN_CHUNK, D_MODEL).astype(_BF16)
        wcv = jnp.concatenate([w_ffn_conv[l], b_ffn_conv[l][None],
                               jnp.zeros((SUBLANES - FFN_CONV - 1, FFN_DIM), _F32)], axis=0)
        wcv = wcv.reshape(SUBLANES, N_FFN_CHUNKS, FFN_CHUNK).transpose(1, 0, 2)
        h = _ffn(h, g_ffn_pre[l][None], g_ffn_post[l][None], wg, wu, wd, wcv)

    start = FRONT_PAD + N_META
    return h[start:start + seq][None]
```

```python
import functools
import math

import jax
import jax.numpy as jnp
from jax import lax
from jax.experimental import pallas as pl
from jax.experimental.pallas import tpu as pltpu

D_MODEL = 1024
N_META = 16
SSM_WIDTH = 512
SSM_GROUP = 16
SSM_GROUPS = 32
SSM_STATE = 64
MLSTM_WIDTH = 512
MLSTM_HEADS = 4
MLSTM_HEAD_DIM = 128
MLSTM_REF_CHUNK = 64
QK_CONV = 4
FFN_DIM = 2816
FFN_CONV = 3
NORM_EPS = 1e-6
PAD_LOG_INPUT_GATE = -1e4

LANES = 128
SUBLANES = 8
TL = 512
CH = 128
SS = 8
PAIRS = SSM_GROUPS // 2
PAIR_IN = 2 * SSM_GROUP
PAIR_W = SS * PAIR_IN
PAIR_STATE = 2 * SSM_STATE
PAIRS_PER_TILE = LANES // PAIR_IN
HALF = PAIRS * PAIR_STATE
FRONT_PAD = MLSTM_REF_CHUNK - N_META
FFN_CHUNK = 256
N_FFN_CHUNKS = FFN_DIM // FFN_CHUNK
GATE_PAD = LANES
VMEM_LIMIT = 56 * 1024 * 1024

_BF16 = jnp.bfloat16
_F32 = jnp.float32


def _rms(x, g):
    return x * lax.rsqrt(jnp.mean(x * x, axis=-1, keepdims=True) + NORM_EPS) * g


def _gelu_tanh(x):
    return 0.5 * x * (1.0 + jnp.tanh(math.sqrt(2.0 / math.pi) * (x + 0.044715 * (x * x * x))))


def _sigmoid(x):
    return 1.0 / (1.0 + jnp.exp(-x))


def _log_sigmoid(x):
    return jnp.minimum(x, 0.0) - jnp.log(1.0 + jnp.exp(-jnp.abs(x)))


def _dot(a, b):
    return jnp.dot(a, b, preferred_element_type=_F32)


def _layer_spec(shape, l):
    nd = len(shape) - 1
    return pl.BlockSpec((None,) + tuple(shape[1:]), lambda i, _l=l, _nd=nd: (_l,) + (0,) * _nd)


def _s5_in_kernel(x_ref, g_ref, w_ref, u_ref, us_ref):
    hn = _rms(x_ref[...], g_ref[...]).astype(_BF16)
    u = _dot(hn, w_ref[...])
    for t in range(SSM_WIDTH // LANES):
        us_ref[t] = u[:, t * LANES:(t + 1) * LANES]
    for j in range(SS):
        for t in range(SSM_WIDTH // LANES):
            rows = us_ref[t, pl.ds(j, TL // SS, stride=SS), :]
            for k in range(PAIRS_PER_TILE):
                u_ref[t * PAIRS_PER_TILE + k, :, j * PAIR_IN:(j + 1) * PAIR_IN] = (
                    rows[:, k * PAIR_IN:(k + 1) * PAIR_IN])


def _s5_in(x, g, w, l):
    lp = x.shape[0]
    return pl.pallas_call(
        _s5_in_kernel,
        grid=(lp // TL,),
        in_specs=[
            pl.BlockSpec((TL, D_MODEL), lambda i: (i, 0)),
            _layer_spec(g.shape, l),
            _layer_spec(w.shape, l),
        ],
        out_specs=pl.BlockSpec((PAIRS, TL // SS, PAIR_W), lambda i: (0, i, 0)),
        out_shape=jax.ShapeDtypeStruct((PAIRS, lp // SS, PAIR_W), _F32),
        scratch_shapes=[pltpu.VMEM((SSM_WIDTH // LANES, TL, LANES), _F32)],
        compiler_params=pltpu.CompilerParams(dimension_semantics=("parallel",),
                                             vmem_limit_bytes=VMEM_LIMIT),
        name="s5_in",
    )(x, g, w)


def _s5_core_kernel(rt, u_ref, bs_ref, cs_ref, ds_ref, a_ref, y_ref, vx_ref, st_ref):
    rb = rt // SUBLANES

    @pl.when(pl.program_id(0) == 0)
    def _():
        st_ref[...] = jnp.zeros_like(st_ref)

    for p in range(PAIRS):
        v = _dot(u_ref[p].astype(_BF16), bs_ref[p])
        vx_ref[:, :, p * PAIR_STATE:(p + 1) * PAIR_STATE] = (
            v[:, :PAIR_STATE].reshape(rb, SUBLANES, PAIR_STATE))
        vx_ref[:, :, HALF + p * PAIR_STATE:HALF + (p + 1) * PAIR_STATE] = (
            v[:, PAIR_STATE:].reshape(rb, SUBLANES, PAIR_STATE))

    a_re = a_ref[:, :HALF]
    a_im = a_ref[:, HALF:]

    def body(i, carry):
        s_re, s_im = carry
        for j in range(SUBLANES):
            v_re = vx_ref[i, j:j + 1, :HALF]
            v_im = vx_ref[i, j:j + 1, HALF:]
            vx_ref[i, j:j + 1, :HALF] = s_re[j:j + 1]
            vx_ref[i, j:j + 1, HALF:] = s_im[j:j + 1]
            n_re = a_re * s_re - a_im * s_im + v_re
            n_im = a_re * s_im + a_im * s_re + v_im
            s_re, s_im = n_re, n_im
        return s_re, s_im

    s_re, s_im = lax.fori_loop(0, rb, body, (st_ref[:, :HALF], st_ref[:, HALF:]))
    st_ref[:, :HALF] = s_re
    st_ref[:, HALF:] = s_im

    for p in range(PAIRS):
        x_re = vx_ref[:, :, p * PAIR_STATE:(p + 1) * PAIR_STATE].reshape(rt, PAIR_STATE)
        x_im = vx_ref[:, :, HALF + p * PAIR_STATE:HALF + (p + 1) * PAIR_STATE].reshape(rt, PAIR_STATE)
        xp = jnp.concatenate([x_re, x_im], axis=-1).astype(_BF16)
        y_ref[p] = _dot(xp, cs_ref[p]) + _dot(u_ref[p].astype(_BF16), ds_ref[p])


def _s5_core(u, bs, cs, ds, a8, l):
    rows = u.shape[1]
    rt = rows // 8 if (rows // 8) % SUBLANES == 0 else rows
    return pl.pallas_call(
        functools.partial(_s5_core_kernel, rt),
        grid=(rows // rt,),
        in_specs=[
            pl.BlockSpec((PAIRS, rt, PAIR_W), lambda i: (0, i, 0)),
            _layer_spec(bs.shape, l), _layer_spec(cs.shape, l), _layer_spec(ds.shape, l),
            _layer_spec(a8.shape, l),
        ],
        out_specs=pl.BlockSpec((PAIRS, rt, PAIR_W), lambda i: (0, i, 0)),
        out_shape=jax.ShapeDtypeStruct((PAIRS, rows, PAIR_W), _F32),
        scratch_shapes=[
            pltpu.VMEM((rt // SUBLANES, SUBLANES, 2 * HALF), _F32),
            pltpu.VMEM((SUBLANES, 2 * HALF), _F32),
        ],
        compiler_params=pltpu.CompilerParams(dimension_semantics=("arbitrary",),
                                             vmem_limit_bytes=VMEM_LIMIT),
        name="s5_core",
    )(u, bs, cs, ds, a8)


def _mixer_kernel(x_ref, y5_ref, gpre_ref, gpost_ref, wqk_ref, wv_ref, wo_ref, wga_ref, wgb_ref,
                  wgc_ref, wgr_ref, bgc_ref, bgr_ref, wcv_ref, ghead_ref, wglu_ref, wa_ref, wb_ref,
                  wout_ref, o_ref, cv_ref, ct_ref, n_ref, m_ref, yb_ref, y5s_ref):
    pid = pl.program_id(0)

    @pl.when(pid == 0)
    def _():
        cv_ref[0:SUBLANES, :] = jnp.zeros((SUBLANES, 2 * MLSTM_WIDTH), _F32)
        ct_ref[...] = jnp.zeros_like(ct_ref)
        n_ref[...] = jnp.zeros_like(n_ref)
        m_ref[...] = jnp.zeros_like(m_ref)

    x = x_ref[...]
    hn = _rms(x, gpre_ref[...]).astype(_BF16)

    row_c = pid * TL + lax.broadcasted_iota(jnp.int32, (TL, 1), 0)
    valid_c = row_c >= FRONT_PAD
    row_r = pid * TL + lax.broadcasted_iota(jnp.int32, (1, TL), 1)
    valid_r = row_r >= FRONT_PAD

    cv_ref[SUBLANES:SUBLANES + TL, :] = _dot(hn, wqk_ref[...])
    acc = wcv_ref[QK_CONV:QK_CONV + 1, :]
    for j in range(QK_CONV):
        off = SUBLANES - (QK_CONV - 1) + j
        acc = acc + wcv_ref[j:j + 1, :] * cv_ref[off:off + TL, :]
    cv_ref[0:SUBLANES, :] = cv_ref[TL:TL + SUBLANES, :]
    qk = jnp.where(valid_c, acc * _sigmoid(acc), 0.0)
    q_all = qk[:, :MLSTM_WIDTH].astype(_BF16)
    k_all = qk[:, MLSTM_WIDTH:] * (MLSTM_HEAD_DIM ** -0.5)
    v_all = jnp.where(valid_c, _dot(hn, wv_ref[...]), 0.0)
    o_all = _dot(hn, wo_ref[...])

    lane = lax.broadcasted_iota(jnp.int32, (1, GATE_PAD), 1)
    gc = _dot(hn, wgc_ref[...]) + bgc_ref[...]
    gc = jnp.where(lane < MLSTM_HEADS, gc, _log_sigmoid(gc))
    gc = jnp.where(valid_c, gc, jnp.where(lane < MLSTM_HEADS, PAD_LOG_INPUT_GATE, 0.0))
    sub = lax.broadcasted_iota(jnp.int32, (2 * MLSTM_HEADS, 1), 0)
    gr = lax.dot_general(wgr_ref[...], hn, (((1,), (1,)), ((), ())),
                         preferred_element_type=_F32) + bgr_ref[:, 0:1]
    gr = jnp.where(sub < MLSTM_HEADS, gr, _log_sigmoid(gr))
    gr = jnp.where(valid_r, gr, jnp.where(sub < MLSTM_HEADS, PAD_LOG_INPUT_GATE, 0.0))

    it = lax.broadcasted_iota(jnp.int32, (CH, CH), 0)
    js = lax.broadcasted_iota(jnp.int32, (CH, CH), 1)
    causal = it >= js
    tri_l = causal.astype(_F32)
    tri_u = (it <= js).astype(_F32)

    for c in range(TL // CH):
        r0 = c * CH
        b_c = jnp.dot(tri_l, gc[r0:r0 + CH, :], preferred_element_type=_F32,
                      precision=lax.Precision.HIGHEST)
        b_r = jnp.dot(gr[:, r0:r0 + CH], tri_u, preferred_element_type=_F32,
                      precision=lax.Precision.HIGHEST)
        for h in range(MLSTM_HEADS):
            hs = slice(h * MLSTM_HEAD_DIM, (h + 1) * MLSTM_HEAD_DIM)
            fl = MLSTM_HEADS + h
            bcol = b_c[:, fl:fl + 1]
            rcol = gc[r0:r0 + CH, h:h + 1] - bcol
            rrow = gr[h:h + 1, r0:r0 + CH] - b_r[fl:fl + 1, :]
            m_st = m_ref[h:h + 1, 0:1]
            rmat = jnp.where(causal, rrow, -jnp.inf)
            mcol = jnp.maximum(m_st, jnp.max(rmat, axis=-1, keepdims=True))
            pmat = jnp.exp(rmat - mcol)
            qc = q_all[r0:r0 + CH, hs]
            kc = k_all[r0:r0 + CH, hs]
            vc = v_all[r0:r0 + CH, hs]
            kcb = kc.astype(_BF16)
            vcb = vc.astype(_BF16)
            s = lax.dot_general(qc, kcb, (((1,), (1,)), ((), ())),
                                preferred_element_type=_F32) * pmat
            w_inter = jnp.exp(m_st - mcol)
            ct = ct_ref[h]
            nrow = n_ref[h:h + 1, :]
            num = _dot(s.astype(_BF16), vcb) + w_inter * _dot(qc, ct.astype(_BF16))
            qn = jnp.sum(qc.astype(_F32) * nrow, axis=-1, keepdims=True)
            den = jnp.sum(s, axis=-1, keepdims=True) + w_inter * qn
            hout = num / jnp.maximum(jnp.abs(den), jnp.exp(-(bcol + mcol)))
            hout = hout * lax.rsqrt(jnp.mean(hout * hout, axis=-1, keepdims=True) + NORM_EPS)
            hout = hout * ghead_ref[:, hs] * _sigmoid(o_all[r0:r0 + CH, hs])
            yb_ref[r0:r0 + CH, hs] = hout
            m_last = mcol[CH - 1:CH, :]
            g_tot = bcol[CH - 1:CH, :]
            decay = jnp.exp(m_st - m_last)
            kw = kc * jnp.exp(rcol - m_last)
            ct_ref[h] = decay * ct + _dot(kw.T.astype(_BF16), vcb)
            n_ref[h:h + 1, :] = decay * nrow + jnp.sum(kw, axis=0, keepdims=True)
            m_ref[h:h + 1, :] = jnp.broadcast_to(g_tot + m_last, (1, LANES))

    for j in range(SS):
        for t in range(SSM_WIDTH // LANES):
            yj = jnp.concatenate([y5_ref[t * PAIRS_PER_TILE + k, :, j * PAIR_IN:(j + 1) * PAIR_IN]
                                  for k in range(PAIRS_PER_TILE)], axis=-1)
            y5s_ref[t, pl.ds(j, TL // SS, stride=SS), :] = yj
    ya = _gelu_tanh(jnp.concatenate([y5s_ref[t] for t in range(SSM_WIDTH // LANES)], axis=-1))
    ya = ya * _sigmoid(_dot(ya.astype(_BF16), wglu_ref[...]))

    merged = (_sigmoid(_dot(hn, wga_ref[...])) * _dot(ya.astype(_BF16), wa_ref[...])
              + _sigmoid(_dot(hn, wgb_ref[...])) * _dot(yb_ref[...].astype(_BF16), wb_ref[...]))
    o_ref[...] = x + _rms(_dot(merged.astype(_BF16), wout_ref[...]), gpost_ref[...])


def _mixer(x, y5, consts, l):
    lp = x.shape[0]
    return pl.pallas_call(
        _mixer_kernel,
        grid=(lp // TL,),
        in_specs=[
            pl.BlockSpec((TL, D_MODEL), lambda i: (i, 0)),
            pl.BlockSpec((PAIRS, TL // SS, PAIR_W), lambda i: (0, i, 0)),
        ] + [_layer_spec(c.shape, l) for c in consts],
        out_specs=pl.BlockSpec((TL, D_MODEL), lambda i: (i, 0)),
        out_shape=jax.ShapeDtypeStruct((lp, D_MODEL), _F32),
        scratch_shapes=[
            pltpu.VMEM((TL + SUBLANES, 2 * MLSTM_WIDTH), _F32),
            pltpu.VMEM((MLSTM_HEADS, MLSTM_HEAD_DIM, MLSTM_HEAD_DIM), _F32),
            pltpu.VMEM((SUBLANES, MLSTM_HEAD_DIM), _F32),
            pltpu.VMEM((SUBLANES, LANES), _F32),
            pltpu.VMEM((TL, MLSTM_WIDTH), _F32),
            pltpu.VMEM((SSM_WIDTH // LANES, TL, LANES), _F32),
        ],
        compiler_params=pltpu.CompilerParams(dimension_semantics=("arbitrary",),
                                             vmem_limit_bytes=VMEM_LIMIT),
        name="mixer",
    )(x, y5, *consts)


def _ffn_kernel(x_ref, gpre_ref, gpost_ref, wg_ref, wu_ref, wd_ref, wcv_ref, o_ref, cv_ref):
    @pl.when(pl.program_id(0) == 0)
    def _():
        cv_ref[0:SUBLANES, :] = jnp.zeros((SUBLANES, FFN_DIM), _F32)

    x = x_ref[...]
    hn = _rms(x, gpre_ref[...]).astype(_BF16)
    acc = None
    for c in range(N_FFN_CHUNKS):
        cs = slice(c * FFN_CHUNK, (c + 1) * FFN_CHUNK)
        cv_ref[SUBLANES:SUBLANES + TL, cs] = _dot(hn, wg_ref[:, cs])
        conv = wcv_ref[FFN_CONV:FFN_CONV + 1, cs]
        for j in range(FFN_CONV):
            off = SUBLANES - (FFN_CONV - 1) + j
            conv = conv + wcv_ref[j:j + 1, cs] * cv_ref[off:off + TL, cs]
        cv_ref[0:SUBLANES, cs] = cv_ref[TL:TL + SUBLANES, cs]
        act = _gelu_tanh(conv) * _dot(hn, wu_ref[:, cs])
        part = _dot(act.astype(_BF16), wd_ref[cs, :])
        acc = part if acc is None else acc + part
    o_ref[...] = x + _rms(acc, gpost_ref[...])


def _ffn(x, consts, l):
    lp = x.shape[0]
    return pl.pallas_call(
        _ffn_kernel,
        grid=(lp // TL,),
        in_specs=[pl.BlockSpec((TL, D_MODEL), lambda i: (i, 0))] + [_layer_spec(c.shape, l) for c in consts],
        out_specs=pl.BlockSpec((TL, D_MODEL), lambda i: (i, 0)),
        out_shape=jax.ShapeDtypeStruct((lp, D_MODEL), _F32),
        scratch_shapes=[pltpu.VMEM((TL + SUBLANES, FFN_DIM), _F32)],
        compiler_params=pltpu.CompilerParams(dimension_semantics=("arbitrary",),
                                             vmem_limit_bytes=VMEM_LIMIT),
        name="ffn",
    )(x, *consts)


def _s5_prep(lam_re, lam_im, b_re, b_im, c_re, c_im, d, log_dt):
    hp = lax.Precision.HIGHEST
    p_, gh = SSM_STATE, SSM_GROUP
    dt = jnp.exp(log_dt)[:, None]
    ks = jnp.arange(SS + 1, dtype=_F32)[:, None, None]
    mag = jnp.exp(ks * (lam_re * dt)[None])
    ang = ks * (lam_im * dt)[None]
    ak_re, ak_im = mag * jnp.cos(ang), mag * jnp.sin(ang)
    nr, ni = ak_re[1] - 1.0, ak_im[1]
    den = lam_re * lam_re + lam_im * lam_im
    z_re = (nr * lam_re + ni * lam_im) / den
    z_im = (ni * lam_re - nr * lam_im) / den
    bb_re = z_re[..., None] * b_re - z_im[..., None] * b_im
    bb_im = z_re[..., None] * b_im + z_im[..., None] * b_re
    wk_re = ak_re[:SS, :, :, None] * bb_re[None] - ak_im[:SS, :, :, None] * bb_im[None]
    wk_im = ak_re[:SS, :, :, None] * bb_im[None] + ak_im[:SS, :, :, None] * bb_re[None]
    ca_re = c_re[None] * ak_re[:, :, None, :] - c_im[None] * ak_im[:, :, None, :]
    ca_im = c_re[None] * ak_im[:, :, None, :] + c_im[None] * ak_re[:, :, None, :]
    eye2 = jnp.eye(2, dtype=_F32)

    blk = jnp.stack([wk_re[::-1], wk_im[::-1]], axis=0)
    blk = blk.reshape(2, SS, PAIRS, 2, p_, gh)
    bs = jnp.einsum('ciqgph,ge->qighcep', blk, eye2)
    bs = bs.reshape(PAIRS, PAIR_W, PAIR_W)

    cblk = jnp.stack([ca_re[1:], -ca_im[1:]], axis=0)
    cblk = cblk.reshape(2, SS, PAIRS, 2, gh, p_)
    cs = jnp.einsum('cjqghp,ge->qcgpjeh', cblk, eye2)
    cs = cs.reshape(PAIRS, PAIR_W, PAIR_W)

    kt = (jnp.einsum('tghp,gpk->tghk', ca_re[:SS], bb_re, precision=hp)
          - jnp.einsum('tghp,gpk->tghk', ca_im[:SS], bb_im, precision=hp))
    kt = kt.at[0].add(d[:, :, None] * jnp.eye(gh, dtype=_F32)[None])
    ii = jnp.arange(SS)
    tau = ii[None, :] - ii[:, None]
    kfull = jnp.where((tau >= 0)[:, :, None, None, None], kt[jnp.clip(tau, 0, SS - 1)], 0.0)
    kfull = kfull.reshape(SS, SS, PAIRS, 2, gh, gh)
    ds = jnp.einsum('ijqghk,ge->qigkjeh', kfull, eye2)
    ds = ds.reshape(PAIRS, PAIR_W, PAIR_W)

    a8 = jnp.concatenate([ak_re[SS].reshape(HALF), ak_im[SS].reshape(HALF)])
    a8 = jnp.broadcast_to(a8[None], (SUBLANES, 2 * HALF))
    return bs.astype(_BF16), cs.astype(_BF16), ds.astype(_BF16), a8


def kernel(x, meta_tokens, g_mix_pre, g_mix_post, w_in, b_gates, ssm_lambda_re, ssm_lambda_im, ssm_b_re,
           ssm_b_im, ssm_c_re, ssm_c_im, ssm_d, ssm_log_dt, w_ssm_glu, w_qk_conv, b_qk_conv, g_head_norm,
           w_branch_ssm, w_branch_mlstm, w_out, g_ffn_pre, g_ffn_post, w_ffn_gate, w_ffn_up, w_ffn_conv,
           b_ffn_conv, w_ffn_down):
    bsz, seq, _ = x.shape
    assert bsz == 1
    depth = w_in.shape[0]
    used = FRONT_PAD + N_META + seq
    lp = -(-used // TL) * TL
    assert (lp // SS) % SUBLANES == 0

    h = jnp.concatenate([jnp.zeros((FRONT_PAD, D_MODEL), _F32), meta_tokens.astype(_F32), x[0],
                         jnp.zeros((lp - used, D_MODEL), _F32)], axis=0)

    nh2 = 2 * MLSTM_HEADS
    c0, w = SSM_WIDTH, MLSTM_WIDTH
    g0 = c0 + 4 * w
    row = lambda a: a[:, None, :]
    w_u = w_in[:, :, :c0].astype(_BF16)
    w_qk = w_in[:, :, c0:c0 + 2 * w].astype(_BF16)
    w_v = w_in[:, :, c0 + 2 * w:c0 + 3 * w].astype(_BF16)
    w_o = w_in[:, :, c0 + 3 * w:g0].astype(_BF16)
    w_g = w_in[:, :, g0:g0 + nh2]
    w_gc = jnp.pad(w_g, ((0, 0), (0, 0), (0, GATE_PAD - nh2))).astype(_BF16)
    w_gr = jnp.swapaxes(w_g, 1, 2).astype(_BF16)
    w_ga = w_in[:, :, g0 + nh2:g0 + nh2 + D_MODEL].astype(_BF16)
    w_gb = w_in[:, :, g0 + nh2 + D_MODEL:].astype(_BF16)
    b_gc = row(jnp.pad(b_gates, ((0, 0), (0, GATE_PAD - nh2))))
    b_gr = jnp.broadcast_to(b_gates[:, :, None], (depth, nh2, LANES))
    w_cv = jnp.concatenate([w_qk_conv, row(b_qk_conv),
                            jnp.zeros((depth, SUBLANES - QK_CONV - 1, 2 * w), _F32)], axis=1)
    mixer_consts = (row(g_mix_pre), row(g_mix_post), w_qk, w_v, w_o, w_ga, w_gb, w_gc, w_gr, b_gc, b_gr, w_cv,
                    row(g_head_norm), w_ssm_glu.astype(_BF16), w_branch_ssm.astype(_BF16),
                    w_branch_mlstm.astype(_BF16), w_out.astype(_BF16))
    f_cv = jnp.concatenate([w_ffn_conv, row(b_ffn_conv),
                            jnp.zeros((depth, SUBLANES - FFN_CONV - 1, FFN_DIM), _F32)], axis=1)
    ffn_consts = (row(g_ffn_pre), row(g_ffn_post), w_ffn_gate.astype(_BF16), w_ffn_up.astype(_BF16),
                  w_ffn_down.astype(_BF16), f_cv)
    bs, cs, ds, a8 = jax.vmap(_s5_prep)(ssm_lambda_re, ssm_lambda_im, ssm_b_re, ssm_b_im,
                                        ssm_c_re, ssm_c_im, ssm_d, ssm_log_dt)
    g_pre = row(g_mix_pre)

    for l in range(depth):
        u = _s5_in(h, g_pre, w_u, l)
        y5 = _s5_core(u, bs, cs, ds, a8, l)
        h = _mixer(h, y5, mixer_consts, l)
        h = _ffn(h, ffn_consts, l)

    start = FRONT_PAD + N_META
    return h[start:start + seq][None]
```

```python
import functools
import math

import jax
import jax.numpy as jnp
from jax import lax
from jax.experimental import pallas as pl
from jax.experimental.pallas import tpu as pltpu

D_MODEL = 1024
N_META = 16
SSM_WIDTH = 512
SSM_GROUP = 16
SSM_GROUPS = 32
SSM_STATE = 64
MLSTM_WIDTH = 512
MLSTM_HEADS = 4
MLSTM_HEAD_DIM = 128
MLSTM_REF_CHUNK = 64
QK_CONV = 4
FFN_DIM = 2816
FFN_CONV = 3
NORM_EPS = 1e-6
PAD_LOG_INPUT_GATE = -1e4

LANES = 128
SUBLANES = 8
TL = 512
CH = 128
SS = 8
PAIRS = SSM_GROUPS // 2
PAIR_IN = 2 * SSM_GROUP
PAIR_W = SS * PAIR_IN
PAIR_STATE = 2 * SSM_STATE
PAIRS_PER_TILE = LANES // PAIR_IN
HALF = PAIRS * PAIR_STATE
FRONT_PAD = MLSTM_REF_CHUNK - N_META
MIX_BLOCK = 256
FFN_CHUNK = 256
N_FFN_CHUNKS = FFN_DIM // FFN_CHUNK
GATE_PAD = LANES
VMEM_LIMIT = 56 * 1024 * 1024

_BF16 = jnp.bfloat16
_F32 = jnp.float32


def _rms(x, g):
    return x * lax.rsqrt(jnp.mean(x * x, axis=-1, keepdims=True) + NORM_EPS) * g


def _gelu_tanh(x):
    return 0.5 * x * (1.0 + jnp.tanh(math.sqrt(2.0 / math.pi) * (x + 0.044715 * (x * x * x))))


def _sigmoid(x):
    return 0.5 + 0.5 * jnp.tanh(0.5 * x)


def _log_sigmoid(x):
    return jnp.minimum(x, 0.0) - jnp.log(1.0 + jnp.exp(-jnp.abs(x)))


def _dot(a, b):
    return jnp.dot(a, b, preferred_element_type=_F32)


def _layer_spec(shape, l):
    nd = len(shape) - 1
    return pl.BlockSpec((None,) + tuple(shape[1:]), lambda i, _l=l, _nd=nd: (_l,) + (0,) * _nd)


def _s5_in_kernel(x_ref, g_ref, w_ref, u_ref, us_ref):
    hn = _rms(x_ref[...], g_ref[...]).astype(_BF16)
    u = _dot(hn, w_ref[...])
    for t in range(SSM_WIDTH // LANES):
        us_ref[t] = u[:, t * LANES:(t + 1) * LANES]
    for j in range(SS):
        for t in range(SSM_WIDTH // LANES):
            rows = us_ref[t, pl.ds(j, TL // SS, stride=SS), :]
            for k in range(PAIRS_PER_TILE):
                u_ref[t * PAIRS_PER_TILE + k, :, j * PAIR_IN:(j + 1) * PAIR_IN] = (
                    rows[:, k * PAIR_IN:(k + 1) * PAIR_IN])


def _s5_in(x, g, w, l):
    lp = x.shape[0]
    return pl.pallas_call(
        _s5_in_kernel,
        grid=(lp // TL,),
        in_specs=[
            pl.BlockSpec((TL, D_MODEL), lambda i: (i, 0)),
            _layer_spec(g.shape, l),
            _layer_spec(w.shape, l),
        ],
        out_specs=pl.BlockSpec((PAIRS, TL // SS, PAIR_W), lambda i: (0, i, 0)),
        out_shape=jax.ShapeDtypeStruct((PAIRS, lp // SS, PAIR_W), _F32),
        scratch_shapes=[pltpu.VMEM((SSM_WIDTH // LANES, TL, LANES), _F32)],
        compiler_params=pltpu.CompilerParams(dimension_semantics=("parallel",),
                                             vmem_limit_bytes=VMEM_LIMIT),
        name="s5_in",
    )(x, g, w)


def _s5_core_kernel(rt, u_ref, bs_ref, cs_ref, ds_ref, a_ref, y_ref, vx_ref, st_ref):
    rb = rt // SUBLANES

    @pl.when(pl.program_id(0) == 0)
    def _():
        st_ref[...] = jnp.zeros_like(st_ref)

    for p in range(PAIRS):
        v = lax.dot_general(u_ref[p].astype(_BF16), bs_ref[p], (((1,), (1,)), ((), ())),
                            preferred_element_type=_F32)
        vx_ref[:, :, p * PAIR_STATE:(p + 1) * PAIR_STATE] = (
            v[:, :PAIR_STATE].reshape(rb, SUBLANES, PAIR_STATE))
        vx_ref[:, :, HALF + p * PAIR_STATE:HALF + (p + 1) * PAIR_STATE] = (
            v[:, PAIR_STATE:].reshape(rb, SUBLANES, PAIR_STATE))

    a_re = a_ref[:, :HALF]
    a_im = a_ref[:, HALF:]

    def body(i, carry):
        s_re, s_im = carry
        v_re = vx_ref[i, :, :HALF]
        v_im = vx_ref[i, :, HALF:]
        for j in range(SUBLANES):
            vx_ref[i, j:j + 1, :HALF] = s_re[j:j + 1]
            vx_ref[i, j:j + 1, HALF:] = s_im[j:j + 1]
            n_re = a_re * s_re - a_im * s_im + v_re
            n_im = a_re * s_im + a_im * s_re + v_im
            s_re = pltpu.roll(n_re, 1, 0)
            s_im = pltpu.roll(n_im, 1, 0)
        return s_re, s_im

    s_re, s_im = lax.fori_loop(0, rb, body, (st_ref[:, :HALF], st_ref[:, HALF:]))
    st_ref[:, :HALF] = s_re
    st_ref[:, HALF:] = s_im

    for p in range(PAIRS):
        x_re = vx_ref[:, :, p * PAIR_STATE:(p + 1) * PAIR_STATE].reshape(rt, PAIR_STATE)
        x_im = vx_ref[:, :, HALF + p * PAIR_STATE:HALF + (p + 1) * PAIR_STATE].reshape(rt, PAIR_STATE)
        xp = jnp.concatenate([x_re, x_im], axis=-1).astype(_BF16)
        y_ref[p] = _dot(xp, cs_ref[p]) + _dot(u_ref[p].astype(_BF16), ds_ref[p])


def _s5_core(u, bs, cs, ds, a8, l):
    rows = u.shape[1]
    rt = rows // 8 if (rows // 8) % SUBLANES == 0 else rows
    return pl.pallas_call(
        functools.partial(_s5_core_kernel, rt),
        grid=(rows // rt,),
        in_specs=[
            pl.BlockSpec((PAIRS, rt, PAIR_W), lambda i: (0, i, 0)),
            _layer_spec(bs.shape, l), _layer_spec(cs.shape, l), _layer_spec(ds.shape, l),
            _layer_spec(a8.shape, l),
        ],
        out_specs=pl.BlockSpec((PAIRS, rt, PAIR_W), lambda i: (0, i, 0)),
        out_shape=jax.ShapeDtypeStruct((PAIRS, rows, PAIR_W), _F32),
        scratch_shapes=[
            pltpu.VMEM((rt // SUBLANES, SUBLANES, 2 * HALF), _F32),
            pltpu.VMEM((SUBLANES, 2 * HALF), _F32),
        ],
        compiler_params=pltpu.CompilerParams(dimension_semantics=("arbitrary",),
                                             vmem_limit_bytes=VMEM_LIMIT),
        name="s5_core",
    )(u, bs, cs, ds, a8)


def _mixer_kernel(x_ref, y5_ref, gpre_ref, gpost_ref, wqk_ref, wv_ref, wo_ref, wga_ref, wgb_ref,
                  wgc_ref, wgr_ref, bgc_ref, bgr_ref, wcv_ref, ghead_ref, wglu_ref, wa_ref, wb_ref,
                  wout_ref, o_ref, cv_ref, st_ref, m_ref, yb_ref, y5s_ref, q_ref, k_ref, v_ref,
                  so_ref, sga_ref, sgb_ref):
    pid = pl.program_id(0)

    @pl.when(pid == 0)
    def _():
        cv_ref[0:SUBLANES, :] = jnp.zeros((SUBLANES, 2 * MLSTM_WIDTH), _F32)
        st_ref[...] = jnp.zeros_like(st_ref)
        m_ref[...] = jnp.zeros_like(m_ref)
        for h in range(MLSTM_HEADS):
            v_ref[h, :, MLSTM_HEAD_DIM:] = jnp.ones((TL, MLSTM_HEAD_DIM), _BF16)

    x = x_ref[...]
    hn = _rms(x, gpre_ref[...]).astype(_BF16)

    row_c = pid * TL + lax.broadcasted_iota(jnp.int32, (TL, 1), 0)
    valid_c = row_c >= FRONT_PAD
    row_r = pid * TL + lax.broadcasted_iota(jnp.int32, (1, TL), 1)
    valid_r = row_r >= FRONT_PAD

    cv_ref[SUBLANES:SUBLANES + TL, :] = _dot(hn, wqk_ref[...])

    def conv_block(b):
        cols = slice(b * MIX_BLOCK, (b + 1) * MIX_BLOCK)
        acc = wcv_ref[QK_CONV:QK_CONV + 1, cols]
        for j in range(QK_CONV):
            off = SUBLANES - (QK_CONV - 1) + j
            acc = acc + wcv_ref[j:j + 1, cols] * cv_ref[off:off + TL, cols]
        cv_ref[0:SUBLANES, cols] = cv_ref[TL:TL + SUBLANES, cols]
        qk = jnp.where(valid_c, acc * _sigmoid(acc), 0.0)
        if b < MLSTM_WIDTH // MIX_BLOCK:
            q_ref[:, cols] = qk.astype(_BF16)
        else:
            kcols = slice(b * MIX_BLOCK - MLSTM_WIDTH, (b + 1) * MIX_BLOCK - MLSTM_WIDTH)
            k_ref[:, kcols] = qk * (MLSTM_HEAD_DIM ** -0.5)

    conv_block(0)
    v_all = jnp.where(valid_c, _dot(hn, wv_ref[...]), 0.0).astype(_BF16)
    for h in range(MLSTM_HEADS):
        v_ref[h, :, :MLSTM_HEAD_DIM] = v_all[:, h * MLSTM_HEAD_DIM:(h + 1) * MLSTM_HEAD_DIM]
    conv_block(1)
    so_ref[...] = _sigmoid(_dot(hn, wo_ref[...]))
    conv_block(2)
    lane = lax.broadcasted_iota(jnp.int32, (1, GATE_PAD), 1)
    gc = _dot(hn, wgc_ref[...]) + bgc_ref[...]
    gc = jnp.where(lane < MLSTM_HEADS, gc, _log_sigmoid(gc))
    gc = jnp.where(valid_c, gc, jnp.where(lane < MLSTM_HEADS, PAD_LOG_INPUT_GATE, 0.0))
    sub = lax.broadcasted_iota(jnp.int32, (2 * MLSTM_HEADS, 1), 0)
    gr = lax.dot_general(wgr_ref[...], hn, (((1,), (1,)), ((), ())),
                         preferred_element_type=_F32) + bgr_ref[:, 0:1]
    gr = jnp.where(sub < MLSTM_HEADS, gr, _log_sigmoid(gr))
    gr = jnp.where(valid_r, gr, jnp.where(sub < MLSTM_HEADS, PAD_LOG_INPUT_GATE, 0.0))
    conv_block(3)

    it = lax.broadcasted_iota(jnp.int32, (CH, CH), 0)
    js = lax.broadcasted_iota(jnp.int32, (CH, CH), 1)
    causal = it >= js
    tri_l = causal.astype(_F32)
    tri_u = (it <= js).astype(_F32)

    def regroup_s5(j):
        for t in range(SSM_WIDTH // LANES):
            yj = jnp.concatenate([y5_ref[t * PAIRS_PER_TILE + k, :, j * PAIR_IN:(j + 1) * PAIR_IN]
                                  for k in range(PAIRS_PER_TILE)], axis=-1)
            y5s_ref[t, pl.ds(j, TL // SS, stride=SS), :] = yj

    n_chunks = TL // CH
    for c in range(n_chunks):
        r0 = c * CH
        gcols = slice(c * (D_MODEL // n_chunks), (c + 1) * (D_MODEL // n_chunks))
        sga_ref[:, gcols] = _sigmoid(_dot(hn, wga_ref[:, gcols]))
        sgb_ref[:, gcols] = _sigmoid(_dot(hn, wgb_ref[:, gcols]))
        for j in range(c * SS // n_chunks, (c + 1) * SS // n_chunks):
            regroup_s5(j)
        hp = lax.Precision.HIGHEST
        gcc = gc[r0:r0 + CH, :]
        b_c = jnp.dot(tri_l, gcc, preferred_element_type=_F32, precision=hp)
        b_r = jnp.dot(gr[:, r0:r0 + CH], tri_u, preferred_element_type=_F32, precision=hp)
        for h in range(MLSTM_HEADS):
            hs = slice(h * MLSTM_HEAD_DIM, (h + 1) * MLSTM_HEAD_DIM)
            fl = MLSTM_HEADS + h
            bcol = b_c[:, fl:fl + 1]
            rcol = gcc[:, h:h + 1] - bcol
            rrow = gr[h:h + 1, r0:r0 + CH] - b_r[fl:fl + 1, :]
            m_st = m_ref[h:h + 1, :]
            rmat = jnp.where(causal, rrow, -jnp.inf)
            mc1 = jnp.maximum(m_st[:, 0:1], jnp.max(rmat, axis=-1, keepdims=True))
            mcol = jnp.broadcast_to(mc1, (CH, LANES))
            floor = jnp.broadcast_to(jnp.exp(-(bcol + mc1)), (CH, LANES))
            pmat = jnp.exp(rmat - mcol)
            w_inter = jnp.exp(m_st - mcol)
            qc = q_ref[r0:r0 + CH, hs]
            kc = k_ref[r0:r0 + CH, hs]
            va = v_ref[h, r0:r0 + CH, :]
            s = lax.dot_general(qc, kc.astype(_BF16), (((1,), (1,)), ((), ())),
                                preferred_element_type=_F32) * pmat
            st = st_ref[h]
            sv = _dot(s.astype(_BF16), va)
            qs = _dot(qc, st.astype(_BF16))
            num = sv[:, :MLSTM_HEAD_DIM] + w_inter * qs[:, :MLSTM_HEAD_DIM]
            den = sv[:, MLSTM_HEAD_DIM:] + w_inter * qs[:, MLSTM_HEAD_DIM:]
            hout = num / jnp.maximum(jnp.abs(den), floor)
            hout = hout * lax.rsqrt(jnp.mean(hout * hout, axis=-1, keepdims=True) + NORM_EPS)
            hout = hout * ghead_ref[:, hs] * so_ref[r0:r0 + CH, hs]
            yb_ref[r0:r0 + CH, hs] = hout.astype(_BF16)
            m_last = mcol[CH - 1:CH, :]
            decay = jnp.exp(m_st - m_last)
            kw = kc * jnp.exp(rcol - mc1[CH - 1:CH, :])
            st_ref[h] = jnp.concatenate([decay, decay], axis=-1) * st + _dot(kw.T.astype(_BF16), va)
            m_ref[h:h + 1, :] = jnp.broadcast_to(bcol[CH - 1:CH, :], (1, LANES)) + m_last

    ya = _gelu_tanh(jnp.concatenate([y5s_ref[t] for t in range(SSM_WIDTH // LANES)], axis=-1))
    ya = ya * _sigmoid(_dot(ya.astype(_BF16), wglu_ref[...]))

    merged = (sga_ref[...] * _dot(ya.astype(_BF16), wa_ref[...])
              + sgb_ref[...] * _dot(yb_ref[...], wb_ref[...]))
    o_ref[...] = x + _rms(_dot(merged.astype(_BF16), wout_ref[...]), gpost_ref[...])


def _mixer(x, y5, consts, l):
    lp = x.shape[0]
    return pl.pallas_call(
        _mixer_kernel,
        grid=(lp // TL,),
        in_specs=[
            pl.BlockSpec((TL, D_MODEL), lambda i: (i, 0)),
            pl.BlockSpec((PAIRS, TL // SS, PAIR_W), lambda i: (0, i, 0)),
        ] + [_layer_spec(c.shape, l) for c in consts],
        out_specs=pl.BlockSpec((TL, D_MODEL), lambda i: (i, 0)),
        out_shape=jax.ShapeDtypeStruct((lp, D_MODEL), _F32),
        scratch_shapes=[
            pltpu.VMEM((TL + SUBLANES, 2 * MLSTM_WIDTH), _F32),
            pltpu.VMEM((MLSTM_HEADS, MLSTM_HEAD_DIM, 2 * MLSTM_HEAD_DIM), _F32),
            pltpu.VMEM((SUBLANES, LANES), _F32),
            pltpu.VMEM((TL, MLSTM_WIDTH), _BF16),
            pltpu.VMEM((SSM_WIDTH // LANES, TL, LANES), _F32),
            pltpu.VMEM((TL, MLSTM_WIDTH), _BF16),
            pltpu.VMEM((TL, MLSTM_WIDTH), _F32),
            pltpu.VMEM((MLSTM_HEADS, TL, 2 * MLSTM_HEAD_DIM), _BF16),
            pltpu.VMEM((TL, MLSTM_WIDTH), _F32),
            pltpu.VMEM((TL, D_MODEL), _F32),
            pltpu.VMEM((TL, D_MODEL), _F32),
        ],
        compiler_params=pltpu.CompilerParams(dimension_semantics=("arbitrary",),
                                             vmem_limit_bytes=VMEM_LIMIT),
        name="mixer",
    )(x, y5, *consts)


def _ffn_kernel(x_ref, gpre_ref, gpost_ref, wg_ref, wu_ref, wd_ref, wcv_ref, o_ref, cv_ref):
    @pl.when(pl.program_id(0) == 0)
    def _():
        cv_ref[0:SUBLANES, :] = jnp.zeros((SUBLANES, FFN_DIM), _F32)

    x = x_ref[...]
    hn = _rms(x, gpre_ref[...]).astype(_BF16)
    chunk = lambda c: slice(c * FFN_CHUNK, (c + 1) * FFN_CHUNK)
    acc = None
    cv_ref[SUBLANES:SUBLANES + TL, chunk(0)] = _dot(hn, wg_ref[:, chunk(0)])
    up_next = _dot(hn, wu_ref[:, chunk(0)])
    for c in range(N_FFN_CHUNKS):
        cs = chunk(c)
        up = up_next
        if c + 1 < N_FFN_CHUNKS:
            cv_ref[SUBLANES:SUBLANES + TL, chunk(c + 1)] = _dot(hn, wg_ref[:, chunk(c + 1)])
            up_next = _dot(hn, wu_ref[:, chunk(c + 1)])
        conv = wcv_ref[FFN_CONV:FFN_CONV + 1, cs]
        for j in range(FFN_CONV):
            off = SUBLANES - (FFN_CONV - 1) + j
            conv = conv + wcv_ref[j:j + 1, cs] * cv_ref[off:off + TL, cs]
        cv_ref[0:SUBLANES, cs] = cv_ref[TL:TL + SUBLANES, cs]
        act = _gelu_tanh(conv) * up
        part = _dot(act.astype(_BF16), wd_ref[cs, :])
        acc = part if acc is None else acc + part
    o_ref[...] = x + _rms(acc, gpost_ref[...])


def _ffn(x, consts, l):
    lp = x.shape[0]
    return pl.pallas_call(
        _ffn_kernel,
        grid=(lp // TL,),
        in_specs=[pl.BlockSpec((TL, D_MODEL), lambda i: (i, 0))] + [_layer_spec(c.shape, l) for c in consts],
        out_specs=pl.BlockSpec((TL, D_MODEL), lambda i: (i, 0)),
        out_shape=jax.ShapeDtypeStruct((lp, D_MODEL), _F32),
        scratch_shapes=[pltpu.VMEM((TL + SUBLANES, FFN_DIM), _F32)],
        compiler_params=pltpu.CompilerParams(dimension_semantics=("arbitrary",),
                                             vmem_limit_bytes=VMEM_LIMIT),
        name="ffn",
    )(x, *consts)


def _s5_prep(lam_re, lam_im, b_re, b_im, c_re, c_im, d, log_dt):
    hp = lax.Precision.HIGHEST
    gh, npow = SSM_GROUP, SS + 1
    lr = lam_re.reshape(PAIRS, PAIR_STATE)
    li = lam_im.reshape(PAIRS, PAIR_STATE)
    dt = jnp.repeat(jnp.exp(log_dt), SSM_STATE).reshape(PAIRS, PAIR_STATE)
    ks = jnp.arange(npow, dtype=_F32)
    mag = jnp.exp((lr * dt)[..., None] * ks)
    ang = (li * dt)[..., None] * ks
    ak_re, ak_im = mag * jnp.cos(ang), mag * jnp.sin(ang)
    nr, ni = ak_re[..., 1] - 1.0, ak_im[..., 1]
    den = lr * lr + li * li
    z_re = (nr * lr + ni * li) / den
    z_im = (ni * lr - nr * li) / den
    bn_re = b_re.reshape(PAIRS, PAIR_STATE, gh)
    bn_im = b_im.reshape(PAIRS, PAIR_STATE, gh)
    bb_re = z_re[..., None] * bn_re - z_im[..., None] * bn_im
    bb_im = z_re[..., None] * bn_im + z_im[..., None] * bn_re
    cn_re = jnp.swapaxes(c_re, 1, 2).reshape(PAIRS, PAIR_STATE, gh)
    cn_im = jnp.swapaxes(c_im, 1, 2).reshape(PAIRS, PAIR_STATE, gh)

    col = jnp.arange(npow * PAIR_IN)
    e_ch = (jnp.arange(gh)[:, None] == (col % gh)[None, :]).astype(_F32)
    e_pw = (jnp.arange(npow)[:, None] == (col // PAIR_IN)[None, :]).astype(_F32)
    same_group = ((jnp.arange(PAIR_STATE) // SSM_STATE)[:, None] == ((col // gh) % 2)[None, :]).astype(_F32)

    def expand(m, e):
        return jnp.einsum('qnh,hc->qnc', m, e, precision=hp)

    def times_powers(m_re, m_im, pw_re, pw_im):
        mr, mi = expand(m_re, e_ch), expand(m_im, e_ch)
        pr, pi = expand(pw_re, e_pw), expand(pw_im, e_pw)
        return (pr * mr - pi * mi) * same_group, (pr * mi + pi * mr) * same_group

    wk_re, wk_im = times_powers(bb_re, bb_im, ak_re[..., ::-1], ak_im[..., ::-1])
    bs = jnp.concatenate([wk_re[..., PAIR_IN:], wk_im[..., PAIR_IN:]], axis=1)

    ca_re, ca_im = times_powers(cn_re, cn_im, ak_re, ak_im)
    cs = jnp.concatenate([ca_re[..., PAIR_IN:], -ca_im[..., PAIR_IN:]], axis=1)

    bbm_re, bbm_im = wk_re[..., SS * PAIR_IN:], wk_im[..., SS * PAIR_IN:]
    kt = (jnp.einsum('qnr,qnc->qrc', bbm_re, ca_re[..., :PAIR_W], precision=hp)
          - jnp.einsum('qnr,qnc->qrc', bbm_im, ca_im[..., :PAIR_W], precision=hp))
    ds = jnp.stack([jnp.pad(kt[..., :PAIR_W - PAIR_IN * i], ((0, 0), (0, 0), (PAIR_IN * i, 0)))
                    for i in range(SS)], axis=1).reshape(PAIRS, PAIR_W, PAIR_W)
    skip = jnp.tile(d.reshape(PAIRS, PAIR_IN), (1, SS))
    ds = ds + skip[:, None, :] * jnp.eye(PAIR_W, dtype=_F32)[None]

    a8 = jnp.concatenate([ak_re[..., SS].reshape(HALF), ak_im[..., SS].reshape(HALF)])
    a8 = jnp.broadcast_to(a8[None], (SUBLANES, 2 * HALF))
    return bs.astype(_BF16), cs.astype(_BF16), ds.astype(_BF16), a8


def kernel(x, meta_tokens, g_mix_pre, g_mix_post, w_in, b_gates, ssm_lambda_re, ssm_lambda_im, ssm_b_re,
           ssm_b_im, ssm_c_re, ssm_c_im, ssm_d, ssm_log_dt, w_ssm_glu, w_qk_conv, b_qk_conv, g_head_norm,
           w_branch_ssm, w_branch_mlstm, w_out, g_ffn_pre, g_ffn_post, w_ffn_gate, w_ffn_up, w_ffn_conv,
           b_ffn_conv, w_ffn_down):
    bsz, seq, _ = x.shape
    assert bsz == 1
    depth = w_in.shape[0]
    used = FRONT_PAD + N_META + seq
    lp = -(-used // TL) * TL
    assert (lp // SS) % SUBLANES == 0

    h = jnp.concatenate([jnp.zeros((FRONT_PAD, D_MODEL), _F32), meta_tokens.astype(_F32), x[0],
                         jnp.zeros((lp - used, D_MODEL), _F32)], axis=0)

    nh2 = 2 * MLSTM_HEADS
    c0, w = SSM_WIDTH, MLSTM_WIDTH
    g0 = c0 + 4 * w
    row = lambda a: a[:, None, :]
    w_u = w_in[:, :, :c0].astype(_BF16)
    w_qk = w_in[:, :, c0:c0 + 2 * w].astype(_BF16)
    w_v = w_in[:, :, c0 + 2 * w:c0 + 3 * w].astype(_BF16)
    w_o = w_in[:, :, c0 + 3 * w:g0].astype(_BF16)
    w_g = w_in[:, :, g0:g0 + nh2]
    w_gc = jnp.pad(w_g, ((0, 0), (0, 0), (0, GATE_PAD - nh2))).astype(_BF16)
    w_gr = jnp.swapaxes(w_g, 1, 2).astype(_BF16)
    w_ga = w_in[:, :, g0 + nh2:g0 + nh2 + D_MODEL].astype(_BF16)
    w_gb = w_in[:, :, g0 + nh2 + D_MODEL:].astype(_BF16)
    b_gc = row(jnp.pad(b_gates, ((0, 0), (0, GATE_PAD - nh2))))
    b_gr = jnp.broadcast_to(b_gates[:, :, None], (depth, nh2, LANES))
    w_cv = jnp.concatenate([w_qk_conv, row(b_qk_conv),
                            jnp.zeros((depth, SUBLANES - QK_CONV - 1, 2 * w), _F32)], axis=1)
    mixer_consts = (row(g_mix_pre), row(g_mix_post), w_qk, w_v, w_o, w_ga, w_gb, w_gc, w_gr, b_gc, b_gr, w_cv,
                    row(g_head_norm), w_ssm_glu.astype(_BF16), w_branch_ssm.astype(_BF16),
                    w_branch_mlstm.astype(_BF16), w_out.astype(_BF16))
    f_cv = jnp.concatenate([w_ffn_conv, row(b_ffn_conv),
                            jnp.zeros((depth, SUBLANES - FFN_CONV - 1, FFN_DIM), _F32)], axis=1)
    ffn_consts = (row(g_ffn_pre), row(g_ffn_post), w_ffn_gate.astype(_BF16), w_ffn_up.astype(_BF16),
                  w_ffn_down.astype(_BF16), f_cv)
    bs, cs, ds, a8 = jax.vmap(_s5_prep)(ssm_lambda_re, ssm_lambda_im, ssm_b_re, ssm_b_im,
                                        ssm_c_re, ssm_c_im, ssm_d, ssm_log_dt)
    g_pre = row(g_mix_pre)

    for l in range(depth):
        u = _s5_in(h, g_pre, w_u, l)
        y5 = _s5_core(u, bs, cs, ds, a8, l)
        h = _mixer(h, y5, mixer_consts, l)
        h = _ffn(h, ffn_consts, l)

    start = FRONT_PAD + N_META
    return h[start:start + seq][None]
```

```python
import functools
import math

import jax
import jax.numpy as jnp
from jax import lax
from jax.experimental import pallas as pl
from jax.experimental.pallas import tpu as pltpu

D_MODEL = 1024
N_META = 16
SSM_WIDTH = 512
SSM_GROUP = 16
SSM_GROUPS = 32
SSM_STATE = 64
MLSTM_WIDTH = 512
MLSTM_HEADS = 4
MLSTM_HEAD_DIM = 128
MLSTM_REF_CHUNK = 64
QK_CONV = 4
FFN_DIM = 2816
FFN_CONV = 3
NORM_EPS = 1e-6
PAD_LOG_INPUT_GATE = -1e4

LANES = 128
SUBLANES = 8
TL = 512
CH = 128
SS = 8
PAIRS = SSM_GROUPS // 2
PAIR_IN = 2 * SSM_GROUP
PAIR_W = SS * PAIR_IN
PAIR_STATE = 2 * SSM_STATE
PAIRS_PER_TILE = LANES // PAIR_IN
HALF = PAIRS * PAIR_STATE
FRONT_PAD = TL - N_META
MIX_BLOCK = 256
FFN_CHUNK = 256
N_FFN_CHUNKS = FFN_DIM // FFN_CHUNK
FFN_DOWN_GROUP = 4
GATE_PAD = LANES
VMEM_LIMIT = 56 * 1024 * 1024

_BF16 = jnp.bfloat16
_F32 = jnp.float32


def _rms(x, g):
    return x * lax.rsqrt(jnp.mean(x * x, axis=-1, keepdims=True) + NORM_EPS) * g


def _gelu_tanh(x):
    return 0.5 * x * (1.0 + jnp.tanh(math.sqrt(2.0 / math.pi) * (x + 0.044715 * (x * x * x))))


def _sigmoid(x):
    return 0.5 + 0.5 * jnp.tanh(0.5 * x)


def _log_sigmoid(x):
    return jnp.minimum(x, 0.0) - jnp.log(1.0 + jnp.exp(-jnp.abs(x)))


def _dot(a, b):
    return jnp.dot(a, b, preferred_element_type=_F32)


def _layer_spec(shape, l):
    nd = len(shape) - 1
    return pl.BlockSpec((None,) + tuple(shape[1:]), lambda i, _l=l, _nd=nd: (_l,) + (0,) * _nd)


def _stream_specs(first):
    if first:
        return [pl.BlockSpec((TL, D_MODEL), lambda i: (0, 0)),
                pl.BlockSpec((TL, D_MODEL), lambda i: (jnp.maximum(i - 1, 0), 0))]
    return [pl.BlockSpec((TL, D_MODEL), lambda i: (i, 0))]


def _stream_tile(first, refs):
    if first:
        head_ref, x_ref, *rest = refs
        return jnp.where(pl.program_id(0) == 0, head_ref[...], x_ref[...]), rest
    x_ref, *rest = refs
    return x_ref[...], rest


def _s5_in_kernel(first, *refs):
    x, (g_ref, w_ref, u_ref, us_ref) = _stream_tile(first, refs)
    hn = _rms(x, g_ref[...]).astype(_BF16)
    u = _dot(hn, w_ref[...])
    for t in range(SSM_WIDTH // LANES):
        us_ref[t] = u[:, t * LANES:(t + 1) * LANES]
    for j in range(SS):
        for t in range(SSM_WIDTH // LANES):
            rows = us_ref[t, pl.ds(j, TL // SS, stride=SS), :].astype(_BF16)
            for k in range(PAIRS_PER_TILE):
                u_ref[t * PAIRS_PER_TILE + k, :, j * PAIR_IN:(j + 1) * PAIR_IN] = (
                    rows[:, k * PAIR_IN:(k + 1) * PAIR_IN])


def _s5_in(stream, g, w, l, lp):
    first = len(stream) == 2
    return pl.pallas_call(
        functools.partial(_s5_in_kernel, first),
        grid=(lp // TL,),
        in_specs=_stream_specs(first) + [_layer_spec(g.shape, l), _layer_spec(w.shape, l)],
        out_specs=pl.BlockSpec((PAIRS, TL // SS, PAIR_W), lambda i: (0, i, 0)),
        out_shape=jax.ShapeDtypeStruct((PAIRS, lp // SS, PAIR_W), _BF16),
        scratch_shapes=[pltpu.VMEM((SSM_WIDTH // LANES, TL, LANES), _F32)],
        compiler_params=pltpu.CompilerParams(dimension_semantics=("parallel",),
                                             vmem_limit_bytes=VMEM_LIMIT),
        name="s5_in",
    )(*stream, g, w)


def _s5_core_kernel(rt, u_ref, bs_ref, cs_ref, ds_ref, a_ref, y_ref, vx_ref, st_ref):
    rb = rt // SUBLANES

    @pl.when(pl.program_id(0) == 0)
    def _():
        st_ref[...] = jnp.zeros_like(st_ref)

    for p in range(PAIRS):
        v = lax.dot_general(u_ref[p], bs_ref[p], (((1,), (1,)), ((), ())),
                            preferred_element_type=_F32)
        vx_ref[:, :, p * PAIR_STATE:(p + 1) * PAIR_STATE] = (
            v[:, :PAIR_STATE].reshape(rb, SUBLANES, PAIR_STATE))
        vx_ref[:, :, HALF + p * PAIR_STATE:HALF + (p + 1) * PAIR_STATE] = (
            v[:, PAIR_STATE:].reshape(rb, SUBLANES, PAIR_STATE))

    def cmul(t, z_re, z_im):
        t_re, t_im = a_ref[t, :, :HALF], a_ref[t, :, HALF:]
        return t_re * z_re - t_im * z_im, t_re * z_im + t_im * z_re

    first = lax.broadcasted_iota(jnp.int32, (SUBLANES, HALF), 0) == 0

    def body(i, carry):
        s_re, s_im = carry
        w_re = vx_ref[i, :, :HALF]
        w_im = vx_ref[i, :, HALF:]
        for t, shift in enumerate((1, 2, 4)):
            d_re, d_im = cmul(t, pltpu.roll(w_re, shift, 0), pltpu.roll(w_im, shift, 0))
            w_re, w_im = w_re + d_re, w_im + d_im
        x_re, x_im = cmul(3, s_re, s_im)
        vx_ref[i, :, :HALF] = x_re + jnp.where(first, 0.0, pltpu.roll(w_re, 1, 0))
        vx_ref[i, :, HALF:] = x_im + jnp.where(first, 0.0, pltpu.roll(w_im, 1, 0))
        n_re, n_im = cmul(4, s_re, s_im)
        last = slice(SUBLANES - 1, SUBLANES)
        return (n_re + jnp.broadcast_to(w_re[last], (SUBLANES, HALF)),
                n_im + jnp.broadcast_to(w_im[last], (SUBLANES, HALF)))

    s_re, s_im = lax.fori_loop(0, rb, body, (st_ref[:, :HALF], st_ref[:, HALF:]))
    st_ref[:, :HALF] = s_re
    st_ref[:, HALF:] = s_im

    for p in range(PAIRS):
        x_re = vx_ref[:, :, p * PAIR_STATE:(p + 1) * PAIR_STATE].reshape(rt, PAIR_STATE)
        x_im = vx_ref[:, :, HALF + p * PAIR_STATE:HALF + (p + 1) * PAIR_STATE].reshape(rt, PAIR_STATE)
        xp = jnp.concatenate([x_re, x_im], axis=-1).astype(_BF16)
        y_ref[p] = _dot(xp, cs_ref[p]) + _dot(u_ref[p], ds_ref[p])


def _s5_core(u, bs, cs, ds, a8, l):
    rows = u.shape[1]
    rt = rows // 8 if (rows // 8) % SUBLANES == 0 else rows
    return pl.pallas_call(
        functools.partial(_s5_core_kernel, rt),
        grid=(rows // rt,),
        in_specs=[
            pl.BlockSpec((PAIRS, rt, PAIR_W), lambda i: (0, i, 0)),
            _layer_spec(bs.shape, l), _layer_spec(cs.shape, l), _layer_spec(ds.shape, l),
            _layer_spec(a8.shape, l),
        ],
        out_specs=pl.BlockSpec((PAIRS, rt, PAIR_W), lambda i: (0, i, 0)),
        out_shape=jax.ShapeDtypeStruct((PAIRS, rows, PAIR_W), _F32),
        scratch_shapes=[
            pltpu.VMEM((rt // SUBLANES, SUBLANES, 2 * HALF), _F32),
            pltpu.VMEM((SUBLANES, 2 * HALF), _F32),
        ],
        compiler_params=pltpu.CompilerParams(dimension_semantics=("arbitrary",),
                                             vmem_limit_bytes=VMEM_LIMIT),
        name="s5_core",
    )(u, bs, cs, ds, a8)


def _mixer_kernel(first, *refs):
    x, rest = _stream_tile(first, refs)
    (y5_ref, gpre_ref, gpost_ref, wqk_ref, wv_ref, wo_ref, wga_ref, wgb_ref,
     wgc_ref, wgr_ref, bgc_ref, bgr_ref, wcv_ref, ghead_ref, wglu_ref, wa_ref, wb_ref,
     wout_ref, o_ref, cv_ref, st_ref, m_ref, yb_ref, y5s_ref, q_ref, k_ref, v_ref,
     so_ref, sga_ref, sgb_ref) = rest
    pid = pl.program_id(0)

    @pl.when(pid == 0)
    def _():
        cv_ref[0:SUBLANES, :] = jnp.zeros((SUBLANES, 2 * MLSTM_WIDTH), _F32)
        st_ref[...] = jnp.zeros_like(st_ref)
        m_ref[...] = jnp.zeros_like(m_ref)
        for h in range(MLSTM_HEADS):
            v_ref[h, :, MLSTM_HEAD_DIM:] = jnp.ones((TL, MLSTM_HEAD_DIM), _BF16)

    hn = _rms(x, gpre_ref[...]).astype(_BF16)

    row_c = pid * TL + lax.broadcasted_iota(jnp.int32, (TL, 1), 0)
    valid_c = row_c >= FRONT_PAD
    row_r = pid * TL + lax.broadcasted_iota(jnp.int32, (1, TL), 1)
    valid_r = row_r >= FRONT_PAD

    n_blocks = 2 * MLSTM_WIDTH // MIX_BLOCK

    def proj_block(b):
        cols = slice(b * MIX_BLOCK, (b + 1) * MIX_BLOCK)
        cv_ref[SUBLANES:SUBLANES + TL, cols] = _dot(hn, wqk_ref[:, cols])

    def conv_block(b):
        cols = slice(b * MIX_BLOCK, (b + 1) * MIX_BLOCK)
        acc = wcv_ref[QK_CONV:QK_CONV + 1, cols]
        for j in range(QK_CONV):
            off = SUBLANES - (QK_CONV - 1) + j
            acc = acc + wcv_ref[j:j + 1, cols] * cv_ref[off:off + TL, cols]
        cv_ref[0:SUBLANES, cols] = cv_ref[TL:TL + SUBLANES, cols]
        qk = jnp.where(valid_c, acc * _sigmoid(acc), 0.0)
        if b < MLSTM_WIDTH // MIX_BLOCK:
            q_ref[:, cols] = qk.astype(_BF16)
        else:
            kcols = slice(b * MIX_BLOCK - MLSTM_WIDTH, (b + 1) * MIX_BLOCK - MLSTM_WIDTH)
            k_ref[:, kcols] = qk * (MLSTM_HEAD_DIM ** -0.5)

    proj_block(0)
    proj_block(1)
    conv_block(0)
    proj_block(2)
    conv_block(1)
    proj_block(3)
    assert n_blocks == 4
    v_all = jnp.where(valid_c, _dot(hn, wv_ref[...]), 0.0).astype(_BF16)
    for h in range(MLSTM_HEADS):
        v_ref[h, :, :MLSTM_HEAD_DIM] = v_all[:, h * MLSTM_HEAD_DIM:(h + 1) * MLSTM_HEAD_DIM]
    conv_block(2)
    so_ref[...] = _sigmoid(_dot(hn, wo_ref[...]))
    lane = lax.broadcasted_iota(jnp.int32, (1, GATE_PAD), 1)
    gc = _dot(hn, wgc_ref[...]) + bgc_ref[...]
    gc = jnp.where(lane < MLSTM_HEADS, gc, _log_sigmoid(gc))
    gc = jnp.where(valid_c, gc, jnp.where(lane < MLSTM_HEADS, PAD_LOG_INPUT_GATE, 0.0))
    sub = lax.broadcasted_iota(jnp.int32, (2 * MLSTM_HEADS, 1), 0)
    gr = lax.dot_general(wgr_ref[...], hn, (((1,), (1,)), ((), ())),
                         preferred_element_type=_F32) + bgr_ref[:, 0:1]
    gr = jnp.where(sub < MLSTM_HEADS, gr, _log_sigmoid(gr))
    gr = jnp.where(valid_r, gr, jnp.where(sub < MLSTM_HEADS, PAD_LOG_INPUT_GATE, 0.0))
    conv_block(3)

    it = lax.broadcasted_iota(jnp.int32, (CH, CH), 0)
    js = lax.broadcasted_iota(jnp.int32, (CH, CH), 1)
    causal = it >= js
    tri_l = causal.astype(_F32)
    tri_u = (it <= js).astype(_F32)

    def regroup_s5(j):
        for t in range(SSM_WIDTH // LANES):
            yj = jnp.concatenate([y5_ref[t * PAIRS_PER_TILE + k, :, j * PAIR_IN:(j + 1) * PAIR_IN]
                                  for k in range(PAIRS_PER_TILE)], axis=-1)
            y5s_ref[t, pl.ds(j, TL // SS, stride=SS), :] = yj

    n_chunks = TL // CH
    hp = lax.Precision.HIGHEST
    rows = [slice(c * CH, (c + 1) * CH) for c in range(n_chunks)]
    heads = [slice(h * MLSTM_HEAD_DIM, (h + 1) * MLSTM_HEAD_DIM) for h in range(MLSTM_HEADS)]

    gcc = [gc[rows[c], :] for c in range(n_chunks)]
    b_c = [jnp.dot(tri_l, gcc[c], preferred_element_type=_F32, precision=hp) for c in range(n_chunks)]
    b_r = [jnp.dot(gr[:, rows[c]], tri_u, preferred_element_type=_F32, precision=hp) for c in range(n_chunks)]

    rrow = [[None] * MLSTM_HEADS for _ in range(n_chunks)]
    m_in = [[None] * MLSTM_HEADS for _ in range(n_chunks)]
    m_out = [[None] * MLSTM_HEADS for _ in range(n_chunks)]
    for h in range(MLSTM_HEADS):
        fl = MLSTM_HEADS + h
        m = m_ref[h:h + 1, :]
        for c in range(n_chunks):
            rrow[c][h] = gr[h:h + 1, rows[c]] - b_r[c][fl:fl + 1, :]
            m_in[c][h] = m
            m_out[c][h] = jnp.maximum(m, jnp.max(rrow[c][h], axis=-1, keepdims=True))
            m = b_r[c][fl:fl + 1, CH - 1:CH] + m_out[c][h]
        m_ref[h:h + 1, :] = m

    st_in = [[None] * MLSTM_HEADS for _ in range(n_chunks)]
    for h in range(MLSTM_HEADS):
        fl = MLSTM_HEADS + h
        st = st_ref[h]
        for c in range(n_chunks):
            st_in[c][h] = st
            rcol = gcc[c][:, h:h + 1] - b_c[c][:, fl:fl + 1]
            kw = k_ref[rows[c], heads[h]] * jnp.exp(rcol - m_out[c][h][:, 0:1])
            decay = jnp.exp(m_in[c][h] - m_out[c][h])
            st = (jnp.concatenate([decay, decay], axis=-1) * st
                  + _dot(kw.T.astype(_BF16), v_ref[h, rows[c], :]))
        st_ref[h] = st

    for c in range(n_chunks):
        gcols = slice(c * (D_MODEL // n_chunks), (c + 1) * (D_MODEL // n_chunks))
        sga_ref[:, gcols] = _sigmoid(_dot(hn, wga_ref[:, gcols]))
        sgb_ref[:, gcols] = _sigmoid(_dot(hn, wgb_ref[:, gcols]))
        for j in range(c * SS // n_chunks, (c + 1) * SS // n_chunks):
            regroup_s5(j)
        hrange = range(MLSTM_HEADS)
        qc = [q_ref[rows[c], heads[h]] for h in hrange]
        qk = [lax.dot_general(qc[h], k_ref[rows[c], heads[h]].astype(_BF16), (((1,), (1,)), ((), ())),
                              preferred_element_type=_F32) for h in hrange]
        qs = [_dot(qc[h], st_in[c][h].astype(_BF16)) for h in hrange]
        mc1, mcol, s = [], [], []
        for h in hrange:
            rmat = jnp.where(causal, rrow[c][h], -jnp.inf)
            mc1.append(jnp.maximum(m_in[c][h][:, 0:1], jnp.max(rmat, axis=-1, keepdims=True)))
            mcol.append(jnp.broadcast_to(mc1[h], (CH, LANES)))
            s.append((qk[h] * jnp.exp(rmat - mcol[h])).astype(_BF16))
        sv = [_dot(s[h], v_ref[h, rows[c], :]) for h in hrange]
        for h in hrange:
            hs = heads[h]
            bcol = b_c[c][:, MLSTM_HEADS + h:MLSTM_HEADS + h + 1]
            floor = jnp.broadcast_to(jnp.exp(-(bcol + mc1[h])), (CH, LANES))
            w_inter = jnp.exp(m_in[c][h] - mcol[h])
            num = sv[h][:, :MLSTM_HEAD_DIM] + w_inter * qs[h][:, :MLSTM_HEAD_DIM]
            den = sv[h][:, MLSTM_HEAD_DIM:] + w_inter * qs[h][:, MLSTM_HEAD_DIM:]
            hout = num / jnp.maximum(jnp.abs(den), floor)
            hout = hout * lax.rsqrt(jnp.mean(hout * hout, axis=-1, keepdims=True) + NORM_EPS)
            hout = hout * ghead_ref[:, hs] * so_ref[rows[c], hs]
            yb_ref[rows[c], hs] = hout.astype(_BF16)

    ya = _gelu_tanh(jnp.concatenate([y5s_ref[t] for t in range(SSM_WIDTH // LANES)], axis=-1))
    ya = ya * _sigmoid(_dot(ya.astype(_BF16), wglu_ref[...]))

    merged = (sga_ref[...] * _dot(ya.astype(_BF16), wa_ref[...])
              + sgb_ref[...] * _dot(yb_ref[...], wb_ref[...]))
    o_ref[...] = x + _rms(_dot(merged.astype(_BF16), wout_ref[...]), gpost_ref[...])


def _mixer(stream, y5, consts, l, lp):
    first = len(stream) == 2
    return pl.pallas_call(
        functools.partial(_mixer_kernel, first),
        grid=(lp // TL,),
        in_specs=_stream_specs(first) + [
            pl.BlockSpec((PAIRS, TL // SS, PAIR_W), lambda i: (0, i, 0)),
        ] + [_layer_spec(c.shape, l) for c in consts],
        out_specs=pl.BlockSpec((TL, D_MODEL), lambda i: (i, 0)),
        out_shape=jax.ShapeDtypeStruct((lp, D_MODEL), _F32),
        scratch_shapes=[
            pltpu.VMEM((TL + SUBLANES, 2 * MLSTM_WIDTH), _F32),
            pltpu.VMEM((MLSTM_HEADS, MLSTM_HEAD_DIM, 2 * MLSTM_HEAD_DIM), _F32),
            pltpu.VMEM((SUBLANES, LANES), _F32),
            pltpu.VMEM((TL, MLSTM_WIDTH), _BF16),
            pltpu.VMEM((SSM_WIDTH // LANES, TL, LANES), _F32),
            pltpu.VMEM((TL, MLSTM_WIDTH), _BF16),
            pltpu.VMEM((TL, MLSTM_WIDTH), _F32),
            pltpu.VMEM((MLSTM_HEADS, TL, 2 * MLSTM_HEAD_DIM), _BF16),
            pltpu.VMEM((TL, MLSTM_WIDTH), _F32),
            pltpu.VMEM((TL, D_MODEL), _F32),
            pltpu.VMEM((TL, D_MODEL), _F32),
        ],
        compiler_params=pltpu.CompilerParams(dimension_semantics=("arbitrary",),
                                             vmem_limit_bytes=VMEM_LIMIT),
        name="mixer",
    )(*stream, y5, *consts)


def _ffn_kernel(x_ref, gpre_ref, gpost_ref, wg_ref, wu_ref, wd_ref, wcv_ref, o_ref, cv_ref, act_ref):
    @pl.when(pl.program_id(0) == 0)
    def _():
        cv_ref[0:SUBLANES, :] = jnp.zeros((SUBLANES, FFN_DIM), _F32)

    x = x_ref[...]
    hn = _rms(x, gpre_ref[...]).astype(_BF16)
    chunk = lambda c: slice(c * FFN_CHUNK, (c + 1) * FFN_CHUNK)
    acc = None
    group_start = 0
    cv_ref[SUBLANES:SUBLANES + TL, chunk(0)] = _dot(hn, wg_ref[:, chunk(0)])
    up_next = _dot(hn, wu_ref[:, chunk(0)])
    for c in range(N_FFN_CHUNKS):
        cs = chunk(c)
        up = up_next
        if c + 1 < N_FFN_CHUNKS:
            cv_ref[SUBLANES:SUBLANES + TL, chunk(c + 1)] = _dot(hn, wg_ref[:, chunk(c + 1)])
            up_next = _dot(hn, wu_ref[:, chunk(c + 1)])
        conv = wcv_ref[FFN_CONV:FFN_CONV + 1, cs]
        for j in range(FFN_CONV):
            off = SUBLANES - (FFN_CONV - 1) + j
            conv = conv + wcv_ref[j:j + 1, cs] * cv_ref[off:off + TL, cs]
        cv_ref[0:SUBLANES, cs] = cv_ref[TL:TL + SUBLANES, cs]
        act_ref[:, cs] = (_gelu_tanh(conv) * up).astype(_BF16)
        if (c + 1) % FFN_DOWN_GROUP == 0 or c + 1 == N_FFN_CHUNKS:
            gs = slice(group_start * FFN_CHUNK, (c + 1) * FFN_CHUNK)
            part = _dot(act_ref[:, gs], wd_ref[gs, :])
            acc = part if acc is None else acc + part
            group_start = c + 1
    o_ref[...] = x + _rms(acc, gpost_ref[...])


def _ffn(x, consts, l, last):
    lp = x.shape[0]
    if last:
        out_spec = pl.BlockSpec((TL, D_MODEL), lambda i: (jnp.maximum(i - 1, 0), 0))
        out_rows = lp - TL
    else:
        out_spec = pl.BlockSpec((TL, D_MODEL), lambda i: (i, 0))
        out_rows = lp
    return pl.pallas_call(
        _ffn_kernel,
        grid=(lp // TL,),
        in_specs=[pl.BlockSpec((TL, D_MODEL), lambda i: (i, 0))] + [_layer_spec(c.shape, l) for c in consts],
        out_specs=out_spec,
        out_shape=jax.ShapeDtypeStruct((out_rows, D_MODEL), _F32),
        scratch_shapes=[pltpu.VMEM((TL + SUBLANES, FFN_DIM), _F32),
                        pltpu.VMEM((TL, FFN_DIM), _BF16)],
        compiler_params=pltpu.CompilerParams(dimension_semantics=("arbitrary",),
                                             vmem_limit_bytes=VMEM_LIMIT),
        name="ffn",
    )(x, *consts)


def _s5_prep(lam_re, lam_im, b_re, b_im, c_re, c_im, d, log_dt):
    hp = lax.Precision.HIGHEST
    gh, npow = SSM_GROUP, SS + 1
    lr = lam_re.reshape(PAIRS, PAIR_STATE)
    li = lam_im.reshape(PAIRS, PAIR_STATE)
    dt = jnp.repeat(jnp.exp(log_dt), SSM_STATE).reshape(PAIRS, PAIR_STATE)
    ks = jnp.arange(npow, dtype=_F32)
    mag = jnp.exp((lr * dt)[..., None] * ks)
    ang = (li * dt)[..., None] * ks
    ak_re, ak_im = mag * jnp.cos(ang), mag * jnp.sin(ang)
    nr, ni = ak_re[..., 1] - 1.0, ak_im[..., 1]
    den = lr * lr + li * li
    z_re = (nr * lr + ni * li) / den
    z_im = (ni * lr - nr * li) / den
    bn_re = b_re.reshape(PAIRS, PAIR_STATE, gh)
    bn_im = b_im.reshape(PAIRS, PAIR_STATE, gh)
    bb_re = z_re[..., None] * bn_re - z_im[..., None] * bn_im
    bb_im = z_re[..., None] * bn_im + z_im[..., None] * bn_re
    cn_re = jnp.swapaxes(c_re, 1, 2).reshape(PAIRS, PAIR_STATE, gh)
    cn_im = jnp.swapaxes(c_im, 1, 2).reshape(PAIRS, PAIR_STATE, gh)

    col = jnp.arange(npow * PAIR_IN)
    e_ch = (jnp.arange(gh)[:, None] == (col % gh)[None, :]).astype(_F32)
    e_pw = (jnp.arange(npow)[:, None] == (col // PAIR_IN)[None, :]).astype(_F32)
    same_group = ((jnp.arange(PAIR_STATE) // SSM_STATE)[:, None] == ((col // gh) % 2)[None, :]).astype(_F32)

    def expand(m, e):
        return jnp.einsum('qnh,hc->qnc', m, e, precision=hp)

    def times_powers(m_re, m_im, pw_re, pw_im):
        mr, mi = expand(m_re, e_ch), expand(m_im, e_ch)
        pr, pi = expand(pw_re, e_pw), expand(pw_im, e_pw)
        return (pr * mr - pi * mi) * same_group, (pr * mi + pi * mr) * same_group

    wk_re, wk_im = times_powers(bb_re, bb_im, ak_re[..., ::-1], ak_im[..., ::-1])
    bs = jnp.concatenate([wk_re[..., PAIR_IN:], wk_im[..., PAIR_IN:]], axis=1)

    ca_re, ca_im = times_powers(cn_re, cn_im, ak_re, ak_im)
    cs = jnp.concatenate([ca_re[..., PAIR_IN:], -ca_im[..., PAIR_IN:]], axis=1)

    bbm_re, bbm_im = wk_re[..., SS * PAIR_IN:], wk_im[..., SS * PAIR_IN:]
    kt = (jnp.einsum('qnr,qnc->qrc', bbm_re, ca_re[..., :PAIR_W], precision=hp)
          - jnp.einsum('qnr,qnc->qrc', bbm_im, ca_im[..., :PAIR_W], precision=hp))
    ds = jnp.stack([jnp.pad(kt[..., :PAIR_W - PAIR_IN * i], ((0, 0), (0, 0), (PAIR_IN * i, 0)))
                    for i in range(SS)], axis=1).reshape(PAIRS, PAIR_W, PAIR_W)
    skip = jnp.tile(d.reshape(PAIRS, PAIR_IN), (1, SS))
    ds = ds + skip[:, None, :] * jnp.eye(PAIR_W, dtype=_F32)[None]

    ks8 = SS * jnp.arange(SUBLANES + 1, dtype=_F32)
    mag8 = jnp.exp((lr * dt).reshape(HALF)[None, :] * ks8[:, None])
    ang8 = (li * dt).reshape(HALF)[None, :] * ks8[:, None]
    apow = jnp.concatenate([mag8 * jnp.cos(ang8), mag8 * jnp.sin(ang8)], axis=1)
    sub = jnp.arange(SUBLANES)
    shifted = [jnp.where((sub >= k)[:, None], apow[k][None, :], 0.0) for k in (1, 2, 4)]
    a8 = jnp.stack(shifted + [apow[:SUBLANES], jnp.broadcast_to(apow[SUBLANES][None], (SUBLANES, 2 * HALF))])
    return bs.astype(_BF16), cs.astype(_BF16), ds.astype(_BF16), a8


def kernel(x, meta_tokens, g_mix_pre, g_mix_post, w_in, b_gates, ssm_lambda_re, ssm_lambda_im, ssm_b_re,
           ssm_b_im, ssm_c_re, ssm_c_im, ssm_d, ssm_log_dt, w_ssm_glu, w_qk_conv, b_qk_conv, g_head_norm,
           w_branch_ssm, w_branch_mlstm, w_out, g_ffn_pre, g_ffn_post, w_ffn_gate, w_ffn_up, w_ffn_conv,
           b_ffn_conv, w_ffn_down):
    bsz, seq, _ = x.shape
    assert bsz == 1
    depth = w_in.shape[0]
    assert seq % TL == 0 and FRONT_PAD + N_META == TL
    lp = TL + seq
    assert (lp // SS) % SUBLANES == 0

    head = jnp.concatenate([jnp.zeros((FRONT_PAD, D_MODEL), _F32), meta_tokens.astype(_F32)], axis=0)
    stream = (head, x[0])

    nh2 = 2 * MLSTM_HEADS
    c0, w = SSM_WIDTH, MLSTM_WIDTH
    g0 = c0 + 4 * w
    row = lambda a: a[:, None, :]
    w_u = w_in[:, :, :c0].astype(_BF16)
    w_qk = w_in[:, :, c0:c0 + 2 * w].astype(_BF16)
    w_v = w_in[:, :, c0 + 2 * w:c0 + 3 * w].astype(_BF16)
    w_o = w_in[:, :, c0 + 3 * w:g0].astype(_BF16)
    w_g = w_in[:, :, g0:g0 + nh2]
    w_gc = jnp.pad(w_g, ((0, 0), (0, 0), (0, GATE_PAD - nh2))).astype(_BF16)
    w_gr = jnp.swapaxes(w_g, 1, 2).astype(_BF16)
    w_ga = w_in[:, :, g0 + nh2:g0 + nh2 + D_MODEL].astype(_BF16)
    w_gb = w_in[:, :, g0 + nh2 + D_MODEL:].astype(_BF16)
    b_gc = row(jnp.pad(b_gates, ((0, 0), (0, GATE_PAD - nh2))))
    b_gr = jnp.broadcast_to(b_gates[:, :, None], (depth, nh2, LANES))
    w_cv = jnp.concatenate([w_qk_conv, row(b_qk_conv),
                            jnp.zeros((depth, SUBLANES - QK_CONV - 1, 2 * w), _F32)], axis=1)
    mixer_consts = (row(g_mix_pre), row(g_mix_post), w_qk, w_v, w_o, w_ga, w_gb, w_gc, w_gr, b_gc, b_gr, w_cv,
                    row(g_head_norm), w_ssm_glu.astype(_BF16), w_branch_ssm.astype(_BF16),
                    w_branch_mlstm.astype(_BF16), w_out.astype(_BF16))
    f_cv = jnp.concatenate([w_ffn_conv, row(b_ffn_conv),
                            jnp.zeros((depth, SUBLANES - FFN_CONV - 1, FFN_DIM), _F32)], axis=1)
    ffn_consts = (row(g_ffn_pre), row(g_ffn_post), w_ffn_gate.astype(_BF16), w_ffn_up.astype(_BF16),
                  w_ffn_down.astype(_BF16), f_cv)
    bs, cs, ds, a8 = jax.vmap(_s5_prep)(ssm_lambda_re, ssm_lambda_im, ssm_b_re, ssm_b_im,
                                        ssm_c_re, ssm_c_im, ssm_d, ssm_log_dt)
    g_pre = row(g_mix_pre)

    for l in range(depth):
        u = _s5_in(stream, g_pre, w_u, l, lp)
        y5 = _s5_core(u, bs, cs, ds, a8, l)
        h = _mixer(stream, y5, mixer_consts, l, lp)
        h = _ffn(h, ffn_consts, l, last=(l == depth - 1))
        stream = (h,)
    return h[None]
```

```python
import functools
import math

import jax
import jax.numpy as jnp
from jax import lax
from jax.experimental import pallas as pl
from jax.experimental.pallas import tpu as pltpu

D_MODEL = 1024
N_META = 16
SSM_WIDTH = 512
SSM_GROUP = 16
SSM_GROUPS = 32
SSM_STATE = 64
MLSTM_WIDTH = 512
MLSTM_HEADS = 4
MLSTM_HEAD_DIM = 128
MLSTM_REF_CHUNK = 64
QK_CONV = 4
FFN_DIM = 2816
FFN_CONV = 3
NORM_EPS = 1e-6
PAD_LOG_INPUT_GATE = -1e4

LANES = 128
SUBLANES = 8
TL = 512
CH = 128
SS = 8
PAIRS = SSM_GROUPS // 2
PAIR_IN = 2 * SSM_GROUP
PAIR_W = SS * PAIR_IN
PAIR_STATE = 2 * SSM_STATE
PAIRS_PER_TILE = LANES // PAIR_IN
HALF = PAIRS * PAIR_STATE
FRONT_PAD = TL - N_META
MIX_BLOCK = 256
FFN_CHUNK = 256
N_FFN_CHUNKS = FFN_DIM // FFN_CHUNK
FFN_DOWN_GROUP = 4
GATE_PAD = LANES
WIN_V = 2 * MLSTM_WIDTH
WIN_O = WIN_V + MLSTM_WIDTH
WIN_GA = WIN_O + MLSTM_WIDTH
WIN_GB = WIN_GA + D_MODEL
WIN_GATES = WIN_GB + D_MODEL
WIN_COLS = WIN_GATES + GATE_PAD
VMEM_LIMIT = 56 * 1024 * 1024

_BF16 = jnp.bfloat16
_F32 = jnp.float32


def _rms(x, g):
    return x * lax.rsqrt(jnp.mean(x * x, axis=-1, keepdims=True) + NORM_EPS) * g


def _gelu_tanh(x):
    return 0.5 * x * (1.0 + jnp.tanh(math.sqrt(2.0 / math.pi) * (x + 0.044715 * (x * x * x))))


def _sigmoid(x):
    return 0.5 + 0.5 * jnp.tanh(0.5 * x)


def _log_sigmoid(x):
    return jnp.minimum(x, 0.0) - jnp.log(1.0 + jnp.exp(-jnp.abs(x)))


def _dot(a, b):
    return jnp.dot(a, b, preferred_element_type=_F32)


def _layer_spec(shape, l):
    nd = len(shape) - 1
    return pl.BlockSpec((None,) + tuple(shape[1:]), lambda i, _l=l, _nd=nd: (_l,) + (0,) * _nd)


def _stream_specs(first):
    if first:
        return [pl.BlockSpec((TL, D_MODEL), lambda i: (0, 0)),
                pl.BlockSpec((TL, D_MODEL), lambda i: (jnp.maximum(i - 1, 0), 0))]
    return [pl.BlockSpec((TL, D_MODEL), lambda i: (i, 0))]


def _stream_tile(first, refs):
    if first:
        head_ref, x_ref, *rest = refs
        return jnp.where(pl.program_id(0) == 0, head_ref[...], x_ref[...]), rest
    x_ref, *rest = refs
    return x_ref[...], rest


def _s5_in_kernel(first, *refs):
    x, (g_ref, w_ref, u_ref, us_ref) = _stream_tile(first, refs)
    _s5_project(x, g_ref, w_ref, u_ref, us_ref)


def _s5_project(x, g_ref, w_ref, u_ref, us_ref):
    hn = _rms(x, g_ref[...]).astype(_BF16)
    u = _dot(hn, w_ref[...])
    for t in range(SSM_WIDTH // LANES):
        us_ref[t] = u[:, t * LANES:(t + 1) * LANES]
    for j in range(SS):
        for t in range(SSM_WIDTH // LANES):
            rows = us_ref[t, pl.ds(j, TL // SS, stride=SS), :].astype(_BF16)
            for k in range(PAIRS_PER_TILE):
                u_ref[t * PAIRS_PER_TILE + k, :, j * PAIR_IN:(j + 1) * PAIR_IN] = (
                    rows[:, k * PAIR_IN:(k + 1) * PAIR_IN])


def _s5_in(stream, g, w, l, lp):
    first = len(stream) == 2
    return pl.pallas_call(
        functools.partial(_s5_in_kernel, first),
        grid=(lp // TL,),
        in_specs=_stream_specs(first) + [_layer_spec(g.shape, l), _layer_spec(w.shape, l)],
        out_specs=pl.BlockSpec((PAIRS, TL // SS, PAIR_W), lambda i: (0, i, 0)),
        out_shape=jax.ShapeDtypeStruct((PAIRS, lp // SS, PAIR_W), _BF16),
        scratch_shapes=[pltpu.VMEM((SSM_WIDTH // LANES, TL, LANES), _F32)],
        compiler_params=pltpu.CompilerParams(dimension_semantics=("parallel",),
                                             vmem_limit_bytes=VMEM_LIMIT),
        name="s5_in",
    )(*stream, g, w)


def _s5_core_kernel(rt, u_ref, bs_ref, cs_ref, ds_ref, a_ref, y_ref, vx_ref, st_ref):
    rb = rt // SUBLANES

    @pl.when(pl.program_id(0) == 0)
    def _():
        st_ref[...] = jnp.zeros_like(st_ref)

    for p in range(PAIRS):
        v = lax.dot_general(u_ref[p], bs_ref[p], (((1,), (1,)), ((), ())),
                            preferred_element_type=_F32)
        vx_ref[:, :, p * PAIR_STATE:(p + 1) * PAIR_STATE] = (
            v[:, :PAIR_STATE].reshape(rb, SUBLANES, PAIR_STATE))
        vx_ref[:, :, HALF + p * PAIR_STATE:HALF + (p + 1) * PAIR_STATE] = (
            v[:, PAIR_STATE:].reshape(rb, SUBLANES, PAIR_STATE))

    def cmul(t, z_re, z_im):
        t_re, t_im = a_ref[t, :, :HALF], a_ref[t, :, HALF:]
        return t_re * z_re - t_im * z_im, t_re * z_im + t_im * z_re

    first = lax.broadcasted_iota(jnp.int32, (SUBLANES, HALF), 0) == 0

    def body(i, carry):
        s_re, s_im = carry
        w_re = vx_ref[i, :, :HALF]
        w_im = vx_ref[i, :, HALF:]
        for t, shift in enumerate((1, 2, 4)):
            d_re, d_im = cmul(t, pltpu.roll(w_re, shift, 0), pltpu.roll(w_im, shift, 0))
            w_re, w_im = w_re + d_re, w_im + d_im
        x_re, x_im = cmul(3, s_re, s_im)
        vx_ref[i, :, :HALF] = x_re + jnp.where(first, 0.0, pltpu.roll(w_re, 1, 0))
        vx_ref[i, :, HALF:] = x_im + jnp.where(first, 0.0, pltpu.roll(w_im, 1, 0))
        n_re, n_im = cmul(4, s_re, s_im)
        last = slice(SUBLANES - 1, SUBLANES)
        return (n_re + jnp.broadcast_to(w_re[last], (SUBLANES, HALF)),
                n_im + jnp.broadcast_to(w_im[last], (SUBLANES, HALF)))

    s_re, s_im = lax.fori_loop(0, rb, body, (st_ref[:, :HALF], st_ref[:, HALF:]))
    st_ref[:, :HALF] = s_re
    st_ref[:, HALF:] = s_im

    for p in range(PAIRS):
        x_re = vx_ref[:, :, p * PAIR_STATE:(p + 1) * PAIR_STATE].reshape(rt, PAIR_STATE)
        x_im = vx_ref[:, :, HALF + p * PAIR_STATE:HALF + (p + 1) * PAIR_STATE].reshape(rt, PAIR_STATE)
        xp = jnp.concatenate([x_re, x_im], axis=-1).astype(_BF16)
        y_ref[p] = _dot(xp, cs_ref[p]) + _dot(u_ref[p], ds_ref[p])


def _s5_core(u, bs, cs, ds, a8, l):
    rows = u.shape[1]
    rt = rows // 8 if (rows // 8) % SUBLANES == 0 else rows
    return pl.pallas_call(
        functools.partial(_s5_core_kernel, rt),
        grid=(rows // rt,),
        in_specs=[
            pl.BlockSpec((PAIRS, rt, PAIR_W), lambda i: (0, i, 0)),
            _layer_spec(bs.shape, l), _layer_spec(cs.shape, l), _layer_spec(ds.shape, l),
            _layer_spec(a8.shape, l),
        ],
        out_specs=pl.BlockSpec((PAIRS, rt, PAIR_W), lambda i: (0, i, 0)),
        out_shape=jax.ShapeDtypeStruct((PAIRS, rows, PAIR_W), _F32),
        scratch_shapes=[
            pltpu.VMEM((rt // SUBLANES, SUBLANES, 2 * HALF), _F32),
            pltpu.VMEM((SUBLANES, 2 * HALF), _F32),
        ],
        compiler_params=pltpu.CompilerParams(dimension_semantics=("arbitrary",),
                                             vmem_limit_bytes=VMEM_LIMIT),
        name="s5_core",
    )(u, bs, cs, ds, a8)


def _mixer_kernel(first, *refs):
    x, rest = _stream_tile(first, refs)
    (y5_ref, gpre_ref, gpost_ref, win_ref, wgr_ref, bgc_ref, bgr_ref, wcv_ref, ghead_ref, wglu_ref, wa_ref, wb_ref,
     wout_ref, o_ref, st_ref, m_ref, yb_ref, y5s_ref, q_ref, k_ref, v_ref,
     so_ref, sga_ref, sgb_ref, *cv_refs) = rest
    pid = pl.program_id(0)

    @pl.when(pid == 0)
    def _():
        for cv_ref in cv_refs:
            cv_ref[0:SUBLANES, :] = jnp.zeros((SUBLANES, MIX_BLOCK), _F32)
        st_ref[...] = jnp.zeros_like(st_ref)
        m_ref[...] = jnp.zeros_like(m_ref)
        for h in range(MLSTM_HEADS):
            v_ref[h, :, MLSTM_HEAD_DIM:] = jnp.ones((TL, MLSTM_HEAD_DIM), _BF16)

    hn = _rms(x, gpre_ref[...]).astype(_BF16)

    row_c = pid * TL + lax.broadcasted_iota(jnp.int32, (TL, 1), 0)
    valid_c = row_c >= FRONT_PAD
    row_r = pid * TL + lax.broadcasted_iota(jnp.int32, (1, TL), 1)
    valid_r = row_r >= FRONT_PAD

    n_blocks = 2 * MLSTM_WIDTH // MIX_BLOCK

    def proj_block(b):
        cols = slice(b * MIX_BLOCK, (b + 1) * MIX_BLOCK)
        cv_refs[b][SUBLANES:SUBLANES + TL, :] = _dot(hn, win_ref[:, cols])

    def conv_block(b):
        cols = slice(b * MIX_BLOCK, (b + 1) * MIX_BLOCK)
        cv_ref = cv_refs[b]
        acc = wcv_ref[QK_CONV:QK_CONV + 1, cols]
        for j in range(QK_CONV):
            off = SUBLANES - (QK_CONV - 1) + j
            acc = acc + wcv_ref[j:j + 1, cols] * cv_ref[off:off + TL, :]
        cv_ref[0:SUBLANES, :] = cv_ref[TL:TL + SUBLANES, :]
        qk = jnp.where(valid_c, acc * _sigmoid(acc), 0.0)
        if b < MLSTM_WIDTH // MIX_BLOCK:
            q_ref[:, cols] = qk.astype(_BF16)
        else:
            kcols = slice(b * MIX_BLOCK - MLSTM_WIDTH, (b + 1) * MIX_BLOCK - MLSTM_WIDTH)
            k_ref[:, kcols] = qk * (MLSTM_HEAD_DIM ** -0.5)

    gate_w = D_MODEL // n_blocks

    def branch_gate_block(dst_ref, base, b):
        dst_ref[:, b * gate_w:(b + 1) * gate_w] = _sigmoid(
            _dot(hn, win_ref[:, base + b * gate_w:base + (b + 1) * gate_w]))

    q_blocks = MLSTM_WIDTH // MIX_BLOCK
    for b in range(q_blocks, n_blocks):
        proj_block(b)
    v_all = jnp.where(valid_c, _dot(hn, win_ref[:, WIN_V:WIN_O]), 0.0).astype(_BF16)
    for h in range(MLSTM_HEADS):
        v_ref[h, :, :MLSTM_HEAD_DIM] = v_all[:, h * MLSTM_HEAD_DIM:(h + 1) * MLSTM_HEAD_DIM]
    lane = lax.broadcasted_iota(jnp.int32, (1, GATE_PAD), 1)
    gc = _dot(hn, win_ref[:, WIN_GATES:]) + bgc_ref[...]
    gc = jnp.where(lane < MLSTM_HEADS, gc, _log_sigmoid(gc))
    gc = jnp.where(valid_c, gc, jnp.where(lane < MLSTM_HEADS, PAD_LOG_INPUT_GATE, 0.0))
    sub = lax.broadcasted_iota(jnp.int32, (2 * MLSTM_HEADS, 1), 0)
    gr = lax.dot_general(wgr_ref[...], hn, (((1,), (1,)), ((), ())),
                         preferred_element_type=_F32) + bgr_ref[:, 0:1]
    gr = jnp.where(sub < MLSTM_HEADS, gr, _log_sigmoid(gr))
    gr = jnp.where(valid_r, gr, jnp.where(sub < MLSTM_HEADS, PAD_LOG_INPUT_GATE, 0.0))
    for b in range(q_blocks):
        conv_block(q_blocks + b)
        proj_block(b)

    it = lax.broadcasted_iota(jnp.int32, (CH, CH), 0)
    js = lax.broadcasted_iota(jnp.int32, (CH, CH), 1)
    causal = it >= js
    tri_l = causal.astype(_F32)
    tri_u = (it <= js).astype(_F32)

    def regroup_s5(j):
        for t in range(SSM_WIDTH // LANES):
            yj = jnp.concatenate([y5_ref[t * PAIRS_PER_TILE + k, :, j * PAIR_IN:(j + 1) * PAIR_IN]
                                  for k in range(PAIRS_PER_TILE)], axis=-1)
            y5s_ref[t, pl.ds(j, TL // SS, stride=SS), :] = yj

    n_chunks = TL // CH
    hp = lax.Precision.HIGHEST
    rows = [slice(c * CH, (c + 1) * CH) for c in range(n_chunks)]
    heads = [slice(h * MLSTM_HEAD_DIM, (h + 1) * MLSTM_HEAD_DIM) for h in range(MLSTM_HEADS)]

    gcc = [gc[rows[c], :] for c in range(n_chunks)]
    b_c = [jnp.dot(tri_l, gcc[c], preferred_element_type=_F32, precision=hp) for c in range(n_chunks)]
    b_r = [jnp.dot(gr[:, rows[c]], tri_u, preferred_element_type=_F32, precision=hp) for c in range(n_chunks)]

    rrow = [[None] * MLSTM_HEADS for _ in range(n_chunks)]
    m_in = [[None] * MLSTM_HEADS for _ in range(n_chunks)]
    m_out = [[None] * MLSTM_HEADS for _ in range(n_chunks)]
    for h in range(MLSTM_HEADS):
        fl = MLSTM_HEADS + h
        m = m_ref[h:h + 1, :]
        for c in range(n_chunks):
            rrow[c][h] = gr[h:h + 1, rows[c]] - b_r[c][fl:fl + 1, :]
            m_in[c][h] = m
            m_out[c][h] = jnp.maximum(m, jnp.max(rrow[c][h], axis=-1, keepdims=True))
            m = b_r[c][fl:fl + 1, CH - 1:CH] + m_out[c][h]
        m_ref[h:h + 1, :] = m

    st_in = [[None] * MLSTM_HEADS for _ in range(n_chunks)]
    for h in range(MLSTM_HEADS):
        fl = MLSTM_HEADS + h
        st = st_ref[h]
        for c in range(n_chunks):
            st_in[c][h] = st
            rcol = gcc[c][:, h:h + 1] - b_c[c][:, fl:fl + 1]
            kw = k_ref[rows[c], heads[h]] * jnp.exp(rcol - m_out[c][h][:, 0:1])
            decay = jnp.exp(m_in[c][h] - m_out[c][h])
            st = (jnp.concatenate([decay, decay], axis=-1) * st
                  + _dot(kw.T.astype(_BF16), v_ref[h, rows[c], :]))
        st_ref[h] = st

    for b in range(q_blocks):
        conv_block(b)
    so_ref[...] = _sigmoid(_dot(hn, win_ref[:, WIN_O:WIN_GA]))

    for c in range(n_chunks):
        branch_gate_block(sga_ref, WIN_GA, c)
        branch_gate_block(sgb_ref, WIN_GB, c)
        for j in range(c * SS // n_chunks, (c + 1) * SS // n_chunks):
            regroup_s5(j)
        hrange = range(MLSTM_HEADS)
        qc = [q_ref[rows[c], heads[h]] for h in hrange]
        qk = [lax.dot_general(qc[h], k_ref[rows[c], heads[h]].astype(_BF16), (((1,), (1,)), ((), ())),
                              preferred_element_type=_F32) for h in hrange]
        qs = [_dot(qc[h], st_in[c][h].astype(_BF16)) for h in hrange]
        mc1, mcol, s = [], [], []
        for h in hrange:
            rmat = jnp.where(causal, rrow[c][h], -jnp.inf)
            mc1.append(jnp.maximum(m_in[c][h][:, 0:1], jnp.max(rmat, axis=-1, keepdims=True)))
            mcol.append(jnp.broadcast_to(mc1[h], (CH, LANES)))
            s.append((qk[h] * jnp.exp(rmat - mcol[h])).astype(_BF16))
        sv = [_dot(s[h], v_ref[h, rows[c], :]) for h in hrange]
        for h in hrange:
            hs = heads[h]
            bcol = b_c[c][:, MLSTM_HEADS + h:MLSTM_HEADS + h + 1]
            floor = jnp.broadcast_to(jnp.exp(-(bcol + mc1[h])), (CH, LANES))
            w_inter = jnp.exp(m_in[c][h] - mcol[h])
            num = sv[h][:, :MLSTM_HEAD_DIM] + w_inter * qs[h][:, :MLSTM_HEAD_DIM]
            den = sv[h][:, MLSTM_HEAD_DIM:] + w_inter * qs[h][:, MLSTM_HEAD_DIM:]
            hout = num / jnp.maximum(jnp.abs(den), floor)
            hout = hout * lax.rsqrt(jnp.mean(hout * hout, axis=-1, keepdims=True) + NORM_EPS)
            hout = hout * ghead_ref[:, hs] * so_ref[rows[c], hs]
            yb_ref[rows[c], hs] = hout.astype(_BF16)

    ya = _gelu_tanh(jnp.concatenate([y5s_ref[t] for t in range(SSM_WIDTH // LANES)], axis=-1))
    ya = ya * _sigmoid(_dot(ya.astype(_BF16), wglu_ref[...]))

    merged = (sga_ref[...] * _dot(ya.astype(_BF16), wa_ref[...])
              + sgb_ref[...] * _dot(yb_ref[...], wb_ref[...]))
    o_ref[...] = x + _rms(_dot(merged.astype(_BF16), wout_ref[...]), gpost_ref[...])


def _mixer(stream, y5, consts, l, lp):
    first = len(stream) == 2
    return pl.pallas_call(
        functools.partial(_mixer_kernel, first),
        grid=(lp // TL,),
        in_specs=_stream_specs(first) + [
            pl.BlockSpec((PAIRS, TL // SS, PAIR_W), lambda i: (0, i, 0)),
        ] + [_layer_spec(c.shape, l) for c in consts],
        out_specs=pl.BlockSpec((TL, D_MODEL), lambda i: (i, 0)),
        out_shape=jax.ShapeDtypeStruct((lp, D_MODEL), _F32),
        scratch_shapes=[
            pltpu.VMEM((MLSTM_HEADS, MLSTM_HEAD_DIM, 2 * MLSTM_HEAD_DIM), _F32),
            pltpu.VMEM((SUBLANES, LANES), _F32),
            pltpu.VMEM((TL, MLSTM_WIDTH), _BF16),
            pltpu.VMEM((SSM_WIDTH // LANES, TL, LANES), _F32),
            pltpu.VMEM((TL, MLSTM_WIDTH), _BF16),
            pltpu.VMEM((TL, MLSTM_WIDTH), _F32),
            pltpu.VMEM((MLSTM_HEADS, TL, 2 * MLSTM_HEAD_DIM), _BF16),
            pltpu.VMEM((TL, MLSTM_WIDTH), _F32),
            pltpu.VMEM((TL, D_MODEL), _F32),
            pltpu.VMEM((TL, D_MODEL), _F32),
        ] + [pltpu.VMEM((TL + SUBLANES, MIX_BLOCK), _F32)
             for _ in range(2 * MLSTM_WIDTH // MIX_BLOCK)],
        compiler_params=pltpu.CompilerParams(dimension_semantics=("arbitrary",),
                                             vmem_limit_bytes=VMEM_LIMIT),
        name="mixer",
    )(*stream, y5, *consts)


def _ffn_kernel(fuse_next, n_tiles, *refs):
    if fuse_next:
        (x_ref, gpre_ref, gpost_ref, wg_ref, wu_ref, wd_ref, wcv_ref, gnext_ref, wnext_ref,
         o_ref, u_ref, cv_ref, act_ref, prev_ref, us_ref) = refs
    else:
        x_ref, gpre_ref, gpost_ref, wg_ref, wu_ref, wd_ref, wcv_ref, o_ref, cv_ref, act_ref = refs
    pid = pl.program_id(0)

    @pl.when(pid == 0)
    def _():
        cv_ref[0:SUBLANES, :] = jnp.zeros((SUBLANES, FFN_DIM), _F32)
        if fuse_next:
            prev_ref[...] = jnp.zeros_like(prev_ref)

    def tile():
        if fuse_next:
            _s5_project(prev_ref[...], gnext_ref, wnext_ref, u_ref, us_ref)
        out = _ffn_tile(x_ref[...], gpre_ref, gpost_ref, wg_ref, wu_ref, wd_ref, wcv_ref, cv_ref, act_ref)
        o_ref[...] = out
        if fuse_next:
            prev_ref[...] = out

    if fuse_next:
        pl.when(pid < n_tiles)(tile)

        @pl.when(pid == n_tiles)
        def _():
            _s5_project(prev_ref[...], gnext_ref, wnext_ref, u_ref, us_ref)
    else:
        tile()


def _ffn_tile(x, gpre_ref, gpost_ref, wg_ref, wu_ref, wd_ref, wcv_ref, cv_ref, act_ref):
    hn = _rms(x, gpre_ref[...]).astype(_BF16)
    chunk = lambda c: slice(c * FFN_CHUNK, (c + 1) * FFN_CHUNK)
    acc = None
    group_start = 0
    cv_ref[SUBLANES:SUBLANES + TL, chunk(0)] = _dot(hn, wg_ref[:, chunk(0)])
    up_next = _dot(hn, wu_ref[:, chunk(0)])
    for c in range(N_FFN_CHUNKS):
        cs = chunk(c)
        up = up_next
        if c + 1 < N_FFN_CHUNKS:
            cv_ref[SUBLANES:SUBLANES + TL, chunk(c + 1)] = _dot(hn, wg_ref[:, chunk(c + 1)])
            up_next = _dot(hn, wu_ref[:, chunk(c + 1)])
        conv = wcv_ref[FFN_CONV:FFN_CONV + 1, cs]
        for j in range(FFN_CONV):
            off = SUBLANES - (FFN_CONV - 1) + j
            conv = conv + wcv_ref[j:j + 1, cs] * cv_ref[off:off + TL, cs]
        cv_ref[0:SUBLANES, cs] = cv_ref[TL:TL + SUBLANES, cs]
        act_ref[:, cs] = (_gelu_tanh(conv) * up).astype(_BF16)
        if (c + 1) % FFN_DOWN_GROUP == 0 or c + 1 == N_FFN_CHUNKS:
            gs = slice(group_start * FFN_CHUNK, (c + 1) * FFN_CHUNK)
            part = _dot(act_ref[:, gs], wd_ref[gs, :])
            acc = part if acc is None else acc + part
            group_start = c + 1
    return x + _rms(acc, gpost_ref[...])


def _ffn(x, consts, l, next_s5=None):
    lp = x.shape[0]
    n_tiles = lp // TL
    scratch = [pltpu.VMEM((TL + SUBLANES, FFN_DIM), _F32),
               pltpu.VMEM((TL, FFN_DIM), _BF16)]
    in_specs = [_layer_spec(c.shape, l) for c in consts]
    if next_s5 is None:
        grid = n_tiles
        x_spec = pl.BlockSpec((TL, D_MODEL), lambda i: (i, 0))
        out_specs = pl.BlockSpec((TL, D_MODEL), lambda i: (jnp.maximum(i - 1, 0), 0))
        out_shape = jax.ShapeDtypeStruct((lp - TL, D_MODEL), _F32)
        args = (x, *consts)
    else:
        grid = n_tiles + 1
        x_spec = pl.BlockSpec((TL, D_MODEL), lambda i: (jnp.minimum(i, n_tiles - 1), 0))
        out_specs = [pl.BlockSpec((TL, D_MODEL), lambda i: (jnp.minimum(i, n_tiles - 1), 0)),
                     pl.BlockSpec((PAIRS, TL // SS, PAIR_W), lambda i: (0, jnp.maximum(i - 1, 0), 0))]
        out_shape = [jax.ShapeDtypeStruct((lp, D_MODEL), _F32),
                     jax.ShapeDtypeStruct((PAIRS, lp // SS, PAIR_W), _BF16)]
        in_specs += [_layer_spec(a.shape, l + 1) for a in next_s5]
        scratch += [pltpu.VMEM((TL, D_MODEL), _F32),
                    pltpu.VMEM((SSM_WIDTH // LANES, TL, LANES), _F32)]
        args = (x, *consts, *next_s5)
    return pl.pallas_call(
        functools.partial(_ffn_kernel, next_s5 is not None, n_tiles),
        grid=(grid,),
        in_specs=[x_spec] + in_specs,
        out_specs=out_specs,
        out_shape=out_shape,
        scratch_shapes=scratch,
        compiler_params=pltpu.CompilerParams(dimension_semantics=("arbitrary",),
                                             vmem_limit_bytes=VMEM_LIMIT),
        name="ffn",
    )(*args)


def _s5_prep(lam_re, lam_im, b_re, b_im, c_re, c_im, d, log_dt):
    hp = lax.Precision.HIGHEST
    gh, npow = SSM_GROUP, SS + 1
    lr = lam_re.reshape(PAIRS, PAIR_STATE)
    li = lam_im.reshape(PAIRS, PAIR_STATE)
    dt = jnp.repeat(jnp.exp(log_dt), SSM_STATE).reshape(PAIRS, PAIR_STATE)
    ks = jnp.arange(npow, dtype=_F32)
    mag = jnp.exp((lr * dt)[..., None] * ks)
    ang = (li * dt)[..., None] * ks
    ak_re, ak_im = mag * jnp.cos(ang), mag * jnp.sin(ang)
    nr, ni = ak_re[..., 1] - 1.0, ak_im[..., 1]
    den = lr * lr + li * li
    z_re = (nr * lr + ni * li) / den
    z_im = (ni * lr - nr * li) / den
    bn_re = b_re.reshape(PAIRS, PAIR_STATE, gh)
    bn_im = b_im.reshape(PAIRS, PAIR_STATE, gh)
    bb_re = z_re[..., None] * bn_re - z_im[..., None] * bn_im
    bb_im = z_re[..., None] * bn_im + z_im[..., None] * bn_re
    cn_re = jnp.swapaxes(c_re, 1, 2).reshape(PAIRS, PAIR_STATE, gh)
    cn_im = jnp.swapaxes(c_im, 1, 2).reshape(PAIRS, PAIR_STATE, gh)

    col = jnp.arange(npow * PAIR_IN)
    e_ch = (jnp.arange(gh)[:, None] == (col % gh)[None, :]).astype(_F32)
    e_pw = (jnp.arange(npow)[:, None] == (col // PAIR_IN)[None, :]).astype(_F32)
    same_group = ((jnp.arange(PAIR_STATE) // SSM_STATE)[:, None] == ((col // gh) % 2)[None, :]).astype(_F32)

    def expand(m, e):
        return jnp.einsum('qnh,hc->qnc', m, e, precision=hp)

    def times_powers(m_re, m_im, pw_re, pw_im):
        mr, mi = expand(m_re, e_ch), expand(m_im, e_ch)
        pr, pi = expand(pw_re, e_pw), expand(pw_im, e_pw)
        return (pr * mr - pi * mi) * same_group, (pr * mi + pi * mr) * same_group

    wk_re, wk_im = times_powers(bb_re, bb_im, ak_re[..., ::-1], ak_im[..., ::-1])
    bs = jnp.concatenate([wk_re[..., PAIR_IN:], wk_im[..., PAIR_IN:]], axis=1)

    ca_re, ca_im = times_powers(cn_re, cn_im, ak_re, ak_im)
    cs = jnp.concatenate([ca_re[..., PAIR_IN:], -ca_im[..., PAIR_IN:]], axis=1)

    bbm_re, bbm_im = wk_re[..., SS * PAIR_IN:], wk_im[..., SS * PAIR_IN:]
    kt = (jnp.einsum('qnr,qnc->qrc', bbm_re, ca_re[..., :PAIR_W], precision=hp)
          - jnp.einsum('qnr,qnc->qrc', bbm_im, ca_im[..., :PAIR_W], precision=hp))
    ds = jnp.stack([jnp.pad(kt[..., :PAIR_W - PAIR_IN * i], ((0, 0), (0, 0), (PAIR_IN * i, 0)))
                    for i in range(SS)], axis=1).reshape(PAIRS, PAIR_W, PAIR_W)
    skip = jnp.tile(d.reshape(PAIRS, PAIR_IN), (1, SS))
    ds = ds + skip[:, None, :] * jnp.eye(PAIR_W, dtype=_F32)[None]

    ks8 = SS * jnp.arange(SUBLANES + 1, dtype=_F32)
    mag8 = jnp.exp((lr * dt).reshape(HALF)[None, :] * ks8[:, None])
    ang8 = (li * dt).reshape(HALF)[None, :] * ks8[:, None]
    apow = jnp.concatenate([mag8 * jnp.cos(ang8), mag8 * jnp.sin(ang8)], axis=1)
    sub = jnp.arange(SUBLANES)
    shifted = [jnp.where((sub >= k)[:, None], apow[k][None, :], 0.0) for k in (1, 2, 4)]
    a8 = jnp.stack(shifted + [apow[:SUBLANES], jnp.broadcast_to(apow[SUBLANES][None], (SUBLANES, 2 * HALF))])
    return bs.astype(_BF16), cs.astype(_BF16), ds.astype(_BF16), a8


def kernel(x, meta_tokens, g_mix_pre, g_mix_post, w_in, b_gates, ssm_lambda_re, ssm_lambda_im, ssm_b_re,
           ssm_b_im, ssm_c_re, ssm_c_im, ssm_d, ssm_log_dt, w_ssm_glu, w_qk_conv, b_qk_conv, g_head_norm,
           w_branch_ssm, w_branch_mlstm, w_out, g_ffn_pre, g_ffn_post, w_ffn_gate, w_ffn_up, w_ffn_conv,
           b_ffn_conv, w_ffn_down):
    bsz, seq, _ = x.shape
    assert bsz == 1
    depth = w_in.shape[0]
    assert seq % TL == 0 and FRONT_PAD + N_META == TL
    lp = TL + seq
    assert (lp // SS) % SUBLANES == 0

    head = jnp.concatenate([jnp.zeros((FRONT_PAD, D_MODEL), _F32), meta_tokens.astype(_F32)], axis=0)
    stream = (head, x[0])

    nh2 = 2 * MLSTM_HEADS
    c0, w = SSM_WIDTH, MLSTM_WIDTH
    g0 = c0 + 4 * w
    row = lambda a: a[:, None, :]
    w_u = w_in[:, :, :c0].astype(_BF16)
    w_g = w_in[:, :, g0:g0 + nh2]
    w_gr = jnp.swapaxes(w_g, 1, 2).astype(_BF16)
    w_mix = jnp.concatenate([w_in[:, :, c0:g0], w_in[:, :, g0 + nh2:], w_g,
                             jnp.zeros((depth, D_MODEL, GATE_PAD - nh2), _F32)], axis=-1).astype(_BF16)
    assert w_mix.shape[-1] == WIN_COLS
    b_gc = row(jnp.pad(b_gates, ((0, 0), (0, GATE_PAD - nh2))))
    b_gr = jnp.broadcast_to(b_gates[:, :, None], (depth, nh2, LANES))
    w_cv = jnp.concatenate([w_qk_conv, row(b_qk_conv),
                            jnp.zeros((depth, SUBLANES - QK_CONV - 1, 2 * w), _F32)], axis=1)
    mixer_consts = (row(g_mix_pre), row(g_mix_post), w_mix, w_gr, b_gc, b_gr, w_cv,
                    row(g_head_norm), w_ssm_glu.astype(_BF16), w_branch_ssm.astype(_BF16),
                    w_branch_mlstm.astype(_BF16), w_out.astype(_BF16))
    f_cv = jnp.concatenate([w_ffn_conv, row(b_ffn_conv),
                            jnp.zeros((depth, SUBLANES - FFN_CONV - 1, FFN_DIM), _F32)], axis=1)
    ffn_consts = (row(g_ffn_pre), row(g_ffn_post), w_ffn_gate.astype(_BF16), w_ffn_up.astype(_BF16),
                  w_ffn_down.astype(_BF16), f_cv)
    bs, cs, ds, a8 = jax.vmap(_s5_prep)(ssm_lambda_re, ssm_lambda_im, ssm_b_re, ssm_b_im,
                                        ssm_c_re, ssm_c_im, ssm_d, ssm_log_dt)
    g_pre = row(g_mix_pre)

    u = _s5_in(stream, g_pre, w_u, 0, lp)
    for l in range(depth):
        y5 = _s5_core(u, bs, cs, ds, a8, l)
        h = _mixer(stream, y5, mixer_consts, l, lp)
        if l + 1 < depth:
            h, u = _ffn(h, ffn_consts, l, next_s5=(g_pre, w_u))
        else:
            h = _ffn(h, ffn_consts, l)
        stream = (h,)
    return h[None]
```

```python
import functools
import math

import jax
import jax.numpy as jnp
from jax import lax
from jax.experimental import pallas as pl
from jax.experimental.pallas import tpu as pltpu

D_MODEL = 1024
N_META = 16
SSM_WIDTH = 512
SSM_GROUP = 16
SSM_GROUPS = 32
SSM_STATE = 64
MLSTM_WIDTH = 512
MLSTM_HEADS = 4
MLSTM_HEAD_DIM = 128
MLSTM_REF_CHUNK = 64
QK_CONV = 4
FFN_DIM = 2816
FFN_CONV = 3
NORM_EPS = 1e-6
PAD_LOG_INPUT_GATE = -1e4

LANES = 128
SUBLANES = 8
TL = 512
CH = 128
SS = 8
PAIRS = SSM_GROUPS // 2
PAIR_IN = 2 * SSM_GROUP
PAIR_W = SS * PAIR_IN
PAIR_STATE = 2 * SSM_STATE
PAIRS_PER_TILE = LANES // PAIR_IN
HALF = PAIRS * PAIR_STATE
FRONT_PAD = TL - N_META
MIX_BLOCK = 256
MLSTM_STAGE_CHUNKS = 1
FFN_CHUNK = 256
N_FFN_CHUNKS = FFN_DIM // FFN_CHUNK
FFN_DOWN_GROUP = 4
GATE_PAD = LANES
WIN_V = 2 * MLSTM_WIDTH
WIN_O = WIN_V + MLSTM_WIDTH
WIN_END = WIN_O + MLSTM_WIDTH
VMEM_LIMIT = 56 * 1024 * 1024

_BF16 = jnp.bfloat16
_F32 = jnp.float32


def _rms(x, g):
    return x * lax.rsqrt(jnp.mean(x * x, axis=-1, keepdims=True) + NORM_EPS) * g


def _gelu_tanh(x):
    return 0.5 * x * (1.0 + jnp.tanh(math.sqrt(2.0 / math.pi) * (x + 0.044715 * (x * x * x))))


def _sigmoid(x):
    return 0.5 + 0.5 * jnp.tanh(0.5 * x)


def _log_sigmoid(x):
    return jnp.minimum(x, 0.0) - jnp.log(1.0 + jnp.exp(-jnp.abs(x)))


def _dot(a, b):
    return jnp.dot(a, b, preferred_element_type=_F32)


def _layer_spec(shape, l):
    nd = len(shape) - 1
    return pl.BlockSpec((None,) + tuple(shape[1:]), lambda i, _l=l, _nd=nd: (_l,) + (0,) * _nd)


def _stream_specs(first):
    if first:
        return [pl.BlockSpec((TL, D_MODEL), lambda i: (0, 0)),
                pl.BlockSpec((TL, D_MODEL), lambda i: (jnp.maximum(i - 1, 0), 0))]
    return [pl.BlockSpec((TL, D_MODEL), lambda i: (i, 0))]


def _stream_tile(first, refs):
    if first:
        head_ref, x_ref, *rest = refs
        return jnp.where(pl.program_id(0) == 0, head_ref[...], x_ref[...]), rest
    x_ref, *rest = refs
    return x_ref[...], rest


def _s5_in_kernel(first, *refs):
    x, (g_ref, w_ref, u_ref, us_ref) = _stream_tile(first, refs)
    _s5_project(x, g_ref, w_ref, u_ref, us_ref)


def _s5_project(x, g_ref, w_ref, u_ref, us_ref):
    hn = _rms(x, g_ref[...]).astype(_BF16)
    u = _dot(hn, w_ref[...])
    for t in range(SSM_WIDTH // LANES):
        us_ref[t] = u[:, t * LANES:(t + 1) * LANES]
    for j in range(SS):
        for t in range(SSM_WIDTH // LANES):
            rows = us_ref[t, pl.ds(j, TL // SS, stride=SS), :].astype(_BF16)
            for k in range(PAIRS_PER_TILE):
                u_ref[t * PAIRS_PER_TILE + k, :, j * PAIR_IN:(j + 1) * PAIR_IN] = (
                    rows[:, k * PAIR_IN:(k + 1) * PAIR_IN])


def _s5_in(stream, g, w, l, lp):
    first = len(stream) == 2
    return pl.pallas_call(
        functools.partial(_s5_in_kernel, first),
        grid=(lp // TL,),
        in_specs=_stream_specs(first) + [_layer_spec(g.shape, l), _layer_spec(w.shape, l)],
        out_specs=pl.BlockSpec((PAIRS, TL // SS, PAIR_W), lambda i: (0, i, 0)),
        out_shape=jax.ShapeDtypeStruct((PAIRS, lp // SS, PAIR_W), _BF16),
        scratch_shapes=[pltpu.VMEM((SSM_WIDTH // LANES, TL, LANES), _F32)],
        compiler_params=pltpu.CompilerParams(dimension_semantics=("parallel",),
                                             vmem_limit_bytes=VMEM_LIMIT),
        name="s5_in",
    )(*stream, g, w)


def _s5_core_kernel(rt, u_ref, bs_ref, cs_ref, ds_ref, a_ref, y_ref, vx_ref, st_ref):
    rb = rt // SUBLANES

    @pl.when(pl.program_id(0) == 0)
    def _():
        st_ref[...] = jnp.zeros_like(st_ref)

    for p in range(PAIRS):
        v = _dot(u_ref[p], bs_ref[p])
        vx_ref[:, :, p * PAIR_STATE:(p + 1) * PAIR_STATE] = (
            v[:, :PAIR_STATE].reshape(rb, SUBLANES, PAIR_STATE))
        vx_ref[:, :, HALF + p * PAIR_STATE:HALF + (p + 1) * PAIR_STATE] = (
            v[:, PAIR_STATE:].reshape(rb, SUBLANES, PAIR_STATE))

    def cmul(t, z_re, z_im):
        t_re, t_im = a_ref[t, :, :HALF], a_ref[t, :, HALF:]
        return t_re * z_re - t_im * z_im, t_re * z_im + t_im * z_re

    first = lax.broadcasted_iota(jnp.int32, (SUBLANES, HALF), 0) == 0

    def body(i, carry):
        s_re, s_im = carry
        w_re = vx_ref[i, :, :HALF]
        w_im = vx_ref[i, :, HALF:]
        for t, shift in enumerate((1, 2, 4)):
            d_re, d_im = cmul(t, pltpu.roll(w_re, shift, 0), pltpu.roll(w_im, shift, 0))
            w_re, w_im = w_re + d_re, w_im + d_im
        x_re, x_im = cmul(3, s_re, s_im)
        vx_ref[i, :, :HALF] = x_re + jnp.where(first, 0.0, pltpu.roll(w_re, 1, 0))
        vx_ref[i, :, HALF:] = x_im + jnp.where(first, 0.0, pltpu.roll(w_im, 1, 0))
        n_re, n_im = cmul(4, s_re, s_im)
        last = slice(SUBLANES - 1, SUBLANES)
        return (n_re + jnp.broadcast_to(w_re[last], (SUBLANES, HALF)),
                n_im + jnp.broadcast_to(w_im[last], (SUBLANES, HALF)))

    s_re, s_im = lax.fori_loop(0, rb, body, (st_ref[:, :HALF], st_ref[:, HALF:]))
    st_ref[:, :HALF] = s_re
    st_ref[:, HALF:] = s_im

    for p in range(PAIRS):
        x_re = vx_ref[:, :, p * PAIR_STATE:(p + 1) * PAIR_STATE].reshape(rt, PAIR_STATE)
        x_im = vx_ref[:, :, HALF + p * PAIR_STATE:HALF + (p + 1) * PAIR_STATE].reshape(rt, PAIR_STATE)
        xp = jnp.concatenate([x_re, x_im], axis=-1).astype(_BF16)
        y_ref[p] = (lax.dot_general(xp, cs_ref[p], (((1,), (1,)), ((), ())), preferred_element_type=_F32)
                    + _dot(u_ref[p], ds_ref[p]))


def _s5_core(u, bs, cs, ds, a8, l):
    rows = u.shape[1]
    rt = rows // 8 if (rows // 8) % SUBLANES == 0 else rows
    return pl.pallas_call(
        functools.partial(_s5_core_kernel, rt),
        grid=(rows // rt,),
        in_specs=[
            pl.BlockSpec((PAIRS, rt, PAIR_W), lambda i: (0, i, 0)),
            _layer_spec(bs.shape, l), _layer_spec(cs.shape, l), _layer_spec(ds.shape, l),
            _layer_spec(a8.shape, l),
        ],
        out_specs=pl.BlockSpec((PAIRS, rt, PAIR_W), lambda i: (0, i, 0)),
        out_shape=jax.ShapeDtypeStruct((PAIRS, rows, PAIR_W), _F32),
        scratch_shapes=[
            pltpu.VMEM((rt // SUBLANES, SUBLANES, 2 * HALF), _F32),
            pltpu.VMEM((SUBLANES, 2 * HALF), _F32),
        ],
        compiler_params=pltpu.CompilerParams(dimension_semantics=("arbitrary",),
                                             vmem_limit_bytes=VMEM_LIMIT),
        name="s5_core",
    )(u, bs, cs, ds, a8)


def _mixer_kernel(first, *refs):
    x, rest = _stream_tile(first, refs)
    (y5_ref, gpre_ref, gpost_ref, win_ref, wga_ref, wgb_ref, wgc_ref, wgr_ref, bgc_ref, bgr_ref, wcv_ref,
     ghead_ref, wglu_ref, wa_ref, wb_ref,
     wout_ref, o_ref, st_ref, m_ref, yb_ref, y5s_ref, q_ref, k_ref, v_ref,
     so_ref, sga_ref, sgb_ref, *cv_refs) = rest
    pid = pl.program_id(0)

    @pl.when(pid == 0)
    def _():
        for cv_ref in cv_refs:
            cv_ref[0:SUBLANES, :] = jnp.zeros((SUBLANES, MIX_BLOCK), _F32)
        st_ref[...] = jnp.zeros_like(st_ref)
        m_ref[...] = jnp.zeros_like(m_ref)
        for h in range(MLSTM_HEADS):
            v_ref[h, :, MLSTM_HEAD_DIM:] = jnp.ones((TL, MLSTM_HEAD_DIM), _BF16)

    hn = _rms(x, gpre_ref[...]).astype(_BF16)

    row_c = pid * TL + lax.broadcasted_iota(jnp.int32, (TL, 1), 0)
    valid_c = row_c >= FRONT_PAD
    row_r = pid * TL + lax.broadcasted_iota(jnp.int32, (1, TL), 1)
    valid_r = row_r >= FRONT_PAD

    n_blocks = 2 * MLSTM_WIDTH // MIX_BLOCK

    def proj_block(b):
        cols = slice(b * MIX_BLOCK, (b + 1) * MIX_BLOCK)
        cv_refs[b][SUBLANES:SUBLANES + TL, :] = _dot(hn, win_ref[:, cols])

    def conv_block(b):
        cols = slice(b * MIX_BLOCK, (b + 1) * MIX_BLOCK)
        cv_ref = cv_refs[b]
        acc = wcv_ref[QK_CONV:QK_CONV + 1, cols]
        for j in range(QK_CONV):
            off = SUBLANES - (QK_CONV - 1) + j
            acc = acc + wcv_ref[j:j + 1, cols] * cv_ref[off:off + TL, :]
        cv_ref[0:SUBLANES, :] = cv_ref[TL:TL + SUBLANES, :]
        qk = jnp.where(valid_c, acc * _sigmoid(acc), 0.0)
        if b < MLSTM_WIDTH // MIX_BLOCK:
            q_ref[:, cols] = qk.astype(_BF16)
        else:
            kcols = slice(b * MIX_BLOCK - MLSTM_WIDTH, (b + 1) * MIX_BLOCK - MLSTM_WIDTH)
            k_ref[:, kcols] = qk * (MLSTM_HEAD_DIM ** -0.5)

    gate_w = D_MODEL // n_blocks

    def branch_gate_block(dst_ref, w_ref, b):
        gcols = slice(b * gate_w, (b + 1) * gate_w)
        dst_ref[:, gcols] = _sigmoid(_dot(hn, w_ref[:, gcols]))

    q_blocks = MLSTM_WIDTH // MIX_BLOCK
    for b in range(q_blocks, n_blocks):
        proj_block(b)
    v_all = jnp.where(valid_c, _dot(hn, win_ref[:, WIN_V:WIN_O]), 0.0).astype(_BF16)
    for h in range(MLSTM_HEADS):
        v_ref[h, :, :MLSTM_HEAD_DIM] = v_all[:, h * MLSTM_HEAD_DIM:(h + 1) * MLSTM_HEAD_DIM]
    lane = lax.broadcasted_iota(jnp.int32, (1, GATE_PAD), 1)
    gc = _dot(hn, wgc_ref[...]) + bgc_ref[...]
    gc = jnp.where(lane < MLSTM_HEADS, gc, _log_sigmoid(gc))
    gc = jnp.where(valid_c, gc, jnp.where(lane < MLSTM_HEADS, PAD_LOG_INPUT_GATE, 0.0))
    sub = lax.broadcasted_iota(jnp.int32, (2 * MLSTM_HEADS, 1), 0)
    gr = lax.dot_general(wgr_ref[...], hn, (((1,), (1,)), ((), ())),
                         preferred_element_type=_F32) + bgr_ref[:, 0:1]
    gr = jnp.where(sub < MLSTM_HEADS, gr, _log_sigmoid(gr))
    gr = jnp.where(valid_r, gr, jnp.where(sub < MLSTM_HEADS, PAD_LOG_INPUT_GATE, 0.0))
    for b in range(q_blocks):
        conv_block(q_blocks + b)
        proj_block(b)

    it = lax.broadcasted_iota(jnp.int32, (CH, CH), 0)
    js = lax.broadcasted_iota(jnp.int32, (CH, CH), 1)
    causal = it >= js
    tri_l = causal.astype(_F32)
    tri_u = (it <= js).astype(_F32)

    def regroup_s5(j):
        for t in range(SSM_WIDTH // LANES):
            yj = jnp.concatenate([y5_ref[t * PAIRS_PER_TILE + k, :, j * PAIR_IN:(j + 1) * PAIR_IN]
                                  for k in range(PAIRS_PER_TILE)], axis=-1)
            y5s_ref[t, pl.ds(j, TL // SS, stride=SS), :] = yj

    n_chunks = TL // CH
    hp = lax.Precision.HIGHEST
    rows = [slice(c * CH, (c + 1) * CH) for c in range(n_chunks)]
    heads = [slice(h * MLSTM_HEAD_DIM, (h + 1) * MLSTM_HEAD_DIM) for h in range(MLSTM_HEADS)]

    gcc = [gc[rows[c], :] for c in range(n_chunks)]
    b_c = [jnp.dot(tri_l, gcc[c], preferred_element_type=_F32, precision=hp) for c in range(n_chunks)]
    b_r = [jnp.dot(gr[:, rows[c]], tri_u, preferred_element_type=_F32, precision=hp) for c in range(n_chunks)]

    rrow = [[None] * MLSTM_HEADS for _ in range(n_chunks)]
    m_in = [[None] * MLSTM_HEADS for _ in range(n_chunks)]
    m_out = [[None] * MLSTM_HEADS for _ in range(n_chunks)]
    for h in range(MLSTM_HEADS):
        fl = MLSTM_HEADS + h
        m = m_ref[h:h + 1, :]
        for c in range(n_chunks):
            rrow[c][h] = gr[h:h + 1, rows[c]] - b_r[c][fl:fl + 1, :]
            m_in[c][h] = m
            m_out[c][h] = jnp.maximum(m, jnp.max(rrow[c][h], axis=-1, keepdims=True))
            m = b_r[c][fl:fl + 1, CH - 1:CH] + m_out[c][h]
        m_ref[h:h + 1, :] = m

    st_in = [[None] * MLSTM_HEADS for _ in range(n_chunks)]
    for h in range(MLSTM_HEADS):
        fl = MLSTM_HEADS + h
        st = st_ref[h]
        for c in range(n_chunks):
            st_in[c][h] = st
            rcol = gcc[c][:, h:h + 1] - b_c[c][:, fl:fl + 1]
            kw = k_ref[rows[c], heads[h]] * jnp.exp(rcol - m_out[c][h][:, 0:1])
            decay = jnp.exp(m_in[c][h] - m_out[c][h])
            st = (jnp.concatenate([decay, decay], axis=-1) * st
                  + _dot(kw.T.astype(_BF16), v_ref[h, rows[c], :]))
        st_ref[h] = st

    for b in range(q_blocks):
        conv_block(b)
    so_ref[...] = _sigmoid(_dot(hn, win_ref[:, WIN_O:WIN_END]))

    for c0 in range(0, n_chunks, MLSTM_STAGE_CHUNKS):
        group = range(c0, c0 + MLSTM_STAGE_CHUNKS)
        for c in group:
            branch_gate_block(sga_ref, wga_ref, c)
            branch_gate_block(sgb_ref, wgb_ref, c)
            for j in range(c * SS // n_chunks, (c + 1) * SS // n_chunks):
                regroup_s5(j)
        items = [(c, h) for c in group for h in range(MLSTM_HEADS)]
        qc = [q_ref[rows[c], heads[h]] for c, h in items]
        qk = [lax.dot_general(qc[i], k_ref[rows[c], heads[h]].astype(_BF16), (((1,), (1,)), ((), ())),
                              preferred_element_type=_F32) for i, (c, h) in enumerate(items)]
        qs = [_dot(qc[i], st_in[c][h].astype(_BF16)) for i, (c, h) in enumerate(items)]
        mc1, mcol, s = [], [], []
        for i, (c, h) in enumerate(items):
            rmat = jnp.where(causal, rrow[c][h], -jnp.inf)
            mc1.append(jnp.maximum(m_in[c][h][:, 0:1], jnp.max(rmat, axis=-1, keepdims=True)))
            mcol.append(jnp.broadcast_to(mc1[i], (CH, LANES)))
            s.append((qk[i] * jnp.exp(rmat - mcol[i])).astype(_BF16))
        sv = [_dot(s[i], v_ref[h, rows[c], :]) for i, (c, h) in enumerate(items)]
        for i, (c, h) in enumerate(items):
            hs = heads[h]
            bcol = b_c[c][:, MLSTM_HEADS + h:MLSTM_HEADS + h + 1]
            floor = jnp.broadcast_to(jnp.exp(-(bcol + mc1[i])), (CH, LANES))
            w_inter = jnp.exp(m_in[c][h] - mcol[i])
            num = sv[i][:, :MLSTM_HEAD_DIM] + w_inter * qs[i][:, :MLSTM_HEAD_DIM]
            den = sv[i][:, MLSTM_HEAD_DIM:] + w_inter * qs[i][:, MLSTM_HEAD_DIM:]
            hout = num / jnp.maximum(jnp.abs(den), floor)
            hout = hout * lax.rsqrt(jnp.mean(hout * hout, axis=-1, keepdims=True) + NORM_EPS)
            hout = hout * ghead_ref[:, hs] * so_ref[rows[c], hs]
            yb_ref[rows[c], hs] = hout.astype(_BF16)

    ya = _gelu_tanh(jnp.concatenate([y5s_ref[t] for t in range(SSM_WIDTH // LANES)], axis=-1))
    ya = ya * _sigmoid(_dot(ya.astype(_BF16), wglu_ref[...]))

    merged = (sga_ref[...] * _dot(ya.astype(_BF16), wa_ref[...])
              + sgb_ref[...] * _dot(yb_ref[...], wb_ref[...]))
    o_ref[...] = x + _rms(_dot(merged.astype(_BF16), wout_ref[...]), gpost_ref[...])


def _mixer(stream, y5, consts, l, lp):
    first = len(stream) == 2
    return pl.pallas_call(
        functools.partial(_mixer_kernel, first),
        grid=(lp // TL,),
        in_specs=_stream_specs(first) + [
            pl.BlockSpec((PAIRS, TL // SS, PAIR_W), lambda i: (0, i, 0)),
        ] + [_layer_spec(c.shape, l) for c in consts],
        out_specs=pl.BlockSpec((TL, D_MODEL), lambda i: (i, 0)),
        out_shape=jax.ShapeDtypeStruct((lp, D_MODEL), _F32),
        scratch_shapes=[
            pltpu.VMEM((MLSTM_HEADS, MLSTM_HEAD_DIM, 2 * MLSTM_HEAD_DIM), _F32),
            pltpu.VMEM((SUBLANES, LANES), _F32),
            pltpu.VMEM((TL, MLSTM_WIDTH), _BF16),
            pltpu.VMEM((SSM_WIDTH // LANES, TL, LANES), _F32),
            pltpu.VMEM((TL, MLSTM_WIDTH), _BF16),
            pltpu.VMEM((TL, MLSTM_WIDTH), _F32),
            pltpu.VMEM((MLSTM_HEADS, TL, 2 * MLSTM_HEAD_DIM), _BF16),
            pltpu.VMEM((TL, MLSTM_WIDTH), _F32),
            pltpu.VMEM((TL, D_MODEL), _F32),
            pltpu.VMEM((TL, D_MODEL), _F32),
        ] + [pltpu.VMEM((TL + SUBLANES, MIX_BLOCK), _F32)
             for _ in range(2 * MLSTM_WIDTH // MIX_BLOCK)],
        compiler_params=pltpu.CompilerParams(dimension_semantics=("arbitrary",),
                                             vmem_limit_bytes=VMEM_LIMIT),
        name="mixer",
    )(*stream, y5, *consts)


def _ffn_kernel(fuse_next, n_tiles, *refs):
    if fuse_next:
        (x_ref, gpre_ref, gpost_ref, wg_ref, wu_ref, wd_ref, wcv_ref, gnext_ref, wnext_ref,
         o_ref, u_ref, cv_ref, act_ref, prev_ref, us_ref) = refs
    else:
        x_ref, gpre_ref, gpost_ref, wg_ref, wu_ref, wd_ref, wcv_ref, o_ref, cv_ref, act_ref = refs
    pid = pl.program_id(0)

    @pl.when(pid == 0)
    def _():
        cv_ref[0:SUBLANES, :] = jnp.zeros((SUBLANES, FFN_DIM), _F32)
        if fuse_next:
            prev_ref[...] = jnp.zeros_like(prev_ref)

    def tile():
        if fuse_next:
            _s5_project(prev_ref[...], gnext_ref, wnext_ref, u_ref, us_ref)
        out = _ffn_tile(x_ref[...], gpre_ref, gpost_ref, wg_ref, wu_ref, wd_ref, wcv_ref, cv_ref, act_ref)
        o_ref[...] = out
        if fuse_next:
            prev_ref[...] = out

    if fuse_next:
        pl.when(pid < n_tiles)(tile)

        @pl.when(pid == n_tiles)
        def _():
            _s5_project(prev_ref[...], gnext_ref, wnext_ref, u_ref, us_ref)
    else:
        tile()


def _ffn_tile(x, gpre_ref, gpost_ref, wg_ref, wu_ref, wd_ref, wcv_ref, cv_ref, act_ref):
    hn = _rms(x, gpre_ref[...]).astype(_BF16)
    chunk = lambda c: slice(c * FFN_CHUNK, (c + 1) * FFN_CHUNK)
    acc = None
    group_start = 0
    cv_ref[SUBLANES:SUBLANES + TL, chunk(0)] = _dot(hn, wg_ref[:, chunk(0)])
    up_next = _dot(hn, wu_ref[:, chunk(0)])
    for c in range(N_FFN_CHUNKS):
        cs = chunk(c)
        up = up_next
        if c + 1 < N_FFN_CHUNKS:
            cv_ref[SUBLANES:SUBLANES + TL, chunk(c + 1)] = _dot(hn, wg_ref[:, chunk(c + 1)])
            up_next = _dot(hn, wu_ref[:, chunk(c + 1)])
        conv = wcv_ref[FFN_CONV:FFN_CONV + 1, cs]
        for j in range(FFN_CONV):
            off = SUBLANES - (FFN_CONV - 1) + j
            conv = conv + wcv_ref[j:j + 1, cs] * cv_ref[off:off + TL, cs]
        cv_ref[0:SUBLANES, cs] = cv_ref[TL:TL + SUBLANES, cs]
        act_ref[:, cs] = (_gelu_tanh(conv) * up).astype(_BF16)
        if (c + 1) % FFN_DOWN_GROUP == 0 or c + 1 == N_FFN_CHUNKS:
            gs = slice(group_start * FFN_CHUNK, (c + 1) * FFN_CHUNK)
            part = _dot(act_ref[:, gs], wd_ref[gs, :])
            acc = part if acc is None else acc + part
            group_start = c + 1
    return x + _rms(acc, gpost_ref[...])


def _ffn(x, consts, l, next_s5=None):
    lp = x.shape[0]
    n_tiles = lp // TL
    scratch = [pltpu.VMEM((TL + SUBLANES, FFN_DIM), _F32),
               pltpu.VMEM((TL, FFN_DIM), _BF16)]
    in_specs = [_layer_spec(c.shape, l) for c in consts]
    if next_s5 is None:
        grid = n_tiles
        x_spec = pl.BlockSpec((TL, D_MODEL), lambda i: (i, 0))
        out_specs = pl.BlockSpec((TL, D_MODEL), lambda i: (jnp.maximum(i - 1, 0), 0))
        out_shape = jax.ShapeDtypeStruct((lp - TL, D_MODEL), _F32)
        args = (x, *consts)
    else:
        grid = n_tiles + 1
        x_spec = pl.BlockSpec((TL, D_MODEL), lambda i: (jnp.minimum(i, n_tiles - 1), 0))
        out_specs = [pl.BlockSpec((TL, D_MODEL), lambda i: (jnp.minimum(i, n_tiles - 1), 0)),
                     pl.BlockSpec((PAIRS, TL // SS, PAIR_W), lambda i: (0, jnp.maximum(i - 1, 0), 0))]
        out_shape = [jax.ShapeDtypeStruct((lp, D_MODEL), _F32),
                     jax.ShapeDtypeStruct((PAIRS, lp // SS, PAIR_W), _BF16)]
        in_specs += [_layer_spec(a.shape, l + 1) for a in next_s5]
        scratch += [pltpu.VMEM((TL, D_MODEL), _F32),
                    pltpu.VMEM((SSM_WIDTH // LANES, TL, LANES), _F32)]
        args = (x, *consts, *next_s5)
    return pl.pallas_call(
        functools.partial(_ffn_kernel, next_s5 is not None, n_tiles),
        grid=(grid,),
        in_specs=[x_spec] + in_specs,
        out_specs=out_specs,
        out_shape=out_shape,
        scratch_shapes=scratch,
        compiler_params=pltpu.CompilerParams(dimension_semantics=("arbitrary",),
                                             vmem_limit_bytes=VMEM_LIMIT),
        name="ffn",
    )(*args)


def _s5_prep(lam_re, lam_im, b_re, b_im, c_re, c_im, d, log_dt):
    hp = lax.Precision.HIGHEST
    gh, npow = SSM_GROUP, SS + 1
    lr = lam_re.reshape(PAIRS, PAIR_STATE)
    li = lam_im.reshape(PAIRS, PAIR_STATE)
    dt = jnp.repeat(jnp.exp(log_dt), SSM_STATE).reshape(PAIRS, PAIR_STATE)
    ks = jnp.arange(npow, dtype=_F32)[None, :, None]
    mag = jnp.exp((lr * dt)[:, None, :] * ks)
    ang = (li * dt)[:, None, :] * ks
    ak_re, ak_im = mag * jnp.cos(ang), mag * jnp.sin(ang)
    nr, ni = ak_re[:, 1] - 1.0, ak_im[:, 1]
    den = lr * lr + li * li
    z_re = (nr * lr + ni * li) / den
    z_im = (ni * lr - nr * li) / den
    bt_re = jnp.swapaxes(b_re.reshape(PAIRS, PAIR_STATE, gh), 1, 2)
    bt_im = jnp.swapaxes(b_im.reshape(PAIRS, PAIR_STATE, gh), 1, 2)
    bb_re = z_re[:, None, :] * bt_re - z_im[:, None, :] * bt_im
    bb_im = z_re[:, None, :] * bt_im + z_im[:, None, :] * bt_re
    as_hn = lambda c: c.reshape(PAIRS, 2, gh, SSM_STATE).transpose(0, 2, 1, 3).reshape(PAIRS, gh, PAIR_STATE)
    ct_re, ct_im = as_hn(c_re), as_hn(c_im)
    same_group = (jnp.arange(2)[:, None] == (jnp.arange(PAIR_STATE) // SSM_STATE)[None, :]).astype(_F32)
    same_group = same_group[None, None, :, None, :]

    def times_powers(m_re, m_im, pw_re, pw_im):
        pr, pi = pw_re[:, :, None, None, :], pw_im[:, :, None, None, :]
        mr, mi = m_re[:, None, None, :, :], m_im[:, None, None, :, :]
        shape = (PAIRS, npow * PAIR_IN, PAIR_STATE)
        return (((pr * mr - pi * mi) * same_group).reshape(shape),
                ((pr * mi + pi * mr) * same_group).reshape(shape))

    wk_re, wk_im = times_powers(bb_re, bb_im, ak_re[:, ::-1], ak_im[:, ::-1])
    bs = jnp.concatenate([wk_re[:, PAIR_IN:], wk_im[:, PAIR_IN:]], axis=-1)

    ca_re, ca_im = times_powers(ct_re, ct_im, ak_re, ak_im)
    cs = jnp.concatenate([ca_re[:, PAIR_IN:], -ca_im[:, PAIR_IN:]], axis=-1)

    bbm_re, bbm_im = wk_re[:, SS * PAIR_IN:], wk_im[:, SS * PAIR_IN:]
    kt = (jnp.einsum('qrn,qcn->qrc', bbm_re, ca_re[:, :PAIR_W], precision=hp)
          - jnp.einsum('qrn,qcn->qrc', bbm_im, ca_im[:, :PAIR_W], precision=hp))
    ds = jnp.stack([jnp.pad(kt[..., :PAIR_W - PAIR_IN * i], ((0, 0), (0, 0), (PAIR_IN * i, 0)))
                    for i in range(SS)], axis=1).reshape(PAIRS, PAIR_W, PAIR_W)
    skip = jnp.tile(d.reshape(PAIRS, PAIR_IN), (1, SS))
    ds = ds + skip[:, None, :] * jnp.eye(PAIR_W, dtype=_F32)[None]

    ks8 = SS * jnp.arange(SUBLANES + 1, dtype=_F32)
    mag8 = jnp.exp((lr * dt).reshape(HALF)[None, :] * ks8[:, None])
    ang8 = (li * dt).reshape(HALF)[None, :] * ks8[:, None]
    apow = jnp.concatenate([mag8 * jnp.cos(ang8), mag8 * jnp.sin(ang8)], axis=1)
    sub = jnp.arange(SUBLANES)
    shifted = [jnp.where((sub >= k)[:, None], apow[k][None, :], 0.0) for k in (1, 2, 4)]
    a8 = jnp.stack(shifted + [apow[:SUBLANES], jnp.broadcast_to(apow[SUBLANES][None], (SUBLANES, 2 * HALF))])
    return bs.astype(_BF16), cs.astype(_BF16), ds.astype(_BF16), a8


def kernel(x, meta_tokens, g_mix_pre, g_mix_post, w_in, b_gates, ssm_lambda_re, ssm_lambda_im, ssm_b_re,
           ssm_b_im, ssm_c_re, ssm_c_im, ssm_d, ssm_log_dt, w_ssm_glu, w_qk_conv, b_qk_conv, g_head_norm,
           w_branch_ssm, w_branch_mlstm, w_out, g_ffn_pre, g_ffn_post, w_ffn_gate, w_ffn_up, w_ffn_conv,
           b_ffn_conv, w_ffn_down):
    bsz, seq, _ = x.shape
    assert bsz == 1
    depth = w_in.shape[0]
    assert seq % TL == 0 and FRONT_PAD + N_META == TL
    lp = TL + seq
    assert (lp // SS) % SUBLANES == 0

    head = jnp.concatenate([jnp.zeros((FRONT_PAD, D_MODEL), _F32), meta_tokens.astype(_F32)], axis=0)
    stream = (head, x[0])

    nh2 = 2 * MLSTM_HEADS
    c0, w = SSM_WIDTH, MLSTM_WIDTH
    g0 = c0 + 4 * w
    row = lambda a: a[:, None, :]
    w_u = w_in[:, :, :c0].astype(_BF16)
    w_g = w_in[:, :, g0:g0 + nh2]
    w_gr = jnp.swapaxes(w_g, 1, 2).astype(_BF16)
    w_qkvo = w_in[:, :, c0:g0].astype(_BF16)
    w_gc = jnp.pad(w_g, ((0, 0), (0, 0), (0, GATE_PAD - nh2))).astype(_BF16)
    w_ga = w_in[:, :, g0 + nh2:g0 + nh2 + D_MODEL].astype(_BF16)
    w_gb = w_in[:, :, g0 + nh2 + D_MODEL:].astype(_BF16)
    b_gc = row(jnp.pad(b_gates, ((0, 0), (0, GATE_PAD - nh2))))
    b_gr = jnp.broadcast_to(b_gates[:, :, None], (depth, nh2, LANES))
    w_cv = jnp.concatenate([w_qk_conv, row(b_qk_conv),
                            jnp.zeros((depth, SUBLANES - QK_CONV - 1, 2 * w), _F32)], axis=1)
    mixer_consts = (row(g_mix_pre), row(g_mix_post), w_qkvo, w_ga, w_gb, w_gc, w_gr, b_gc, b_gr, w_cv,
                    row(g_head_norm), w_ssm_glu.astype(_BF16), w_branch_ssm.astype(_BF16),
                    w_branch_mlstm.astype(_BF16), w_out.astype(_BF16))
    f_cv = jnp.concatenate([w_ffn_conv, row(b_ffn_conv),
                            jnp.zeros((depth, SUBLANES - FFN_CONV - 1, FFN_DIM), _F32)], axis=1)
    ffn_consts = (row(g_ffn_pre), row(g_ffn_post), w_ffn_gate.astype(_BF16), w_ffn_up.astype(_BF16),
                  w_ffn_down.astype(_BF16), f_cv)
    bs, cs, ds, a8 = jax.vmap(_s5_prep)(ssm_lambda_re, ssm_lambda_im, ssm_b_re, ssm_b_im,
                                        ssm_c_re, ssm_c_im, ssm_d, ssm_log_dt)
    g_pre = row(g_mix_pre)

    u = _s5_in(stream, g_pre, w_u, 0, lp)
    for l in range(depth):
        y5 = _s5_core(u, bs, cs, ds, a8, l)
        h = _mixer(stream, y5, mixer_consts, l, lp)
        if l + 1 < depth:
            h, u = _ffn(h, ffn_consts, l, next_s5=(g_pre, w_u))
        else:
            h = _ffn(h, ffn_consts, l)
        stream = (h,)
    return h[None]
```

```python
import functools
import math

import jax
import jax.numpy as jnp
from jax import lax
from jax.experimental import pallas as pl
from jax.experimental.pallas import tpu as pltpu

D_MODEL = 1024
N_META = 16
SSM_WIDTH = 512
SSM_GROUP = 16
SSM_GROUPS = 32
SSM_STATE = 64
MLSTM_WIDTH = 512
MLSTM_HEADS = 4
MLSTM_HEAD_DIM = 128
MLSTM_REF_CHUNK = 64
QK_CONV = 4
FFN_DIM = 2816
FFN_CONV = 3
NORM_EPS = 1e-6
PAD_LOG_INPUT_GATE = -1e4

LANES = 128
SUBLANES = 8
TL = 512
CH = 128
SS = 8
PAIRS = SSM_GROUPS // 2
PAIR_IN = 2 * SSM_GROUP
PAIR_W = SS * PAIR_IN
PAIR_STATE = 2 * SSM_STATE
PAIRS_PER_TILE = LANES // PAIR_IN
HALF = PAIRS * PAIR_STATE
FRONT_PAD = TL - N_META
MIX_BLOCK = 256
MLSTM_STAGE_CHUNKS = 1
S5_ROW_BLOCK = 96
FFN_CHUNK = 256
N_FFN_CHUNKS = FFN_DIM // FFN_CHUNK
FFN_DOWN_GROUP = 4
GATE_PAD = LANES
WIN_V = 2 * MLSTM_WIDTH
WIN_O = WIN_V + MLSTM_WIDTH
WIN_GA = WIN_O + MLSTM_WIDTH
WIN_GB = WIN_GA + D_MODEL
WIN_GATES = WIN_GB + D_MODEL
WIN_COLS = WIN_GATES + GATE_PAD
W_IN_COLS = SSM_WIDTH + 4 * MLSTM_WIDTH + 2 * MLSTM_HEADS + 2 * D_MODEL
W_IN_GATES = SSM_WIDTH + 4 * MLSTM_WIDTH
REGROUP_ROWS = 256
VMEM_LIMIT = 56 * 1024 * 1024

_BF16 = jnp.bfloat16
_F32 = jnp.float32


def _rms(x, g):
    return x * lax.rsqrt(jnp.mean(x * x, axis=-1, keepdims=True) + NORM_EPS) * g


def _gelu_tanh(x):
    return 0.5 * x * (1.0 + jnp.tanh(math.sqrt(2.0 / math.pi) * (x + 0.044715 * (x * x * x))))


def _sigmoid(x):
    return 0.5 + 0.5 * jnp.tanh(0.5 * x)


def _log_sigmoid(x):
    return jnp.minimum(x, 0.0) - jnp.log(1.0 + jnp.exp(-jnp.abs(x)))


def _dot(a, b):
    return jnp.dot(a, b, preferred_element_type=_F32)


def _layer_spec(shape, l):
    nd = len(shape) - 1
    return pl.BlockSpec((None,) + tuple(shape[1:]), lambda i, _l=l, _nd=nd: (_l,) + (0,) * _nd)


def _stream_specs(first):
    if first:
        return [pl.BlockSpec((TL, D_MODEL), lambda i: (0, 0)),
                pl.BlockSpec((TL, D_MODEL), lambda i: (jnp.maximum(i - 1, 0), 0))]
    return [pl.BlockSpec((TL, D_MODEL), lambda i: (i, 0))]


def _stream_tile(first, refs):
    if first:
        head_ref, x_ref, *rest = refs
        return jnp.where(pl.program_id(0) == 0, head_ref[...], x_ref[...]), rest
    x_ref, *rest = refs
    return x_ref[...], rest


def _s5_in_kernel(first, *refs):
    x, (g_ref, w_ref, u_ref, us_ref) = _stream_tile(first, refs)
    _s5_project(x, g_ref, w_ref, u_ref, us_ref)


def _s5_project(x, g_ref, w_ref, u_ref, us_ref):
    hn = _rms(x, g_ref[...]).astype(_BF16)
    u = _dot(hn, w_ref[...])
    for t in range(SSM_WIDTH // LANES):
        us_ref[t] = u[:, t * LANES:(t + 1) * LANES]
    for j in range(SS):
        for t in range(SSM_WIDTH // LANES):
            rows = us_ref[t, pl.ds(j, TL // SS, stride=SS), :].astype(_BF16)
            for k in range(PAIRS_PER_TILE):
                u_ref[t * PAIRS_PER_TILE + k, :, j * PAIR_IN:(j + 1) * PAIR_IN] = (
                    rows[:, k * PAIR_IN:(k + 1) * PAIR_IN])


def _s5_in(stream, g, w, l, lp):
    first = len(stream) == 2
    return pl.pallas_call(
        functools.partial(_s5_in_kernel, first),
        grid=(lp // TL,),
        in_specs=_stream_specs(first) + [_layer_spec(g.shape, l), _layer_spec(w.shape, l)],
        out_specs=pl.BlockSpec((PAIRS, TL // SS, PAIR_W), lambda i: (0, i, 0)),
        out_shape=jax.ShapeDtypeStruct((PAIRS, lp // SS, PAIR_W), _BF16),
        scratch_shapes=[pltpu.VMEM((SSM_WIDTH // LANES, TL, LANES), _F32)],
        compiler_params=pltpu.CompilerParams(dimension_semantics=("parallel",),
                                             vmem_limit_bytes=VMEM_LIMIT),
        name="s5_in",
    )(*stream, g, w)


def _s5_core_kernel(rt, u_ref, bs_ref, cs_ref, ds_ref, a_ref, y_ref, vx_ref, st_ref):
    @pl.when(pl.program_id(0) == 0)
    def _():
        st_ref[...] = jnp.zeros_like(st_ref)

    bounds = list(range(0, rt, S5_ROW_BLOCK)) + [rt]
    blocks = [(bounds[k], bounds[k + 1]) for k in range(len(bounds) - 1)]

    def state_input(r0, r1):
        g0, g1 = r0 // SUBLANES, r1 // SUBLANES
        for p in range(PAIRS):
            v = _dot(u_ref[p, r0:r1, :], bs_ref[p])
            vx_ref[g0:g1, :, p * PAIR_STATE:(p + 1) * PAIR_STATE] = (
                v[:, :PAIR_STATE].reshape(g1 - g0, SUBLANES, PAIR_STATE))
            vx_ref[g0:g1, :, HALF + p * PAIR_STATE:HALF + (p + 1) * PAIR_STATE] = (
                v[:, PAIR_STATE:].reshape(g1 - g0, SUBLANES, PAIR_STATE))

    def output(r0, r1):
        g0, g1 = r0 // SUBLANES, r1 // SUBLANES
        for p in range(PAIRS):
            x_re = vx_ref[g0:g1, :, p * PAIR_STATE:(p + 1) * PAIR_STATE].reshape(r1 - r0, PAIR_STATE)
            x_im = vx_ref[g0:g1, :, HALF + p * PAIR_STATE:HALF + (p + 1) * PAIR_STATE].reshape(r1 - r0, PAIR_STATE)
            xp = jnp.concatenate([x_re, x_im], axis=-1).astype(_BF16)
            y_ref[p, r0:r1, :] = (
                lax.dot_general(xp, cs_ref[p], (((1,), (1,)), ((), ())), preferred_element_type=_F32)
                + _dot(u_ref[p, r0:r1, :], ds_ref[p]))

    def cmul(t, z_re, z_im):
        t_re, t_im = a_ref[t, :, :HALF], a_ref[t, :, HALF:]
        return t_re * z_re - t_im * z_im, t_re * z_im + t_im * z_re

    first = lax.broadcasted_iota(jnp.int32, (SUBLANES, HALF), 0) == 0

    def body(i, carry):
        s_re, s_im = carry
        w_re = vx_ref[i, :, :HALF]
        w_im = vx_ref[i, :, HALF:]
        for t, shift in enumerate((1, 2, 4)):
            d_re, d_im = cmul(t, pltpu.roll(w_re, shift, 0), pltpu.roll(w_im, shift, 0))
            w_re, w_im = w_re + d_re, w_im + d_im
        x_re, x_im = cmul(3, s_re, s_im)
        vx_ref[i, :, :HALF] = x_re + jnp.where(first, 0.0, pltpu.roll(w_re, 1, 0))
        vx_ref[i, :, HALF:] = x_im + jnp.where(first, 0.0, pltpu.roll(w_im, 1, 0))
        n_re, n_im = cmul(4, s_re, s_im)
        last = slice(SUBLANES - 1, SUBLANES)
        return (n_re + jnp.broadcast_to(w_re[last], (SUBLANES, HALF)),
                n_im + jnp.broadcast_to(w_im[last], (SUBLANES, HALF)))

    carry = (st_ref[:, :HALF], st_ref[:, HALF:])
    state_input(*blocks[0])
    for k, (r0, r1) in enumerate(blocks):
        if k + 1 < len(blocks):
            state_input(*blocks[k + 1])
        for i in range(r0 // SUBLANES, r1 // SUBLANES):
            carry = body(i, carry)
        output(r0, r1)
    st_ref[:, :HALF] = carry[0]
    st_ref[:, HALF:] = carry[1]


def _s5_core(u, bs, cs, ds, a8, l):
    rows = u.shape[1]
    rt = rows // 8 if (rows // 8) % SUBLANES == 0 else rows
    return pl.pallas_call(
        functools.partial(_s5_core_kernel, rt),
        grid=(rows // rt,),
        in_specs=[
            pl.BlockSpec((PAIRS, rt, PAIR_W), lambda i: (0, i, 0)),
            _layer_spec(bs.shape, l), _layer_spec(cs.shape, l), _layer_spec(ds.shape, l),
            _layer_spec(a8.shape, l),
        ],
        out_specs=pl.BlockSpec((PAIRS, rt, PAIR_W), lambda i: (0, i, 0)),
        out_shape=jax.ShapeDtypeStruct((PAIRS, rows, PAIR_W), _F32),
        scratch_shapes=[
            pltpu.VMEM((rt // SUBLANES, SUBLANES, 2 * HALF), _F32),
            pltpu.VMEM((SUBLANES, 2 * HALF), _F32),
        ],
        compiler_params=pltpu.CompilerParams(dimension_semantics=("arbitrary",),
                                             vmem_limit_bytes=VMEM_LIMIT),
        name="s5_core",
    )(u, bs, cs, ds, a8)


def _mixer_kernel(first, *refs):
    x, rest = _stream_tile(first, refs)
    (y5_ref, gpre_ref, gpost_ref, win_ref, wgr_ref, bgc_ref, bgr_ref, wcv_ref,
     ghead_ref, wglu_ref, wa_ref, wb_ref,
     wout_ref, o_ref, st_ref, m_ref, yb_ref, y5s_ref, q_ref, k_ref, v_ref,
     so_ref, sga_ref, sgb_ref, *cv_refs) = rest
    pid = pl.program_id(0)

    @pl.when(pid == 0)
    def _():
        for cv_ref in cv_refs:
            cv_ref[0:SUBLANES, :] = jnp.zeros((SUBLANES, MIX_BLOCK), _F32)
        st_ref[...] = jnp.zeros_like(st_ref)
        m_ref[...] = jnp.zeros_like(m_ref)
        for h in range(MLSTM_HEADS):
            v_ref[h, :, MLSTM_HEAD_DIM:] = jnp.ones((TL, MLSTM_HEAD_DIM), _BF16)

    hn = _rms(x, gpre_ref[...]).astype(_BF16)

    row_c = pid * TL + lax.broadcasted_iota(jnp.int32, (TL, 1), 0)
    valid_c = row_c >= FRONT_PAD
    row_r = pid * TL + lax.broadcasted_iota(jnp.int32, (1, TL), 1)
    valid_r = row_r >= FRONT_PAD

    n_blocks = 2 * MLSTM_WIDTH // MIX_BLOCK

    def proj_block(b):
        cols = slice(b * MIX_BLOCK, (b + 1) * MIX_BLOCK)
        cv_refs[b][SUBLANES:SUBLANES + TL, :] = _dot(hn, win_ref[:, cols])

    def conv_block(b):
        cols = slice(b * MIX_BLOCK, (b + 1) * MIX_BLOCK)
        cv_ref = cv_refs[b]
        acc = wcv_ref[QK_CONV:QK_CONV + 1, cols]
        for j in range(QK_CONV):
            off = SUBLANES - (QK_CONV - 1) + j
            acc = acc + wcv_ref[j:j + 1, cols] * cv_ref[off:off + TL, :]
        cv_ref[0:SUBLANES, :] = cv_ref[TL:TL + SUBLANES, :]
        qk = jnp.where(valid_c, acc * _sigmoid(acc), 0.0)
        if b < MLSTM_WIDTH // MIX_BLOCK:
            q_ref[:, cols] = qk.astype(_BF16)
        else:
            kcols = slice(b * MIX_BLOCK - MLSTM_WIDTH, (b + 1) * MIX_BLOCK - MLSTM_WIDTH)
            k_ref[:, kcols] = qk * (MLSTM_HEAD_DIM ** -0.5)

    gate_w = D_MODEL // n_blocks

    def branch_gate_block(dst_ref, base, b):
        dst_ref[:, b * gate_w:(b + 1) * gate_w] = _sigmoid(
            _dot(hn, win_ref[:, base + b * gate_w:base + (b + 1) * gate_w]))

    q_blocks = MLSTM_WIDTH // MIX_BLOCK
    for b in range(q_blocks, n_blocks):
        proj_block(b)
    v_all = jnp.where(valid_c, _dot(hn, win_ref[:, WIN_V:WIN_O]), 0.0).astype(_BF16)
    for h in range(MLSTM_HEADS):
        v_ref[h, :, :MLSTM_HEAD_DIM] = v_all[:, h * MLSTM_HEAD_DIM:(h + 1) * MLSTM_HEAD_DIM]
    lane = lax.broadcasted_iota(jnp.int32, (1, GATE_PAD), 1)
    gc = _dot(hn, win_ref[:, WIN_GATES:]) + bgc_ref[...]
    gc = jnp.where(lane < MLSTM_HEADS, gc, _log_sigmoid(gc))
    gc = jnp.where(valid_c, gc, jnp.where(lane < MLSTM_HEADS, PAD_LOG_INPUT_GATE, 0.0))
    sub = lax.broadcasted_iota(jnp.int32, (2 * MLSTM_HEADS, 1), 0)
    gr = lax.dot_general(wgr_ref[...], hn, (((1,), (1,)), ((), ())),
                         preferred_element_type=_F32) + bgr_ref[:, 0:1]
    gr = jnp.where(sub < MLSTM_HEADS, gr, _log_sigmoid(gr))
    gr = jnp.where(valid_r, gr, jnp.where(sub < MLSTM_HEADS, PAD_LOG_INPUT_GATE, 0.0))
    for b in range(q_blocks):
        conv_block(q_blocks + b)
        proj_block(b)

    it = lax.broadcasted_iota(jnp.int32, (CH, CH), 0)
    js = lax.broadcasted_iota(jnp.int32, (CH, CH), 1)
    causal = it >= js
    tri_l = causal.astype(_F32)
    tri_u = (it <= js).astype(_F32)

    def regroup_s5(j):
        for t in range(SSM_WIDTH // LANES):
            yj = jnp.concatenate([y5_ref[t * PAIRS_PER_TILE + k, :, j * PAIR_IN:(j + 1) * PAIR_IN]
                                  for k in range(PAIRS_PER_TILE)], axis=-1)
            y5s_ref[t, pl.ds(j, TL // SS, stride=SS), :] = yj

    n_chunks = TL // CH
    hp = lax.Precision.HIGHEST
    rows = [slice(c * CH, (c + 1) * CH) for c in range(n_chunks)]
    heads = [slice(h * MLSTM_HEAD_DIM, (h + 1) * MLSTM_HEAD_DIM) for h in range(MLSTM_HEADS)]

    gcc = [gc[rows[c], :] for c in range(n_chunks)]
    b_c = [jnp.dot(tri_l, gcc[c], preferred_element_type=_F32, precision=hp) for c in range(n_chunks)]
    b_r = [jnp.dot(gr[:, rows[c]], tri_u, preferred_element_type=_F32, precision=hp) for c in range(n_chunks)]

    rrow = [[None] * MLSTM_HEADS for _ in range(n_chunks)]
    m_in = [[None] * MLSTM_HEADS for _ in range(n_chunks)]
    m_out = [[None] * MLSTM_HEADS for _ in range(n_chunks)]
    for h in range(MLSTM_HEADS):
        fl = MLSTM_HEADS + h
        m = m_ref[h:h + 1, :]
        for c in range(n_chunks):
            rrow[c][h] = gr[h:h + 1, rows[c]] - b_r[c][fl:fl + 1, :]
            m_in[c][h] = m
            m_out[c][h] = jnp.maximum(m, jnp.max(rrow[c][h], axis=-1, keepdims=True))
            m = b_r[c][fl:fl + 1, CH - 1:CH] + m_out[c][h]
        m_ref[h:h + 1, :] = m

    st_in = [[None] * MLSTM_HEADS for _ in range(n_chunks)]
    for h in range(MLSTM_HEADS):
        fl = MLSTM_HEADS + h
        st = st_ref[h]
        for c in range(n_chunks):
            st_in[c][h] = st
            rcol = gcc[c][:, h:h + 1] - b_c[c][:, fl:fl + 1]
            kw = k_ref[rows[c], heads[h]] * jnp.exp(rcol - m_out[c][h][:, 0:1])
            decay = jnp.exp(m_in[c][h] - m_out[c][h])
            st = (jnp.concatenate([decay, decay], axis=-1) * st
                  + _dot(kw.T.astype(_BF16), v_ref[h, rows[c], :]))
        st_ref[h] = st

    for b in range(q_blocks):
        conv_block(b)
    so_ref[...] = _sigmoid(_dot(hn, win_ref[:, WIN_O:WIN_GA]))

    for c0 in range(0, n_chunks, MLSTM_STAGE_CHUNKS):
        group = range(c0, c0 + MLSTM_STAGE_CHUNKS)
        for c in group:
            branch_gate_block(sga_ref, WIN_GA, c)
            branch_gate_block(sgb_ref, WIN_GB, c)
            for j in range(c * SS // n_chunks, (c + 1) * SS // n_chunks):
                regroup_s5(j)
        items = [(c, h) for c in group for h in range(MLSTM_HEADS)]
        qc = [q_ref[rows[c], heads[h]] for c, h in items]
        qk = [lax.dot_general(qc[i], k_ref[rows[c], heads[h]].astype(_BF16), (((1,), (1,)), ((), ())),
                              preferred_element_type=_F32) for i, (c, h) in enumerate(items)]
        qs = [_dot(qc[i], st_in[c][h].astype(_BF16)) for i, (c, h) in enumerate(items)]
        mc1, mcol, s = [], [], []
        for i, (c, h) in enumerate(items):
            rmat = jnp.where(causal, rrow[c][h], -jnp.inf)
            mc1.append(jnp.maximum(m_in[c][h][:, 0:1], jnp.max(rmat, axis=-1, keepdims=True)))
            mcol.append(jnp.broadcast_to(mc1[i], (CH, LANES)))
            s.append((qk[i] * jnp.exp(rmat - mcol[i])).astype(_BF16))
        sv = [_dot(s[i], v_ref[h, rows[c], :]) for i, (c, h) in enumerate(items)]
        for i, (c, h) in enumerate(items):
            hs = heads[h]
            bcol = b_c[c][:, MLSTM_HEADS + h:MLSTM_HEADS + h + 1]
            floor = jnp.broadcast_to(jnp.exp(-(bcol + mc1[i])), (CH, LANES))
            w_inter = jnp.exp(m_in[c][h] - mcol[i])
            num = sv[i][:, :MLSTM_HEAD_DIM] + w_inter * qs[i][:, :MLSTM_HEAD_DIM]
            den = sv[i][:, MLSTM_HEAD_DIM:] + w_inter * qs[i][:, MLSTM_HEAD_DIM:]
            hout = num / jnp.maximum(jnp.abs(den), floor)
            hout = hout * lax.rsqrt(jnp.mean(hout * hout, axis=-1, keepdims=True) + NORM_EPS)
            hout = hout * ghead_ref[:, hs] * so_ref[rows[c], hs]
            yb_ref[rows[c], hs] = hout.astype(_BF16)

    ya = _gelu_tanh(jnp.concatenate([y5s_ref[t] for t in range(SSM_WIDTH // LANES)], axis=-1))
    ya = ya * _sigmoid(_dot(ya.astype(_BF16), wglu_ref[...]))

    merged = (sga_ref[...] * _dot(ya.astype(_BF16), wa_ref[...])
              + sgb_ref[...] * _dot(yb_ref[...], wb_ref[...]))
    o_ref[...] = x + _rms(_dot(merged.astype(_BF16), wout_ref[...]), gpost_ref[...])


def _mixer(stream, y5, consts, l, lp):
    first = len(stream) == 2
    return pl.pallas_call(
        functools.partial(_mixer_kernel, first),
        grid=(lp // TL,),
        in_specs=_stream_specs(first) + [
            pl.BlockSpec((PAIRS, TL // SS, PAIR_W), lambda i: (0, i, 0)),
        ] + [_layer_spec(c.shape, l) for c in consts],
        out_specs=pl.BlockSpec((TL, D_MODEL), lambda i: (i, 0)),
        out_shape=jax.ShapeDtypeStruct((lp, D_MODEL), _F32),
        scratch_shapes=[
            pltpu.VMEM((MLSTM_HEADS, MLSTM_HEAD_DIM, 2 * MLSTM_HEAD_DIM), _F32),
            pltpu.VMEM((SUBLANES, LANES), _F32),
            pltpu.VMEM((TL, MLSTM_WIDTH), _BF16),
            pltpu.VMEM((SSM_WIDTH // LANES, TL, LANES), _F32),
            pltpu.VMEM((TL, MLSTM_WIDTH), _BF16),
            pltpu.VMEM((TL, MLSTM_WIDTH), _F32),
            pltpu.VMEM((MLSTM_HEADS, TL, 2 * MLSTM_HEAD_DIM), _BF16),
            pltpu.VMEM((TL, MLSTM_WIDTH), _F32),
            pltpu.VMEM((TL, D_MODEL), _F32),
            pltpu.VMEM((TL, D_MODEL), _F32),
        ] + [pltpu.VMEM((TL + SUBLANES, MIX_BLOCK), _F32)
             for _ in range(2 * MLSTM_WIDTH // MIX_BLOCK)],
        compiler_params=pltpu.CompilerParams(dimension_semantics=("arbitrary",),
                                             vmem_limit_bytes=VMEM_LIMIT),
        name="mixer",
    )(*stream, y5, *consts)


def _ffn_kernel(fuse_next, n_tiles, *refs):
    if fuse_next:
        (x_ref, gpre_ref, gpost_ref, wg_ref, wu_ref, wd_ref, wcv_ref, gnext_ref, wnext_ref,
         o_ref, u_ref, cv_ref, act_ref, prev_ref, us_ref) = refs
    else:
        x_ref, gpre_ref, gpost_ref, wg_ref, wu_ref, wd_ref, wcv_ref, o_ref, cv_ref, act_ref = refs
    pid = pl.program_id(0)

    @pl.when(pid == 0)
    def _():
        cv_ref[0:SUBLANES, :] = jnp.zeros((SUBLANES, FFN_DIM), _F32)
        if fuse_next:
            prev_ref[...] = jnp.zeros_like(prev_ref)

    def tile():
        if fuse_next:
            _s5_project(prev_ref[...], gnext_ref, wnext_ref, u_ref, us_ref)
        out = _ffn_tile(x_ref[...], gpre_ref, gpost_ref, wg_ref, wu_ref, wd_ref, wcv_ref, cv_ref, act_ref)
        o_ref[...] = out
        if fuse_next:
            prev_ref[...] = out

    if fuse_next:
        pl.when(pid < n_tiles)(tile)

        @pl.when(pid == n_tiles)
        def _():
            _s5_project(prev_ref[...], gnext_ref, wnext_ref, u_ref, us_ref)
    else:
        tile()


def _ffn_tile(x, gpre_ref, gpost_ref, wg_ref, wu_ref, wd_ref, wcv_ref, cv_ref, act_ref):
    hn = _rms(x, gpre_ref[...]).astype(_BF16)
    chunk = lambda c: slice(c * FFN_CHUNK, (c + 1) * FFN_CHUNK)
    acc = None
    group_start = 0
    cv_ref[SUBLANES:SUBLANES + TL, chunk(0)] = _dot(hn, wg_ref[:, chunk(0)])
    up_next = _dot(hn, wu_ref[:, chunk(0)])
    for c in range(N_FFN_CHUNKS):
        cs = chunk(c)
        up = up_next
        if c + 1 < N_FFN_CHUNKS:
            cv_ref[SUBLANES:SUBLANES + TL, chunk(c + 1)] = _dot(hn, wg_ref[:, chunk(c + 1)])
            up_next = _dot(hn, wu_ref[:, chunk(c + 1)])
        conv = wcv_ref[FFN_CONV:FFN_CONV + 1, cs]
        for j in range(FFN_CONV):
            off = SUBLANES - (FFN_CONV - 1) + j
            conv = conv + wcv_ref[j:j + 1, cs] * cv_ref[off:off + TL, cs]
        cv_ref[0:SUBLANES, cs] = cv_ref[TL:TL + SUBLANES, cs]
        act_ref[:, cs] = (_gelu_tanh(conv) * up).astype(_BF16)
        if (c + 1) % FFN_DOWN_GROUP == 0 or c + 1 == N_FFN_CHUNKS:
            gs = slice(group_start * FFN_CHUNK, (c + 1) * FFN_CHUNK)
            part = _dot(act_ref[:, gs], wd_ref[gs, :])
            acc = part if acc is None else acc + part
            group_start = c + 1
    return x + _rms(acc, gpost_ref[...])


def _ffn(x, consts, l, next_s5=None):
    lp = x.shape[0]
    n_tiles = lp // TL
    scratch = [pltpu.VMEM((TL + SUBLANES, FFN_DIM), _F32),
               pltpu.VMEM((TL, FFN_DIM), _BF16)]
    in_specs = [_layer_spec(c.shape, l) for c in consts]
    if next_s5 is None:
        grid = n_tiles
        x_spec = pl.BlockSpec((TL, D_MODEL), lambda i: (i, 0))
        out_specs = pl.BlockSpec((TL, D_MODEL), lambda i: (jnp.maximum(i - 1, 0), 0))
        out_shape = jax.ShapeDtypeStruct((lp - TL, D_MODEL), _F32)
        args = (x, *consts)
    else:
        grid = n_tiles + 1
        x_spec = pl.BlockSpec((TL, D_MODEL), lambda i: (jnp.minimum(i, n_tiles - 1), 0))
        out_specs = [pl.BlockSpec((TL, D_MODEL), lambda i: (jnp.minimum(i, n_tiles - 1), 0)),
                     pl.BlockSpec((PAIRS, TL // SS, PAIR_W), lambda i: (0, jnp.maximum(i - 1, 0), 0))]
        out_shape = [jax.ShapeDtypeStruct((lp, D_MODEL), _F32),
                     jax.ShapeDtypeStruct((PAIRS, lp // SS, PAIR_W), _BF16)]
        in_specs += [_layer_spec(a.shape, l + 1) for a in next_s5]
        scratch += [pltpu.VMEM((TL, D_MODEL), _F32),
                    pltpu.VMEM((SSM_WIDTH // LANES, TL, LANES), _F32)]
        args = (x, *consts, *next_s5)
    return pl.pallas_call(
        functools.partial(_ffn_kernel, next_s5 is not None, n_tiles),
        grid=(grid,),
        in_specs=[x_spec] + in_specs,
        out_specs=out_specs,
        out_shape=out_shape,
        scratch_shapes=scratch,
        compiler_params=pltpu.CompilerParams(dimension_semantics=("arbitrary",),
                                             vmem_limit_bytes=VMEM_LIMIT),
        name="ffn",
    )(*args)


def _regroup_w_in_kernel(w_ref, wu_ref, wmix_ref):
    rb = w_ref.shape[0]
    wu_ref[...] = w_ref[:, :SSM_WIDTH].astype(_BF16)
    wmix_ref[:, :WIN_GA] = w_ref[:, SSM_WIDTH:W_IN_GATES].astype(_BF16)
    tail = W_IN_GATES + 2 * MLSTM_HEADS
    wmix_ref[:, WIN_GA:WIN_GATES] = w_ref[:, tail:].astype(_BF16)
    lane = lax.broadcasted_iota(jnp.int32, (rb, GATE_PAD), 1)
    gates = jnp.where(lane < 2 * MLSTM_HEADS, w_ref[:, W_IN_GATES:W_IN_GATES + GATE_PAD], 0.0)
    wmix_ref[:, WIN_GATES:] = gates.astype(_BF16)


def _regroup_w_in(w_in):
    depth = w_in.shape[0]
    assert w_in.shape[1:] == (D_MODEL, W_IN_COLS)
    return pl.pallas_call(
        _regroup_w_in_kernel,
        grid=(depth, D_MODEL // REGROUP_ROWS),
        in_specs=[pl.BlockSpec((None, REGROUP_ROWS, W_IN_COLS), lambda l, r: (l, r, 0))],
        out_specs=[pl.BlockSpec((None, REGROUP_ROWS, SSM_WIDTH), lambda l, r: (l, r, 0)),
                   pl.BlockSpec((None, REGROUP_ROWS, WIN_COLS), lambda l, r: (l, r, 0))],
        out_shape=[jax.ShapeDtypeStruct((depth, D_MODEL, SSM_WIDTH), _BF16),
                   jax.ShapeDtypeStruct((depth, D_MODEL, WIN_COLS), _BF16)],
        compiler_params=pltpu.CompilerParams(dimension_semantics=("parallel", "parallel"),
                                             vmem_limit_bytes=VMEM_LIMIT),
        name="regroup_w_in",
    )(w_in)


def _s5_prep(lam_re, lam_im, b_re, b_im, c_re, c_im, d, log_dt):
    hp = lax.Precision.HIGHEST
    gh, npow = SSM_GROUP, SS + 1
    lr = lam_re.reshape(PAIRS, PAIR_STATE)
    li = lam_im.reshape(PAIRS, PAIR_STATE)
    dt = jnp.repeat(jnp.exp(log_dt), SSM_STATE).reshape(PAIRS, PAIR_STATE)
    ks = jnp.arange(npow, dtype=_F32)[None, :, None]
    mag = jnp.exp((lr * dt)[:, None, :] * ks)
    ang = (li * dt)[:, None, :] * ks
    ak_re, ak_im = mag * jnp.cos(ang), mag * jnp.sin(ang)
    nr, ni = ak_re[:, 1] - 1.0, ak_im[:, 1]
    den = lr * lr + li * li
    z_re = (nr * lr + ni * li) / den
    z_im = (ni * lr - nr * li) / den
    bt_re = jnp.swapaxes(b_re.reshape(PAIRS, PAIR_STATE, gh), 1, 2)
    bt_im = jnp.swapaxes(b_im.reshape(PAIRS, PAIR_STATE, gh), 1, 2)
    bb_re = z_re[:, None, :] * bt_re - z_im[:, None, :] * bt_im
    bb_im = z_re[:, None, :] * bt_im + z_im[:, None, :] * bt_re
    as_hn = lambda c: c.reshape(PAIRS, 2, gh, SSM_STATE).transpose(0, 2, 1, 3).reshape(PAIRS, gh, PAIR_STATE)
    ct_re, ct_im = as_hn(c_re), as_hn(c_im)
    same_group = (jnp.arange(2)[:, None] == (jnp.arange(PAIR_STATE) // SSM_STATE)[None, :]).astype(_F32)
    same_group = same_group[None, None, :, None, :]

    def times_powers(m_re, m_im, pw_re, pw_im):
        pr, pi = pw_re[:, :, None, None, :], pw_im[:, :, None, None, :]
        mr, mi = m_re[:, None, None, :, :], m_im[:, None, None, :, :]
        shape = (PAIRS, npow * PAIR_IN, PAIR_STATE)
        return (((pr * mr - pi * mi) * same_group).reshape(shape),
                ((pr * mi + pi * mr) * same_group).reshape(shape))

    wk_re, wk_im = times_powers(bb_re, bb_im, ak_re[:, ::-1], ak_im[:, ::-1])
    bs = jnp.concatenate([wk_re[:, PAIR_IN:], wk_im[:, PAIR_IN:]], axis=-1)

    ca_re, ca_im = times_powers(ct_re, ct_im, ak_re, ak_im)
    cs = jnp.concatenate([ca_re[:, PAIR_IN:], -ca_im[:, PAIR_IN:]], axis=-1)

    bbm_re, bbm_im = wk_re[:, SS * PAIR_IN:], wk_im[:, SS * PAIR_IN:]
    kt = (jnp.einsum('qrn,qcn->qrc', bbm_re, ca_re[:, :PAIR_W], precision=hp)
          - jnp.einsum('qrn,qcn->qrc', bbm_im, ca_im[:, :PAIR_W], precision=hp))
    ds = jnp.stack([jnp.pad(kt[..., :PAIR_W - PAIR_IN * i], ((0, 0), (0, 0), (PAIR_IN * i, 0)))
                    for i in range(SS)], axis=1).reshape(PAIRS, PAIR_W, PAIR_W)
    skip = jnp.tile(d.reshape(PAIRS, PAIR_IN), (1, SS))
    ds = ds + skip[:, None, :] * jnp.eye(PAIR_W, dtype=_F32)[None]

    ks8 = SS * jnp.arange(SUBLANES + 1, dtype=_F32)
    mag8 = jnp.exp((lr * dt).reshape(HALF)[None, :] * ks8[:, None])
    ang8 = (li * dt).reshape(HALF)[None, :] * ks8[:, None]
    apow = jnp.concatenate([mag8 * jnp.cos(ang8), mag8 * jnp.sin(ang8)], axis=1)
    sub = jnp.arange(SUBLANES)
    shifted = [jnp.where((sub >= k)[:, None], apow[k][None, :], 0.0) for k in (1, 2, 4)]
    a8 = jnp.stack(shifted + [apow[:SUBLANES], jnp.broadcast_to(apow[SUBLANES][None], (SUBLANES, 2 * HALF))])
    return bs.astype(_BF16), cs.astype(_BF16), ds.astype(_BF16), a8


def kernel(x, meta_tokens, g_mix_pre, g_mix_post, w_in, b_gates, ssm_lambda_re, ssm_lambda_im, ssm_b_re,
           ssm_b_im, ssm_c_re, ssm_c_im, ssm_d, ssm_log_dt, w_ssm_glu, w_qk_conv, b_qk_conv, g_head_norm,
           w_branch_ssm, w_branch_mlstm, w_out, g_ffn_pre, g_ffn_post, w_ffn_gate, w_ffn_up, w_ffn_conv,
           b_ffn_conv, w_ffn_down):
    bsz, seq, _ = x.shape
    assert bsz == 1
    depth = w_in.shape[0]
    assert seq % TL == 0 and FRONT_PAD + N_META == TL
    lp = TL + seq
    assert (lp // SS) % SUBLANES == 0

    head = jnp.concatenate([jnp.zeros((FRONT_PAD, D_MODEL), _F32), meta_tokens.astype(_F32)], axis=0)
    stream = (head, x[0])

    nh2 = 2 * MLSTM_HEADS
    c0, w = SSM_WIDTH, MLSTM_WIDTH
    g0 = c0 + 4 * w
    row = lambda a: a[:, None, :]
    w_u, w_mix = _regroup_w_in(w_in)
    w_gr = jnp.swapaxes(w_in[:, :, g0:g0 + nh2], 1, 2).astype(_BF16)
    b_gc = row(jnp.pad(b_gates, ((0, 0), (0, GATE_PAD - nh2))))
    b_gr = jnp.broadcast_to(b_gates[:, :, None], (depth, nh2, LANES))
    w_cv = jnp.concatenate([w_qk_conv, row(b_qk_conv),
                            jnp.zeros((depth, SUBLANES - QK_CONV - 1, 2 * w), _F32)], axis=1)
    mixer_consts = (row(g_mix_pre), row(g_mix_post), w_mix, w_gr, b_gc, b_gr, w_cv,
                    row(g_head_norm), w_ssm_glu.astype(_BF16), w_branch_ssm.astype(_BF16),
                    w_branch_mlstm.astype(_BF16), w_out.astype(_BF16))
    f_cv = jnp.concatenate([w_ffn_conv, row(b_ffn_conv),
                            jnp.zeros((depth, SUBLANES - FFN_CONV - 1, FFN_DIM), _F32)], axis=1)
    ffn_consts = (row(g_ffn_pre), row(g_ffn_post), w_ffn_gate.astype(_BF16), w_ffn_up.astype(_BF16),
                  w_ffn_down.astype(_BF16), f_cv)
    bs, cs, ds, a8 = jax.vmap(_s5_prep)(ssm_lambda_re, ssm_lambda_im, ssm_b_re, ssm_b_im,
                                        ssm_c_re, ssm_c_im, ssm_d, ssm_log_dt)
    g_pre = row(g_mix_pre)

    u = _s5_in(stream, g_pre, w_u, 0, lp)
    for l in range(depth):
        y5 = _s5_core(u, bs, cs, ds, a8, l)
        h = _mixer(stream, y5, mixer_consts, l, lp)
        if l + 1 < depth:
            h, u = _ffn(h, ffn_consts, l, next_s5=(g_pre, w_u))
        else:
            h = _ffn(h, ffn_consts, l)
        stream = (h,)
    return h[None]
```

```python
import functools
import math

import jax
import jax.numpy as jnp
from jax import lax
from jax.experimental import pallas as pl
from jax.experimental.pallas import tpu as pltpu

D_MODEL = 1024
N_META = 16
SSM_WIDTH = 512
SSM_GROUP = 16
SSM_GROUPS = 32
SSM_STATE = 64
MLSTM_WIDTH = 512
MLSTM_HEADS = 4
MLSTM_HEAD_DIM = 128
MLSTM_REF_CHUNK = 64
QK_CONV = 4
FFN_DIM = 2816
FFN_CONV = 3
NORM_EPS = 1e-6
PAD_LOG_INPUT_GATE = -1e4

LANES = 128
SUBLANES = 8
TL = 512
CH = 128
SS = 8
PAIRS = SSM_GROUPS // 2
PAIR_IN = 2 * SSM_GROUP
PAIR_W = SS * PAIR_IN
PAIR_STATE = 2 * SSM_STATE
PAIRS_PER_TILE = LANES // PAIR_IN
HALF = PAIRS * PAIR_STATE
FRONT_PAD = TL - N_META
MIX_BLOCK = 256
MLSTM_STAGE_CHUNKS = 1
S5_ROW_BLOCK = 96
FFN_CHUNK = 256
N_FFN_CHUNKS = FFN_DIM // FFN_CHUNK
FFN_DOWN_GROUP = 4
GATE_PAD = LANES
WIN_V = 2 * MLSTM_WIDTH
WIN_O = WIN_V + MLSTM_WIDTH
WIN_GA = WIN_O + MLSTM_WIDTH
WIN_GB = WIN_GA + D_MODEL
WIN_GATES = WIN_GB + D_MODEL
WIN_COLS = WIN_GATES + GATE_PAD
W_IN_COLS = SSM_WIDTH + 4 * MLSTM_WIDTH + 2 * MLSTM_HEADS + 2 * D_MODEL
W_IN_GATES = SSM_WIDTH + 4 * MLSTM_WIDTH
REGROUP_ROWS = 256
VMEM_LIMIT = 56 * 1024 * 1024

_BF16 = jnp.bfloat16
_F32 = jnp.float32


def _rms(x, g):
    return x * lax.rsqrt(jnp.mean(x * x, axis=-1, keepdims=True) + NORM_EPS) * g


def _gelu_tanh(x):
    return 0.5 * x * (1.0 + jnp.tanh(math.sqrt(2.0 / math.pi) * (x + 0.044715 * (x * x * x))))


def _sigmoid(x):
    return 0.5 + 0.5 * jnp.tanh(0.5 * x)


def _log_sigmoid(x):
    return jnp.minimum(x, 0.0) - jnp.log(1.0 + jnp.exp(-jnp.abs(x)))


def _dot(a, b):
    return jnp.dot(a, b, preferred_element_type=_F32)


def _layer_spec(shape, l):
    nd = len(shape) - 1
    return pl.BlockSpec((None,) + tuple(shape[1:]), lambda i, _l=l, _nd=nd: (_l,) + (0,) * _nd)


def _stream_specs(first):
    if first:
        return [pl.BlockSpec((TL, D_MODEL), lambda i: (0, 0)),
                pl.BlockSpec((TL, D_MODEL), lambda i: (jnp.maximum(i - 1, 0), 0))]
    return [pl.BlockSpec((TL, D_MODEL), lambda i: (i, 0))]


def _stream_tile(first, refs):
    if first:
        head_ref, x_ref, *rest = refs
        return jnp.where(pl.program_id(0) == 0, head_ref[...], x_ref[...]), rest
    x_ref, *rest = refs
    return x_ref[...], rest


def _s5_in_kernel(first, *refs):
    x, (g_ref, w_ref, u_ref, us_ref) = _stream_tile(first, refs)
    _s5_project(x, g_ref, w_ref, u_ref, us_ref)


def _s5_project(x, g_ref, w_ref, u_ref, us_ref):
    hn = _rms(x, g_ref[...]).astype(_BF16)
    u = _dot(hn, w_ref[...])
    for t in range(SSM_WIDTH // LANES):
        us_ref[t] = u[:, t * LANES:(t + 1) * LANES]
    for j in range(SS):
        for t in range(SSM_WIDTH // LANES):
            rows = us_ref[t, pl.ds(j, TL // SS, stride=SS), :].astype(_BF16)
            for k in range(PAIRS_PER_TILE):
                u_ref[t * PAIRS_PER_TILE + k, :, j * PAIR_IN:(j + 1) * PAIR_IN] = (
                    rows[:, k * PAIR_IN:(k + 1) * PAIR_IN])


def _s5_in(stream, g, w, l, lp):
    first = len(stream) == 2
    return pl.pallas_call(
        functools.partial(_s5_in_kernel, first),
        grid=(lp // TL,),
        in_specs=_stream_specs(first) + [_layer_spec(g.shape, l), _layer_spec(w.shape, l)],
        out_specs=pl.BlockSpec((PAIRS, TL // SS, PAIR_W), lambda i: (0, i, 0)),
        out_shape=jax.ShapeDtypeStruct((PAIRS, lp // SS, PAIR_W), _BF16),
        scratch_shapes=[pltpu.VMEM((SSM_WIDTH // LANES, TL, LANES), _F32)],
        compiler_params=pltpu.CompilerParams(dimension_semantics=("parallel",),
                                             vmem_limit_bytes=VMEM_LIMIT),
        name="s5_in",
    )(*stream, g, w)


def _s5_core_kernel(rt, u_ref, bs_ref, cs_ref, ds_ref, a_ref, y_ref, vx_ref, st_ref):
    @pl.when(pl.program_id(0) == 0)
    def _():
        st_ref[...] = jnp.zeros_like(st_ref)

    bounds = list(range(0, rt, S5_ROW_BLOCK)) + [rt]
    blocks = [(bounds[k], bounds[k + 1]) for k in range(len(bounds) - 1)]

    def state_input(r0, r1):
        g0, g1 = r0 // SUBLANES, r1 // SUBLANES
        for p in range(PAIRS):
            v = _dot(u_ref[p, r0:r1, :], bs_ref[p])
            vx_ref[g0:g1, :, p * PAIR_STATE:(p + 1) * PAIR_STATE] = (
                v[:, :PAIR_STATE].reshape(g1 - g0, SUBLANES, PAIR_STATE))
            vx_ref[g0:g1, :, HALF + p * PAIR_STATE:HALF + (p + 1) * PAIR_STATE] = (
                v[:, PAIR_STATE:].reshape(g1 - g0, SUBLANES, PAIR_STATE))

    def output(r0, r1):
        g0, g1 = r0 // SUBLANES, r1 // SUBLANES
        for p in range(PAIRS):
            x_re = vx_ref[g0:g1, :, p * PAIR_STATE:(p + 1) * PAIR_STATE].reshape(r1 - r0, PAIR_STATE)
            x_im = vx_ref[g0:g1, :, HALF + p * PAIR_STATE:HALF + (p + 1) * PAIR_STATE].reshape(r1 - r0, PAIR_STATE)
            xp = jnp.concatenate([x_re, x_im], axis=-1).astype(_BF16)
            y_ref[p, r0:r1, :] = (
                lax.dot_general(xp, cs_ref[p], (((1,), (1,)), ((), ())), preferred_element_type=_F32)
                + _dot(u_ref[p, r0:r1, :], ds_ref[p]))

    def cmul(t, z_re, z_im):
        t_re, t_im = a_ref[t, :, :HALF], a_ref[t, :, HALF:]
        return t_re * z_re - t_im * z_im, t_re * z_im + t_im * z_re

    first = lax.broadcasted_iota(jnp.int32, (SUBLANES, HALF), 0) == 0

    def body(i, carry):
        s_re, s_im = carry
        w_re = vx_ref[i, :, :HALF]
        w_im = vx_ref[i, :, HALF:]
        for t, shift in enumerate((1, 2, 4)):
            d_re, d_im = cmul(t, pltpu.roll(w_re, shift, 0), pltpu.roll(w_im, shift, 0))
            w_re, w_im = w_re + d_re, w_im + d_im
        x_re, x_im = cmul(3, s_re, s_im)
        vx_ref[i, :, :HALF] = x_re + jnp.where(first, 0.0, pltpu.roll(w_re, 1, 0))
        vx_ref[i, :, HALF:] = x_im + jnp.where(first, 0.0, pltpu.roll(w_im, 1, 0))
        n_re, n_im = cmul(4, s_re, s_im)
        last = slice(SUBLANES - 1, SUBLANES)
        return (n_re + jnp.broadcast_to(w_re[last], (SUBLANES, HALF)),
                n_im + jnp.broadcast_to(w_im[last], (SUBLANES, HALF)))

    carry = (st_ref[:, :HALF], st_ref[:, HALF:])
    state_input(*blocks[0])
    for k, (r0, r1) in enumerate(blocks):
        if k + 1 < len(blocks):
            state_input(*blocks[k + 1])
        for i in range(r0 // SUBLANES, r1 // SUBLANES):
            carry = body(i, carry)
        output(r0, r1)
    st_ref[:, :HALF] = carry[0]
    st_ref[:, HALF:] = carry[1]


def _s5_core(u, bs, cs, ds, a8, l):
    rows = u.shape[1]
    rt = rows // 8 if (rows // 8) % SUBLANES == 0 else rows
    return pl.pallas_call(
        functools.partial(_s5_core_kernel, rt),
        grid=(rows // rt,),
        in_specs=[
            pl.BlockSpec((PAIRS, rt, PAIR_W), lambda i: (0, i, 0)),
            _layer_spec(bs.shape, l), _layer_spec(cs.shape, l), _layer_spec(ds.shape, l),
            _layer_spec(a8.shape, l),
        ],
        out_specs=pl.BlockSpec((PAIRS, rt, PAIR_W), lambda i: (0, i, 0)),
        out_shape=jax.ShapeDtypeStruct((PAIRS, rows, PAIR_W), _F32),
        scratch_shapes=[
            pltpu.VMEM((rt // SUBLANES, SUBLANES, 2 * HALF), _F32),
            pltpu.VMEM((SUBLANES, 2 * HALF), _F32),
        ],
        compiler_params=pltpu.CompilerParams(dimension_semantics=("arbitrary",),
                                             vmem_limit_bytes=VMEM_LIMIT),
        name="s5_core",
    )(u, bs, cs, ds, a8)


def _mixer_kernel(first, *refs):
    x, rest = _stream_tile(first, refs)
    (y5_ref, gpre_ref, gpost_ref, win_ref, wgr_ref, bgc_ref, bgr_ref, wcv_ref,
     ghead_ref, wglu_ref, wa_ref, wb_ref,
     wout_ref, o_ref, st_ref, m_ref, yb_ref, y5s_ref, q_ref, k_ref, v_ref,
     so_ref, sga_ref, sgb_ref, *cv_refs) = rest
    pid = pl.program_id(0)

    @pl.when(pid == 0)
    def _():
        for cv_ref in cv_refs:
            cv_ref[0:SUBLANES, :] = jnp.zeros((SUBLANES, MIX_BLOCK), _F32)
        st_ref[...] = jnp.zeros_like(st_ref)
        m_ref[...] = jnp.zeros_like(m_ref)
        for h in range(MLSTM_HEADS):
            v_ref[h, :, MLSTM_HEAD_DIM:] = jnp.ones((TL, MLSTM_HEAD_DIM), _BF16)

    hn = _rms(x, gpre_ref[...]).astype(_BF16)

    row_c = pid * TL + lax.broadcasted_iota(jnp.int32, (TL, 1), 0)
    valid_c = row_c >= FRONT_PAD
    row_r = pid * TL + lax.broadcasted_iota(jnp.int32, (1, TL), 1)
    valid_r = row_r >= FRONT_PAD

    n_blocks = 2 * MLSTM_WIDTH // MIX_BLOCK

    def proj_block(b):
        cols = slice(b * MIX_BLOCK, (b + 1) * MIX_BLOCK)
        cv_refs[b][SUBLANES:SUBLANES + TL, :] = _dot(hn, win_ref[:, cols])

    def conv_block(b):
        cols = slice(b * MIX_BLOCK, (b + 1) * MIX_BLOCK)
        cv_ref = cv_refs[b]
        acc = wcv_ref[QK_CONV:QK_CONV + 1, cols]
        for j in range(QK_CONV):
            off = SUBLANES - (QK_CONV - 1) + j
            acc = acc + wcv_ref[j:j + 1, cols] * cv_ref[off:off + TL, :]
        cv_ref[0:SUBLANES, :] = cv_ref[TL:TL + SUBLANES, :]
        qk = jnp.where(valid_c, acc * _sigmoid(acc), 0.0)
        if b < MLSTM_WIDTH // MIX_BLOCK:
            q_ref[:, cols] = qk.astype(_BF16)
        else:
            kcols = slice(b * MIX_BLOCK - MLSTM_WIDTH, (b + 1) * MIX_BLOCK - MLSTM_WIDTH)
            k_ref[:, kcols] = qk * (MLSTM_HEAD_DIM ** -0.5)

    gate_w = D_MODEL // n_blocks

    def branch_gate_block(dst_ref, base, b):
        dst_ref[:, b * gate_w:(b + 1) * gate_w] = _sigmoid(
            _dot(hn, win_ref[:, base + b * gate_w:base + (b + 1) * gate_w]))

    q_blocks = MLSTM_WIDTH // MIX_BLOCK
    for b in range(q_blocks, n_blocks):
        proj_block(b)
    v_all = jnp.where(valid_c, _dot(hn, win_ref[:, WIN_V:WIN_O]), 0.0).astype(_BF16)
    for h in range(MLSTM_HEADS):
        v_ref[h, :, :MLSTM_HEAD_DIM] = v_all[:, h * MLSTM_HEAD_DIM:(h + 1) * MLSTM_HEAD_DIM]
    lane = lax.broadcasted_iota(jnp.int32, (1, GATE_PAD), 1)
    gc = _dot(hn, win_ref[:, WIN_GATES:]) + bgc_ref[...]
    gc = jnp.where(lane < MLSTM_HEADS, gc, _log_sigmoid(gc))
    gc = jnp.where(valid_c, gc, jnp.where(lane < MLSTM_HEADS, PAD_LOG_INPUT_GATE, 0.0))
    sub = lax.broadcasted_iota(jnp.int32, (2 * MLSTM_HEADS, 1), 0)
    gr = lax.dot_general(wgr_ref[...], hn, (((1,), (1,)), ((), ())),
                         preferred_element_type=_F32) + bgr_ref[:, 0:1]
    gr = jnp.where(sub < MLSTM_HEADS, gr, _log_sigmoid(gr))
    gr = jnp.where(valid_r, gr, jnp.where(sub < MLSTM_HEADS, PAD_LOG_INPUT_GATE, 0.0))
    for b in range(q_blocks):
        conv_block(q_blocks + b)
        proj_block(b)

    it = lax.broadcasted_iota(jnp.int32, (CH, CH), 0)
    js = lax.broadcasted_iota(jnp.int32, (CH, CH), 1)
    causal = it >= js
    tri_l = causal.astype(_F32)
    tri_u = (it <= js).astype(_F32)

    def regroup_s5(j):
        for t in range(SSM_WIDTH // LANES):
            yj = jnp.concatenate([y5_ref[t * PAIRS_PER_TILE + k, :, j * PAIR_IN:(j + 1) * PAIR_IN]
                                  for k in range(PAIRS_PER_TILE)], axis=-1)
            y5s_ref[t, pl.ds(j, TL // SS, stride=SS), :] = yj

    n_chunks = TL // CH
    hp = lax.Precision.HIGHEST
    rows = [slice(c * CH, (c + 1) * CH) for c in range(n_chunks)]
    heads = [slice(h * MLSTM_HEAD_DIM, (h + 1) * MLSTM_HEAD_DIM) for h in range(MLSTM_HEADS)]

    gcc = [gc[rows[c], :] for c in range(n_chunks)]
    b_c = [jnp.dot(tri_l, gcc[c], preferred_element_type=_F32, precision=hp) for c in range(n_chunks)]
    b_r = [jnp.dot(gr[:, rows[c]], tri_u, preferred_element_type=_F32, precision=hp) for c in range(n_chunks)]

    rrow = [[None] * MLSTM_HEADS for _ in range(n_chunks)]
    m_in = [[None] * MLSTM_HEADS for _ in range(n_chunks)]
    m_out = [[None] * MLSTM_HEADS for _ in range(n_chunks)]
    for h in range(MLSTM_HEADS):
        fl = MLSTM_HEADS + h
        m = m_ref[h:h + 1, :]
        for c in range(n_chunks):
            rrow[c][h] = gr[h:h + 1, rows[c]] - b_r[c][fl:fl + 1, :]
            m_in[c][h] = m
            m_out[c][h] = jnp.maximum(m, jnp.max(rrow[c][h], axis=-1, keepdims=True))
            m = b_r[c][fl:fl + 1, CH - 1:CH] + m_out[c][h]
        m_ref[h:h + 1, :] = m

    st_in = [[None] * MLSTM_HEADS for _ in range(n_chunks)]
    for h in range(MLSTM_HEADS):
        fl = MLSTM_HEADS + h
        st = st_ref[h]
        for c in range(n_chunks):
            st_in[c][h] = st
            rcol = gcc[c][:, h:h + 1] - b_c[c][:, fl:fl + 1]
            kw = k_ref[rows[c], heads[h]] * jnp.exp(rcol - m_out[c][h][:, 0:1])
            decay = jnp.exp(m_in[c][h] - m_out[c][h])
            st = (jnp.concatenate([decay, decay], axis=-1) * st
                  + _dot(kw.T.astype(_BF16), v_ref[h, rows[c], :]))
        st_ref[h] = st

    for b in range(q_blocks):
        conv_block(b)
    so_ref[...] = _sigmoid(_dot(hn, win_ref[:, WIN_O:WIN_GA]))

    for c0 in range(0, n_chunks, MLSTM_STAGE_CHUNKS):
        group = range(c0, c0 + MLSTM_STAGE_CHUNKS)
        for c in group:
            branch_gate_block(sga_ref, WIN_GA, c)
            branch_gate_block(sgb_ref, WIN_GB, c)
            for j in range(c * SS // n_chunks, (c + 1) * SS // n_chunks):
                regroup_s5(j)
        items = [(c, h) for c in group for h in range(MLSTM_HEADS)]
        qc = [q_ref[rows[c], heads[h]] for c, h in items]
        qk = [lax.dot_general(qc[i], k_ref[rows[c], heads[h]].astype(_BF16), (((1,), (1,)), ((), ())),
                              preferred_element_type=_F32) for i, (c, h) in enumerate(items)]
        qs = [_dot(qc[i], st_in[c][h].astype(_BF16)) for i, (c, h) in enumerate(items)]
        mc1, mcol, s = [], [], []
        for i, (c, h) in enumerate(items):
            rmat = jnp.where(causal, rrow[c][h], -jnp.inf)
            mc1.append(jnp.maximum(m_in[c][h][:, 0:1], jnp.max(rmat, axis=-1, keepdims=True)))
            mcol.append(jnp.broadcast_to(mc1[i], (CH, LANES)))
            s.append((qk[i] * jnp.exp(rmat - mcol[i])).astype(_BF16))
        sv = [_dot(s[i], v_ref[h, rows[c], :]) for i, (c, h) in enumerate(items)]
        for i, (c, h) in enumerate(items):
            hs = heads[h]
            bcol = b_c[c][:, MLSTM_HEADS + h:MLSTM_HEADS + h + 1]
            floor = jnp.broadcast_to(jnp.exp(-(bcol + mc1[i])), (CH, LANES))
            w_inter = jnp.exp(m_in[c][h] - mcol[i])
            num = sv[i][:, :MLSTM_HEAD_DIM] + w_inter * qs[i][:, :MLSTM_HEAD_DIM]
            den = sv[i][:, MLSTM_HEAD_DIM:] + w_inter * qs[i][:, MLSTM_HEAD_DIM:]
            hout = num / jnp.maximum(jnp.abs(den), floor)
            hout = hout * lax.rsqrt(jnp.mean(hout * hout, axis=-1, keepdims=True) + NORM_EPS)
            hout = hout * ghead_ref[:, hs] * so_ref[rows[c], hs]
            yb_ref[rows[c], hs] = hout.astype(_BF16)

    ya = _gelu_tanh(jnp.concatenate([y5s_ref[t] for t in range(SSM_WIDTH // LANES)], axis=-1))
    ya = ya * _sigmoid(_dot(ya.astype(_BF16), wglu_ref[...]))

    merged = (sga_ref[...] * _dot(ya.astype(_BF16), wa_ref[...])
              + sgb_ref[...] * _dot(yb_ref[...], wb_ref[...]))
    o_ref[...] = x + _rms(_dot(merged.astype(_BF16), wout_ref[...]), gpost_ref[...])


def _mixer(stream, y5, consts, l, lp):
    first = len(stream) == 2
    return pl.pallas_call(
        functools.partial(_mixer_kernel, first),
        grid=(lp // TL,),
        in_specs=_stream_specs(first) + [
            pl.BlockSpec((PAIRS, TL // SS, PAIR_W), lambda i: (0, i, 0)),
        ] + [_layer_spec(c.shape, l) for c in consts],
        out_specs=pl.BlockSpec((TL, D_MODEL), lambda i: (i, 0)),
        out_shape=jax.ShapeDtypeStruct((lp, D_MODEL), _F32),
        scratch_shapes=[
            pltpu.VMEM((MLSTM_HEADS, MLSTM_HEAD_DIM, 2 * MLSTM_HEAD_DIM), _F32),
            pltpu.VMEM((SUBLANES, LANES), _F32),
            pltpu.VMEM((TL, MLSTM_WIDTH), _BF16),
            pltpu.VMEM((SSM_WIDTH // LANES, TL, LANES), _F32),
            pltpu.VMEM((TL, MLSTM_WIDTH), _BF16),
            pltpu.VMEM((TL, MLSTM_WIDTH), _F32),
            pltpu.VMEM((MLSTM_HEADS, TL, 2 * MLSTM_HEAD_DIM), _BF16),
            pltpu.VMEM((TL, MLSTM_WIDTH), _F32),
            pltpu.VMEM((TL, D_MODEL), _F32),
            pltpu.VMEM((TL, D_MODEL), _F32),
        ] + [pltpu.VMEM((TL + SUBLANES, MIX_BLOCK), _F32)
             for _ in range(2 * MLSTM_WIDTH // MIX_BLOCK)],
        compiler_params=pltpu.CompilerParams(dimension_semantics=("arbitrary",),
                                             vmem_limit_bytes=VMEM_LIMIT),
        name="mixer",
    )(*stream, y5, *consts)


def _ffn_kernel(fuse_next, n_tiles, *refs):
    if fuse_next:
        (x_ref, gpre_ref, gpost_ref, wg_ref, wu_ref, wd_ref, wcv_ref, gnext_ref, wnext_ref,
         o_ref, u_ref, cv_ref, act_ref, prev_ref, us_ref) = refs
    else:
        x_ref, gpre_ref, gpost_ref, wg_ref, wu_ref, wd_ref, wcv_ref, o_ref, cv_ref, act_ref = refs
    pid = pl.program_id(0)

    @pl.when(pid == 0)
    def _():
        cv_ref[0:SUBLANES, :] = jnp.zeros((SUBLANES, FFN_DIM), _F32)
        if fuse_next:
            prev_ref[...] = jnp.zeros_like(prev_ref)

    def tile():
        if fuse_next:
            _s5_project(prev_ref[...], gnext_ref, wnext_ref, u_ref, us_ref)
        out = _ffn_tile(x_ref[...], gpre_ref, gpost_ref, wg_ref, wu_ref, wd_ref, wcv_ref, cv_ref, act_ref)
        o_ref[...] = out
        if fuse_next:
            prev_ref[...] = out

    if fuse_next:
        pl.when(pid < n_tiles)(tile)

        @pl.when(pid == n_tiles)
        def _():
            _s5_project(prev_ref[...], gnext_ref, wnext_ref, u_ref, us_ref)
    else:
        tile()


def _ffn_tile(x, gpre_ref, gpost_ref, wg_ref, wu_ref, wd_ref, wcv_ref, cv_ref, act_ref):
    hn = _rms(x, gpre_ref[...]).astype(_BF16)
    chunk = lambda c: slice(c * FFN_CHUNK, (c + 1) * FFN_CHUNK)
    acc = None
    group_start = 0
    cv_ref[SUBLANES:SUBLANES + TL, chunk(0)] = _dot(hn, wg_ref[:, chunk(0)])
    up_next = _dot(hn, wu_ref[:, chunk(0)])
    for c in range(N_FFN_CHUNKS):
        cs = chunk(c)
        up = up_next
        if c + 1 < N_FFN_CHUNKS:
            cv_ref[SUBLANES:SUBLANES + TL, chunk(c + 1)] = _dot(hn, wg_ref[:, chunk(c + 1)])
            up_next = _dot(hn, wu_ref[:, chunk(c + 1)])
        conv = wcv_ref[FFN_CONV:FFN_CONV + 1, cs]
        for j in range(FFN_CONV):
            off = SUBLANES - (FFN_CONV - 1) + j
            conv = conv + wcv_ref[j:j + 1, cs] * cv_ref[off:off + TL, cs]
        cv_ref[0:SUBLANES, cs] = cv_ref[TL:TL + SUBLANES, cs]
        act_ref[:, cs] = (_gelu_tanh(conv) * up).astype(_BF16)
        if (c + 1) % FFN_DOWN_GROUP == 0 or c + 1 == N_FFN_CHUNKS:
            gs = slice(group_start * FFN_CHUNK, (c + 1) * FFN_CHUNK)
            part = _dot(act_ref[:, gs], wd_ref[gs, :])
            acc = part if acc is None else acc + part
            group_start = c + 1
    return x + _rms(acc, gpost_ref[...])


def _ffn(x, consts, l, next_s5=None):
    lp = x.shape[0]
    n_tiles = lp // TL
    scratch = [pltpu.VMEM((TL + SUBLANES, FFN_DIM), _F32),
               pltpu.VMEM((TL, FFN_DIM), _BF16)]
    in_specs = [_layer_spec(c.shape, l) for c in consts]
    if next_s5 is None:
        grid = n_tiles
        x_spec = pl.BlockSpec((TL, D_MODEL), lambda i: (i, 0))
        out_specs = pl.BlockSpec((TL, D_MODEL), lambda i: (jnp.maximum(i - 1, 0), 0))
        out_shape = jax.ShapeDtypeStruct((lp - TL, D_MODEL), _F32)
        args = (x, *consts)
    else:
        grid = n_tiles + 1
        x_spec = pl.BlockSpec((TL, D_MODEL), lambda i: (jnp.minimum(i, n_tiles - 1), 0))
        out_specs = [pl.BlockSpec((TL, D_MODEL), lambda i: (jnp.minimum(i, n_tiles - 1), 0)),
                     pl.BlockSpec((PAIRS, TL // SS, PAIR_W), lambda i: (0, jnp.maximum(i - 1, 0), 0))]
        out_shape = [jax.ShapeDtypeStruct((lp, D_MODEL), _F32),
                     jax.ShapeDtypeStruct((PAIRS, lp // SS, PAIR_W), _BF16)]
        in_specs += [_layer_spec(a.shape, l + 1) for a in next_s5]
        scratch += [pltpu.VMEM((TL, D_MODEL), _F32),
                    pltpu.VMEM((SSM_WIDTH // LANES, TL, LANES), _F32)]
        args = (x, *consts, *next_s5)
    return pl.pallas_call(
        functools.partial(_ffn_kernel, next_s5 is not None, n_tiles),
        grid=(grid,),
        in_specs=[x_spec] + in_specs,
        out_specs=out_specs,
        out_shape=out_shape,
        scratch_shapes=scratch,
        compiler_params=pltpu.CompilerParams(dimension_semantics=("arbitrary",),
                                             vmem_limit_bytes=VMEM_LIMIT),
        name="ffn",
    )(*args)


def _regroup_w_in_kernel(wt_ref, wu_ref, wmix_ref, wgr_ref):
    piece = SSM_WIDTH

    def rows_to_cols(src):
        return wt_ref[src:src + piece, :].T.astype(_BF16)

    wu_ref[...] = rows_to_cols(0)
    tail = W_IN_GATES + 2 * MLSTM_HEADS
    for k in range(WIN_GA // piece):
        wmix_ref[:, k * piece:(k + 1) * piece] = rows_to_cols(SSM_WIDTH + k * piece)
    for k in range(2 * D_MODEL // piece):
        wmix_ref[:, WIN_GA + k * piece:WIN_GA + (k + 1) * piece] = rows_to_cols(tail + k * piece)
    rb = wt_ref.shape[1]
    lane = lax.broadcasted_iota(jnp.int32, (rb, GATE_PAD), 1)
    gates = wt_ref[W_IN_GATES:W_IN_GATES + GATE_PAD, :].T
    wmix_ref[:, WIN_GATES:] = jnp.where(lane < 2 * MLSTM_HEADS, gates, 0.0).astype(_BF16)
    wgr_ref[...] = wt_ref[W_IN_GATES:W_IN_GATES + 2 * MLSTM_HEADS, :].astype(_BF16)


def _regroup_w_in(w_in):
    depth = w_in.shape[0]
    assert w_in.shape[1:] == (D_MODEL, W_IN_COLS)
    return pl.pallas_call(
        _regroup_w_in_kernel,
        grid=(depth, D_MODEL // REGROUP_ROWS),
        in_specs=[pl.BlockSpec((None, W_IN_COLS, REGROUP_ROWS), lambda l, r: (l, 0, r))],
        out_specs=[pl.BlockSpec((None, REGROUP_ROWS, SSM_WIDTH), lambda l, r: (l, r, 0)),
                   pl.BlockSpec((None, REGROUP_ROWS, WIN_COLS), lambda l, r: (l, r, 0)),
                   pl.BlockSpec((None, 2 * MLSTM_HEADS, REGROUP_ROWS), lambda l, r: (l, 0, r))],
        out_shape=[jax.ShapeDtypeStruct((depth, D_MODEL, SSM_WIDTH), _BF16),
                   jax.ShapeDtypeStruct((depth, D_MODEL, WIN_COLS), _BF16),
                   jax.ShapeDtypeStruct((depth, 2 * MLSTM_HEADS, D_MODEL), _BF16)],
        compiler_params=pltpu.CompilerParams(dimension_semantics=("parallel", "parallel"),
                                             vmem_limit_bytes=VMEM_LIMIT),
        name="regroup_w_in",
    )(jnp.swapaxes(w_in, 1, 2))


def _s5_prep(lam_re, lam_im, b_re, b_im, c_re, c_im, d, log_dt):
    hp = lax.Precision.HIGHEST
    gh, npow = SSM_GROUP, SS + 1
    lr = lam_re.reshape(PAIRS, PAIR_STATE)
    li = lam_im.reshape(PAIRS, PAIR_STATE)
    dt = jnp.repeat(jnp.exp(log_dt), SSM_STATE).reshape(PAIRS, PAIR_STATE)
    ks = jnp.arange(npow, dtype=_F32)[None, :, None]
    mag = jnp.exp((lr * dt)[:, None, :] * ks)
    ang = (li * dt)[:, None, :] * ks
    ak_re, ak_im = mag * jnp.cos(ang), mag * jnp.sin(ang)
    nr, ni = ak_re[:, 1] - 1.0, ak_im[:, 1]
    den = lr * lr + li * li
    z_re = (nr * lr + ni * li) / den
    z_im = (ni * lr - nr * li) / den
    bt_re = jnp.swapaxes(b_re.reshape(PAIRS, PAIR_STATE, gh), 1, 2)
    bt_im = jnp.swapaxes(b_im.reshape(PAIRS, PAIR_STATE, gh), 1, 2)
    bb_re = z_re[:, None, :] * bt_re - z_im[:, None, :] * bt_im
    bb_im = z_re[:, None, :] * bt_im + z_im[:, None, :] * bt_re
    as_hn = lambda c: c.reshape(PAIRS, 2, gh, SSM_STATE).transpose(0, 2, 1, 3).reshape(PAIRS, gh, PAIR_STATE)
    ct_re, ct_im = as_hn(c_re), as_hn(c_im)
    same_group = (jnp.arange(2)[:, None] == (jnp.arange(PAIR_STATE) // SSM_STATE)[None, :]).astype(_F32)
    same_group = same_group[None, None, :, None, :]

    def times_powers(m_re, m_im, pw_re, pw_im):
        pr, pi = pw_re[:, :, None, None, :], pw_im[:, :, None, None, :]
        mr, mi = m_re[:, None, None, :, :], m_im[:, None, None, :, :]
        shape = (PAIRS, npow * PAIR_IN, PAIR_STATE)
        return (((pr * mr - pi * mi) * same_group).reshape(shape),
                ((pr * mi + pi * mr) * same_group).reshape(shape))

    wk_re, wk_im = times_powers(bb_re, bb_im, ak_re[:, ::-1], ak_im[:, ::-1])
    bs = jnp.concatenate([wk_re[:, PAIR_IN:], wk_im[:, PAIR_IN:]], axis=-1)

    ca_re, ca_im = times_powers(ct_re, ct_im, ak_re, ak_im)
    cs = jnp.concatenate([ca_re[:, PAIR_IN:], -ca_im[:, PAIR_IN:]], axis=-1)

    bbm_re, bbm_im = wk_re[:, SS * PAIR_IN:], wk_im[:, SS * PAIR_IN:]
    kt = (jnp.einsum('qrn,qcn->qrc', bbm_re, ca_re[:, :PAIR_W], precision=hp)
          - jnp.einsum('qrn,qcn->qrc', bbm_im, ca_im[:, :PAIR_W], precision=hp))
    ds = jnp.stack([jnp.pad(kt[..., :PAIR_W - PAIR_IN * i], ((0, 0), (0, 0), (PAIR_IN * i, 0)))
                    for i in range(SS)], axis=1).reshape(PAIRS, PAIR_W, PAIR_W)
    skip = jnp.tile(d.reshape(PAIRS, PAIR_IN), (1, SS))
    ds = ds + skip[:, None, :] * jnp.eye(PAIR_W, dtype=_F32)[None]

    ks8 = SS * jnp.arange(SUBLANES + 1, dtype=_F32)
    mag8 = jnp.exp((lr * dt).reshape(HALF)[None, :] * ks8[:, None])
    ang8 = (li * dt).reshape(HALF)[None, :] * ks8[:, None]
    apow = jnp.concatenate([mag8 * jnp.cos(ang8), mag8 * jnp.sin(ang8)], axis=1)
    sub = jnp.arange(SUBLANES)
    shifted = [jnp.where((sub >= k)[:, None], apow[k][None, :], 0.0) for k in (1, 2, 4)]
    a8 = jnp.stack(shifted + [apow[:SUBLANES], jnp.broadcast_to(apow[SUBLANES][None], (SUBLANES, 2 * HALF))])
    return bs.astype(_BF16), cs.astype(_BF16), ds.astype(_BF16), a8


def kernel(x, meta_tokens, g_mix_pre, g_mix_post, w_in, b_gates, ssm_lambda_re, ssm_lambda_im, ssm_b_re,
           ssm_b_im, ssm_c_re, ssm_c_im, ssm_d, ssm_log_dt, w_ssm_glu, w_qk_conv, b_qk_conv, g_head_norm,
           w_branch_ssm, w_branch_mlstm, w_out, g_ffn_pre, g_ffn_post, w_ffn_gate, w_ffn_up, w_ffn_conv,
           b_ffn_conv, w_ffn_down):
    bsz, seq, _ = x.shape
    assert bsz == 1
    depth = w_in.shape[0]
    assert seq % TL == 0 and FRONT_PAD + N_META == TL
    lp = TL + seq
    assert (lp // SS) % SUBLANES == 0

    head = jnp.concatenate([jnp.zeros((FRONT_PAD, D_MODEL), _F32), meta_tokens.astype(_F32)], axis=0)
    stream = (head, x[0])

    nh2 = 2 * MLSTM_HEADS
    c0, w = SSM_WIDTH, MLSTM_WIDTH
    g0 = c0 + 4 * w
    row = lambda a: a[:, None, :]
    w_u, w_mix, w_gr = _regroup_w_in(w_in)
    b_gc = row(jnp.pad(b_gates, ((0, 0), (0, GATE_PAD - nh2))))
    b_gr = jnp.broadcast_to(b_gates[:, :, None], (depth, nh2, LANES))
    w_cv = jnp.concatenate([w_qk_conv, row(b_qk_conv),
                            jnp.zeros((depth, SUBLANES - QK_CONV - 1, 2 * w), _F32)], axis=1)
    mixer_consts = (row(g_mix_pre), row(g_mix_post), w_mix, w_gr, b_gc, b_gr, w_cv,
                    row(g_head_norm), w_ssm_glu.astype(_BF16), w_branch_ssm.astype(_BF16),
                    w_branch_mlstm.astype(_BF16), w_out.astype(_BF16))
    f_cv = jnp.concatenate([w_ffn_conv, row(b_ffn_conv),
                            jnp.zeros((depth, SUBLANES - FFN_CONV - 1, FFN_DIM), _F32)], axis=1)
    ffn_consts = (row(g_ffn_pre), row(g_ffn_post), w_ffn_gate.astype(_BF16), w_ffn_up.astype(_BF16),
                  w_ffn_down.astype(_BF16), f_cv)
    bs, cs, ds, a8 = jax.vmap(_s5_prep)(ssm_lambda_re, ssm_lambda_im, ssm_b_re, ssm_b_im,
                                        ssm_c_re, ssm_c_im, ssm_d, ssm_log_dt)
    g_pre = row(g_mix_pre)

    u = _s5_in(stream, g_pre, w_u, 0, lp)
    for l in range(depth):
        y5 = _s5_core(u, bs, cs, ds, a8, l)
        h = _mixer(stream, y5, mixer_consts, l, lp)
        if l + 1 < depth:
            h, u = _ffn(h, ffn_consts, l, next_s5=(g_pre, w_u))
        else:
            h = _ffn(h, ffn_consts, l)
        stream = (h,)
    return h[None]
```

```python
import functools
import math

import jax
import jax.numpy as jnp
from jax import lax
from jax.experimental import pallas as pl
from jax.experimental.pallas import tpu as pltpu

D_MODEL = 1024
N_META = 16
SSM_WIDTH = 512
SSM_GROUP = 16
SSM_GROUPS = 32
SSM_STATE = 64
MLSTM_WIDTH = 512
MLSTM_HEADS = 4
MLSTM_HEAD_DIM = 128
MLSTM_REF_CHUNK = 64
QK_CONV = 4
FFN_DIM = 2816
FFN_CONV = 3
NORM_EPS = 1e-6
PAD_LOG_INPUT_GATE = -1e4

LANES = 128
SUBLANES = 8
TL = 512
CH = 128
SS = 8
PAIRS = SSM_GROUPS // 2
PAIR_IN = 2 * SSM_GROUP
PAIR_W = SS * PAIR_IN
PAIR_STATE = 2 * SSM_STATE
PAIRS_PER_TILE = LANES // PAIR_IN
HALF = PAIRS * PAIR_STATE
FRONT_PAD = TL - N_META
MIX_BLOCK = 256
MLSTM_STAGE_CHUNKS = 1
S5_ROW_BLOCK = 144
FFN_CHUNK = 256
N_FFN_CHUNKS = FFN_DIM // FFN_CHUNK
FFN_DOWN_GROUP = 11
GATE_PAD = LANES
WIN_V = 2 * MLSTM_WIDTH
WIN_O = WIN_V + MLSTM_WIDTH
WIN_GA = WIN_O + MLSTM_WIDTH
WIN_GB = WIN_GA + D_MODEL
WIN_GATES = WIN_GB + D_MODEL
WIN_COLS = WIN_GATES + GATE_PAD
W_IN_COLS = SSM_WIDTH + 4 * MLSTM_WIDTH + 2 * MLSTM_HEADS + 2 * D_MODEL
W_IN_GATES = SSM_WIDTH + 4 * MLSTM_WIDTH
REGROUP_ROWS = 256
VMEM_LIMIT = 56 * 1024 * 1024

_BF16 = jnp.bfloat16
_F32 = jnp.float32


def _rms(x, g):
    return x * lax.rsqrt(jnp.mean(x * x, axis=-1, keepdims=True) + NORM_EPS) * g


def _gelu_tanh(x):
    return 0.5 * x * (1.0 + jnp.tanh(math.sqrt(2.0 / math.pi) * (x + 0.044715 * (x * x * x))))


def _sigmoid(x):
    return 0.5 + 0.5 * jnp.tanh(0.5 * x)


def _log_sigmoid(x):
    return jnp.minimum(x, 0.0) - jnp.log(1.0 + jnp.exp(-jnp.abs(x)))


def _dot(a, b):
    return jnp.dot(a, b, preferred_element_type=_F32)


def _layer_spec(shape, l):
    nd = len(shape) - 1
    return pl.BlockSpec((None,) + tuple(shape[1:]), lambda i, _l=l, _nd=nd: (_l,) + (0,) * _nd)


def _stream_specs(first):
    if first:
        return [pl.BlockSpec((TL, D_MODEL), lambda i: (0, 0)),
                pl.BlockSpec((TL, D_MODEL), lambda i: (jnp.maximum(i - 1, 0), 0))]
    return [pl.BlockSpec((TL, D_MODEL), lambda i: (i, 0))]


def _stream_tile(first, refs):
    if first:
        head_ref, x_ref, *rest = refs
        return jnp.where(pl.program_id(0) == 0, head_ref[...], x_ref[...]), rest
    x_ref, *rest = refs
    return x_ref[...], rest


def _s5_in_kernel(first, *refs):
    x, (g_ref, w_ref, u_ref, us_ref) = _stream_tile(first, refs)
    _s5_project(x, g_ref, w_ref, u_ref, us_ref)


def _s5_project(x, g_ref, w_ref, u_ref, us_ref):
    hn = _rms(x, g_ref[...]).astype(_BF16)
    u = _dot(hn, w_ref[...])
    for t in range(SSM_WIDTH // LANES):
        us_ref[t] = u[:, t * LANES:(t + 1) * LANES]
    for j in range(SS):
        for t in range(SSM_WIDTH // LANES):
            rows = us_ref[t, pl.ds(j, TL // SS, stride=SS), :].astype(_BF16)
            for k in range(PAIRS_PER_TILE):
                u_ref[t * PAIRS_PER_TILE + k, :, j * PAIR_IN:(j + 1) * PAIR_IN] = (
                    rows[:, k * PAIR_IN:(k + 1) * PAIR_IN])


def _s5_in(stream, g, w, l, lp):
    first = len(stream) == 2
    return pl.pallas_call(
        functools.partial(_s5_in_kernel, first),
        grid=(lp // TL,),
        in_specs=_stream_specs(first) + [_layer_spec(g.shape, l), _layer_spec(w.shape, l)],
        out_specs=pl.BlockSpec((PAIRS, TL // SS, PAIR_W), lambda i: (0, i, 0)),
        out_shape=jax.ShapeDtypeStruct((PAIRS, lp // SS, PAIR_W), _BF16),
        scratch_shapes=[pltpu.VMEM((SSM_WIDTH // LANES, TL, LANES), _F32)],
        compiler_params=pltpu.CompilerParams(dimension_semantics=("parallel",),
                                             vmem_limit_bytes=VMEM_LIMIT),
        name="s5_in",
    )(*stream, g, w)


def _s5_core_kernel(rt, u_ref, bs_ref, cs_ref, ds_ref, a_ref, y_ref, vx_ref, st_ref):
    @pl.when(pl.program_id(0) == 0)
    def _():
        st_ref[...] = jnp.zeros_like(st_ref)

    bounds = list(range(0, rt, S5_ROW_BLOCK)) + [rt]
    blocks = [(bounds[k], bounds[k + 1]) for k in range(len(bounds) - 1)]

    def state_input(r0, r1):
        g0, g1 = r0 // SUBLANES, r1 // SUBLANES
        for p in range(PAIRS):
            v = _dot(u_ref[p, r0:r1, :], bs_ref[p])
            vx_ref[g0:g1, :, p * PAIR_STATE:(p + 1) * PAIR_STATE] = (
                v[:, :PAIR_STATE].reshape(g1 - g0, SUBLANES, PAIR_STATE))
            vx_ref[g0:g1, :, HALF + p * PAIR_STATE:HALF + (p + 1) * PAIR_STATE] = (
                v[:, PAIR_STATE:].reshape(g1 - g0, SUBLANES, PAIR_STATE))

    def output(r0, r1):
        g0, g1 = r0 // SUBLANES, r1 // SUBLANES
        for p in range(PAIRS):
            x_re = vx_ref[g0:g1, :, p * PAIR_STATE:(p + 1) * PAIR_STATE].reshape(r1 - r0, PAIR_STATE)
            x_im = vx_ref[g0:g1, :, HALF + p * PAIR_STATE:HALF + (p + 1) * PAIR_STATE].reshape(r1 - r0, PAIR_STATE)
            xp = jnp.concatenate([x_re, x_im], axis=-1).astype(_BF16)
            y_ref[p, r0:r1, :] = (
                lax.dot_general(xp, cs_ref[p], (((1,), (1,)), ((), ())), preferred_element_type=_F32)
                + _dot(u_ref[p, r0:r1, :], ds_ref[p]))

    def cmul(t, z_re, z_im):
        t_re, t_im = a_ref[t, :, :HALF], a_ref[t, :, HALF:]
        return t_re * z_re - t_im * z_im, t_re * z_im + t_im * z_re

    first = lax.broadcasted_iota(jnp.int32, (SUBLANES, HALF), 0) == 0

    def body(i, carry):
        s_re, s_im = carry
        w_re = vx_ref[i, :, :HALF]
        w_im = vx_ref[i, :, HALF:]
        for t, shift in enumerate((1, 2, 4)):
            d_re, d_im = cmul(t, pltpu.roll(w_re, shift, 0), pltpu.roll(w_im, shift, 0))
            w_re, w_im = w_re + d_re, w_im + d_im
        x_re, x_im = cmul(3, s_re, s_im)
        vx_ref[i, :, :HALF] = x_re + jnp.where(first, 0.0, pltpu.roll(w_re, 1, 0))
        vx_ref[i, :, HALF:] = x_im + jnp.where(first, 0.0, pltpu.roll(w_im, 1, 0))
        n_re, n_im = cmul(4, s_re, s_im)
        last = slice(SUBLANES - 1, SUBLANES)
        return (n_re + jnp.broadcast_to(w_re[last], (SUBLANES, HALF)),
                n_im + jnp.broadcast_to(w_im[last], (SUBLANES, HALF)))

    carry = (st_ref[:, :HALF], st_ref[:, HALF:])
    state_input(*blocks[0])
    for k, (r0, r1) in enumerate(blocks):
        if k + 1 < len(blocks):
            state_input(*blocks[k + 1])
        for i in range(r0 // SUBLANES, r1 // SUBLANES):
            carry = body(i, carry)
        output(r0, r1)
    st_ref[:, :HALF] = carry[0]
    st_ref[:, HALF:] = carry[1]


def _s5_core(u, bs, cs, ds, a8, l):
    rows = u.shape[1]
    rt = rows // 8 if (rows // 8) % SUBLANES == 0 else rows
    return pl.pallas_call(
        functools.partial(_s5_core_kernel, rt),
        grid=(rows // rt,),
        in_specs=[
            pl.BlockSpec((PAIRS, rt, PAIR_W), lambda i: (0, i, 0)),
            _layer_spec(bs.shape, l), _layer_spec(cs.shape, l), _layer_spec(ds.shape, l),
            _layer_spec(a8.shape, l),
        ],
        out_specs=pl.BlockSpec((PAIRS, rt, PAIR_W), lambda i: (0, i, 0)),
        out_shape=jax.ShapeDtypeStruct((PAIRS, rows, PAIR_W), _F32),
        scratch_shapes=[
            pltpu.VMEM((rt // SUBLANES, SUBLANES, 2 * HALF), _F32),
            pltpu.VMEM((SUBLANES, 2 * HALF), _F32),
        ],
        compiler_params=pltpu.CompilerParams(dimension_semantics=("arbitrary",),
                                             vmem_limit_bytes=VMEM_LIMIT),
        name="s5_core",
    )(u, bs, cs, ds, a8)


def _mixer_kernel(first, *refs):
    x, rest = _stream_tile(first, refs)
    (y5_ref, gpre_ref, gpost_ref, win_ref, wgr_ref, bgc_ref, bgr_ref, wcv_ref,
     ghead_ref, wglu_ref, wa_ref, wb_ref,
     wout_ref, o_ref, st_ref, m_ref, yb_ref, y5s_ref, q_ref, k_ref, v_ref,
     so_ref, sga_ref, sgb_ref, *cv_refs) = rest
    pid = pl.program_id(0)

    @pl.when(pid == 0)
    def _():
        for cv_ref in cv_refs:
            cv_ref[0:SUBLANES, :] = jnp.zeros((SUBLANES, MIX_BLOCK), _F32)
        st_ref[...] = jnp.zeros_like(st_ref)
        m_ref[...] = jnp.zeros_like(m_ref)
        for h in range(MLSTM_HEADS):
            v_ref[h, :, MLSTM_HEAD_DIM:] = jnp.ones((TL, MLSTM_HEAD_DIM), _BF16)

    hn = _rms(x, gpre_ref[...]).astype(_BF16)

    row_c = pid * TL + lax.broadcasted_iota(jnp.int32, (TL, 1), 0)
    valid_c = row_c >= FRONT_PAD
    row_r = pid * TL + lax.broadcasted_iota(jnp.int32, (1, TL), 1)
    valid_r = row_r >= FRONT_PAD

    n_blocks = 2 * MLSTM_WIDTH // MIX_BLOCK

    def proj_block(b):
        cols = slice(b * MIX_BLOCK, (b + 1) * MIX_BLOCK)
        cv_refs[b][SUBLANES:SUBLANES + TL, :] = _dot(hn, win_ref[:, cols])

    def conv_block(b):
        cols = slice(b * MIX_BLOCK, (b + 1) * MIX_BLOCK)
        cv_ref = cv_refs[b]
        acc = wcv_ref[QK_CONV:QK_CONV + 1, cols]
        for j in range(QK_CONV):
            off = SUBLANES - (QK_CONV - 1) + j
            acc = acc + wcv_ref[j:j + 1, cols] * cv_ref[off:off + TL, :]
        cv_ref[0:SUBLANES, :] = cv_ref[TL:TL + SUBLANES, :]
        qk = jnp.where(valid_c, acc * _sigmoid(acc), 0.0)
        if b < MLSTM_WIDTH // MIX_BLOCK:
            q_ref[:, cols] = qk.astype(_BF16)
        else:
            kcols = slice(b * MIX_BLOCK - MLSTM_WIDTH, (b + 1) * MIX_BLOCK - MLSTM_WIDTH)
            k_ref[:, kcols] = qk * (MLSTM_HEAD_DIM ** -0.5)

    gate_w = D_MODEL // n_blocks

    def branch_gate_block(dst_ref, base, b):
        dst_ref[:, b * gate_w:(b + 1) * gate_w] = _sigmoid(
            _dot(hn, win_ref[:, base + b * gate_w:base + (b + 1) * gate_w]))

    q_blocks = MLSTM_WIDTH // MIX_BLOCK
    for b in range(q_blocks, n_blocks):
        proj_block(b)
    v_all = jnp.where(valid_c, _dot(hn, win_ref[:, WIN_V:WIN_O]), 0.0).astype(_BF16)
    for h in range(MLSTM_HEADS):
        v_ref[h, :, :MLSTM_HEAD_DIM] = v_all[:, h * MLSTM_HEAD_DIM:(h + 1) * MLSTM_HEAD_DIM]
    lane = lax.broadcasted_iota(jnp.int32, (1, GATE_PAD), 1)
    gc = _dot(hn, win_ref[:, WIN_GATES:]) + bgc_ref[...]
    gc = jnp.where(lane < MLSTM_HEADS, gc, _log_sigmoid(gc))
    gc = jnp.where(valid_c, gc, jnp.where(lane < MLSTM_HEADS, PAD_LOG_INPUT_GATE, 0.0))
    sub = lax.broadcasted_iota(jnp.int32, (2 * MLSTM_HEADS, 1), 0)
    gr = lax.dot_general(wgr_ref[...], hn, (((1,), (1,)), ((), ())),
                         preferred_element_type=_F32) + bgr_ref[:, 0:1]
    gr = jnp.where(sub < MLSTM_HEADS, gr, _log_sigmoid(gr))
    gr = jnp.where(valid_r, gr, jnp.where(sub < MLSTM_HEADS, PAD_LOG_INPUT_GATE, 0.0))
    for b in range(q_blocks):
        conv_block(q_blocks + b)
        proj_block(b)

    it = lax.broadcasted_iota(jnp.int32, (CH, CH), 0)
    js = lax.broadcasted_iota(jnp.int32, (CH, CH), 1)
    causal = it >= js
    tri_l = causal.astype(_F32)
    tri_u = (it <= js).astype(_F32)

    def regroup_s5(j):
        for t in range(SSM_WIDTH // LANES):
            yj = jnp.concatenate([y5_ref[t * PAIRS_PER_TILE + k, :, j * PAIR_IN:(j + 1) * PAIR_IN]
                                  for k in range(PAIRS_PER_TILE)], axis=-1)
            y5s_ref[t, pl.ds(j, TL // SS, stride=SS), :] = yj

    n_chunks = TL // CH
    hp = lax.Precision.HIGHEST
    rows = [slice(c * CH, (c + 1) * CH) for c in range(n_chunks)]
    heads = [slice(h * MLSTM_HEAD_DIM, (h + 1) * MLSTM_HEAD_DIM) for h in range(MLSTM_HEADS)]

    gcc = [gc[rows[c], :] for c in range(n_chunks)]
    b_c = [jnp.dot(tri_l, gcc[c], preferred_element_type=_F32, precision=hp) for c in range(n_chunks)]
    b_r = [jnp.dot(gr[:, rows[c]], tri_u, preferred_element_type=_F32, precision=hp) for c in range(n_chunks)]

    rrow = [[None] * MLSTM_HEADS for _ in range(n_chunks)]
    m_in = [[None] * MLSTM_HEADS for _ in range(n_chunks)]
    m_out = [[None] * MLSTM_HEADS for _ in range(n_chunks)]
    for h in range(MLSTM_HEADS):
        fl = MLSTM_HEADS + h
        m = m_ref[h:h + 1, :]
        for c in range(n_chunks):
            rrow[c][h] = gr[h:h + 1, rows[c]] - b_r[c][fl:fl + 1, :]
            m_in[c][h] = m
            m_out[c][h] = jnp.maximum(m, jnp.max(rrow[c][h], axis=-1, keepdims=True))
            m = b_r[c][fl:fl + 1, CH - 1:CH] + m_out[c][h]
        m_ref[h:h + 1, :] = m

    st_in = [[None] * MLSTM_HEADS for _ in range(n_chunks)]
    for h in range(MLSTM_HEADS):
        fl = MLSTM_HEADS + h
        st = st_ref[h]
        for c in range(n_chunks):
            st_in[c][h] = st
            rcol = gcc[c][:, h:h + 1] - b_c[c][:, fl:fl + 1]
            kw = k_ref[rows[c], heads[h]] * jnp.exp(rcol - m_out[c][h][:, 0:1])
            decay = jnp.exp(m_in[c][h] - m_out[c][h])
            st = (jnp.concatenate([decay, decay], axis=-1) * st
                  + _dot(kw.T.astype(_BF16), v_ref[h, rows[c], :]))
        st_ref[h] = st

    for b in range(q_blocks):
        conv_block(b)
    so_ref[...] = _sigmoid(_dot(hn, win_ref[:, WIN_O:WIN_GA]))

    for c0 in range(0, n_chunks, MLSTM_STAGE_CHUNKS):
        group = range(c0, c0 + MLSTM_STAGE_CHUNKS)
        for c in group:
            branch_gate_block(sga_ref, WIN_GA, c)
            branch_gate_block(sgb_ref, WIN_GB, c)
            for j in range(c * SS // n_chunks, (c + 1) * SS // n_chunks):
                regroup_s5(j)
        items = [(c, h) for c in group for h in range(MLSTM_HEADS)]
        qc = [q_ref[rows[c], heads[h]] for c, h in items]
        qk = [lax.dot_general(qc[i], k_ref[rows[c], heads[h]].astype(_BF16), (((1,), (1,)), ((), ())),
                              preferred_element_type=_F32) for i, (c, h) in enumerate(items)]
        qs = [_dot(qc[i], st_in[c][h].astype(_BF16)) for i, (c, h) in enumerate(items)]
        mc1, mcol, s = [], [], []
        for i, (c, h) in enumerate(items):
            rmat = jnp.where(causal, rrow[c][h], -jnp.inf)
            mc1.append(jnp.maximum(m_in[c][h][:, 0:1], jnp.max(rmat, axis=-1, keepdims=True)))
            mcol.append(jnp.broadcast_to(mc1[i], (CH, LANES)))
            s.append((qk[i] * jnp.exp(rmat - mcol[i])).astype(_BF16))
        sv = [_dot(s[i], v_ref[h, rows[c], :]) for i, (c, h) in enumerate(items)]
        for i, (c, h) in enumerate(items):
            hs = heads[h]
            bcol = b_c[c][:, MLSTM_HEADS + h:MLSTM_HEADS + h + 1]
            floor = jnp.broadcast_to(jnp.exp(-(bcol + mc1[i])), (CH, LANES))
            w_inter = jnp.exp(m_in[c][h] - mcol[i])
            num = sv[i][:, :MLSTM_HEAD_DIM] + w_inter * qs[i][:, :MLSTM_HEAD_DIM]
            den = sv[i][:, MLSTM_HEAD_DIM:] + w_inter * qs[i][:, MLSTM_HEAD_DIM:]
            hout = num / jnp.maximum(jnp.abs(den), floor)
            hout = hout * lax.rsqrt(jnp.mean(hout * hout, axis=-1, keepdims=True) + NORM_EPS)
            hout = hout * ghead_ref[:, hs] * so_ref[rows[c], hs]
            yb_ref[rows[c], hs] = hout.astype(_BF16)

    ya = _gelu_tanh(jnp.concatenate([y5s_ref[t] for t in range(SSM_WIDTH // LANES)], axis=-1))
    ya = ya * _sigmoid(_dot(ya.astype(_BF16), wglu_ref[...]))

    merged = (sga_ref[...] * _dot(ya.astype(_BF16), wa_ref[...])
              + sgb_ref[...] * _dot(yb_ref[...], wb_ref[...]))
    o_ref[...] = x + _rms(_dot(merged.astype(_BF16), wout_ref[...]), gpost_ref[...])


def _mixer(stream, y5, consts, l, lp):
    first = len(stream) == 2
    return pl.pallas_call(
        functools.partial(_mixer_kernel, first),
        grid=(lp // TL,),
        in_specs=_stream_specs(first) + [
            pl.BlockSpec((PAIRS, TL // SS, PAIR_W), lambda i: (0, i, 0)),
        ] + [_layer_spec(c.shape, l) for c in consts],
        out_specs=pl.BlockSpec((TL, D_MODEL), lambda i: (i, 0)),
        out_shape=jax.ShapeDtypeStruct((lp, D_MODEL), _F32),
        scratch_shapes=[
            pltpu.VMEM((MLSTM_HEADS, MLSTM_HEAD_DIM, 2 * MLSTM_HEAD_DIM), _F32),
            pltpu.VMEM((SUBLANES, LANES), _F32),
            pltpu.VMEM((TL, MLSTM_WIDTH), _BF16),
            pltpu.VMEM((SSM_WIDTH // LANES, TL, LANES), _F32),
            pltpu.VMEM((TL, MLSTM_WIDTH), _BF16),
            pltpu.VMEM((TL, MLSTM_WIDTH), _F32),
            pltpu.VMEM((MLSTM_HEADS, TL, 2 * MLSTM_HEAD_DIM), _BF16),
            pltpu.VMEM((TL, MLSTM_WIDTH), _F32),
            pltpu.VMEM((TL, D_MODEL), _F32),
            pltpu.VMEM((TL, D_MODEL), _F32),
        ] + [pltpu.VMEM((TL + SUBLANES, MIX_BLOCK), _F32)
             for _ in range(2 * MLSTM_WIDTH // MIX_BLOCK)],
        compiler_params=pltpu.CompilerParams(dimension_semantics=("arbitrary",),
                                             vmem_limit_bytes=VMEM_LIMIT),
        name="mixer",
    )(*stream, y5, *consts)


def _ffn_kernel(fuse_next, n_tiles, *refs):
    if fuse_next:
        (x_ref, gpre_ref, gpost_ref, wg_ref, wu_ref, wd_ref, wcv_ref, gnext_ref, wnext_ref,
         o_ref, u_ref, cv_ref, act_ref, prev_ref, us_ref) = refs
    else:
        x_ref, gpre_ref, gpost_ref, wg_ref, wu_ref, wd_ref, wcv_ref, o_ref, cv_ref, act_ref = refs
    pid = pl.program_id(0)

    @pl.when(pid == 0)
    def _():
        cv_ref[0:SUBLANES, :] = jnp.zeros((SUBLANES, FFN_DIM), _F32)
        if fuse_next:
            prev_ref[...] = jnp.zeros_like(prev_ref)

    def tile():
        if fuse_next:
            _s5_project(prev_ref[...], gnext_ref, wnext_ref, u_ref, us_ref)
        out = _ffn_tile(x_ref[...], gpre_ref, gpost_ref, wg_ref, wu_ref, wd_ref, wcv_ref, cv_ref, act_ref)
        o_ref[...] = out
        if fuse_next:
            prev_ref[...] = out

    if fuse_next:
        pl.when(pid < n_tiles)(tile)

        @pl.when(pid == n_tiles)
        def _():
            _s5_project(prev_ref[...], gnext_ref, wnext_ref, u_ref, us_ref)
    else:
        tile()


def _ffn_tile(x, gpre_ref, gpost_ref, wg_ref, wu_ref, wd_ref, wcv_ref, cv_ref, act_ref):
    hn = _rms(x, gpre_ref[...]).astype(_BF16)
    chunk = lambda c: slice(c * FFN_CHUNK, (c + 1) * FFN_CHUNK)
    acc = None
    group_start = 0
    cv_ref[SUBLANES:SUBLANES + TL, chunk(0)] = _dot(hn, wg_ref[:, chunk(0)])
    up_next = _dot(hn, wu_ref[:, chunk(0)])
    for c in range(N_FFN_CHUNKS):
        cs = chunk(c)
        up = up_next
        if c + 1 < N_FFN_CHUNKS:
            cv_ref[SUBLANES:SUBLANES + TL, chunk(c + 1)] = _dot(hn, wg_ref[:, chunk(c + 1)])
            up_next = _dot(hn, wu_ref[:, chunk(c + 1)])
        conv = wcv_ref[FFN_CONV:FFN_CONV + 1, cs]
        for j in range(FFN_CONV):
            off = SUBLANES - (FFN_CONV - 1) + j
            conv = conv + wcv_ref[j:j + 1, cs] * cv_ref[off:off + TL, cs]
        cv_ref[0:SUBLANES, cs] = cv_ref[TL:TL + SUBLANES, cs]
        act_ref[:, cs] = (_gelu_tanh(conv) * up).astype(_BF16)
        if (c + 1) % FFN_DOWN_GROUP == 0 or c + 1 == N_FFN_CHUNKS:
            gs = slice(group_start * FFN_CHUNK, (c + 1) * FFN_CHUNK)
            part = _dot(act_ref[:, gs], wd_ref[gs, :])
            acc = part if acc is None else acc + part
            group_start = c + 1
    return x + _rms(acc, gpost_ref[...])


def _ffn(x, consts, l, next_s5=None):
    lp = x.shape[0]
    n_tiles = lp // TL
    scratch = [pltpu.VMEM((TL + SUBLANES, FFN_DIM), _F32),
               pltpu.VMEM((TL, FFN_DIM), _BF16)]
    in_specs = [_layer_spec(c.shape, l) for c in consts]
    if next_s5 is None:
        grid = n_tiles
        x_spec = pl.BlockSpec((TL, D_MODEL), lambda i: (i, 0))
        out_specs = pl.BlockSpec((TL, D_MODEL), lambda i: (jnp.maximum(i - 1, 0), 0))
        out_shape = jax.ShapeDtypeStruct((lp - TL, D_MODEL), _F32)
        args = (x, *consts)
    else:
        grid = n_tiles + 1
        x_spec = pl.BlockSpec((TL, D_MODEL), lambda i: (jnp.minimum(i, n_tiles - 1), 0))
        out_specs = [pl.BlockSpec((TL, D_MODEL), lambda i: (jnp.minimum(i, n_tiles - 1), 0)),
                     pl.BlockSpec((PAIRS, TL // SS, PAIR_W), lambda i: (0, jnp.maximum(i - 1, 0), 0))]
        out_shape = [jax.ShapeDtypeStruct((lp, D_MODEL), _F32),
                     jax.ShapeDtypeStruct((PAIRS, lp // SS, PAIR_W), _BF16)]
        in_specs += [_layer_spec(a.shape, l + 1) for a in next_s5]
        scratch += [pltpu.VMEM((TL, D_MODEL), _F32),
                    pltpu.VMEM((SSM_WIDTH // LANES, TL, LANES), _F32)]
        args = (x, *consts, *next_s5)
    return pl.pallas_call(
        functools.partial(_ffn_kernel, next_s5 is not None, n_tiles),
        grid=(grid,),
        in_specs=[x_spec] + in_specs,
        out_specs=out_specs,
        out_shape=out_shape,
        scratch_shapes=scratch,
        compiler_params=pltpu.CompilerParams(dimension_semantics=("arbitrary",),
                                             vmem_limit_bytes=VMEM_LIMIT),
        name="ffn",
    )(*args)


def _regroup_w_in_kernel(wt_ref, wu_ref, wmix_ref, wgr_ref):
    piece = SSM_WIDTH

    def rows_to_cols(src):
        return wt_ref[src:src + piece, :].T.astype(_BF16)

    wu_ref[...] = rows_to_cols(0)
    tail = W_IN_GATES + 2 * MLSTM_HEADS
    for k in range(WIN_GA // piece):
        wmix_ref[:, k * piece:(k + 1) * piece] = rows_to_cols(SSM_WIDTH + k * piece)
    for k in range(2 * D_MODEL // piece):
        wmix_ref[:, WIN_GA + k * piece:WIN_GA + (k + 1) * piece] = rows_to_cols(tail + k * piece)
    rb = wt_ref.shape[1]
    lane = lax.broadcasted_iota(jnp.int32, (rb, GATE_PAD), 1)
    gates = wt_ref[W_IN_GATES:W_IN_GATES + GATE_PAD, :].T
    wmix_ref[:, WIN_GATES:] = jnp.where(lane < 2 * MLSTM_HEADS, gates, 0.0).astype(_BF16)
    wgr_ref[...] = wt_ref[W_IN_GATES:W_IN_GATES + 2 * MLSTM_HEADS, :].astype(_BF16)


def _regroup_w_in(w_in):
    depth = w_in.shape[0]
    assert w_in.shape[1:] == (D_MODEL, W_IN_COLS)
    return pl.pallas_call(
        _regroup_w_in_kernel,
        grid=(depth, D_MODEL // REGROUP_ROWS),
        in_specs=[pl.BlockSpec((None, W_IN_COLS, REGROUP_ROWS), lambda l, r: (l, 0, r))],
        out_specs=[pl.BlockSpec((None, REGROUP_ROWS, SSM_WIDTH), lambda l, r: (l, r, 0)),
                   pl.BlockSpec((None, REGROUP_ROWS, WIN_COLS), lambda l, r: (l, r, 0)),
                   pl.BlockSpec((None, 2 * MLSTM_HEADS, REGROUP_ROWS), lambda l, r: (l, 0, r))],
        out_shape=[jax.ShapeDtypeStruct((depth, D_MODEL, SSM_WIDTH), _BF16),
                   jax.ShapeDtypeStruct((depth, D_MODEL, WIN_COLS), _BF16),
                   jax.ShapeDtypeStruct((depth, 2 * MLSTM_HEADS, D_MODEL), _BF16)],
        compiler_params=pltpu.CompilerParams(dimension_semantics=("parallel", "parallel"),
                                             vmem_limit_bytes=VMEM_LIMIT),
        name="regroup_w_in",
    )(jnp.swapaxes(w_in, 1, 2))


def _s5_prep(lam_re, lam_im, b_re, b_im, c_re, c_im, d, log_dt):
    hp = lax.Precision.HIGHEST
    gh, npow = SSM_GROUP, SS + 1
    lr = lam_re.reshape(PAIRS, PAIR_STATE)
    li = lam_im.reshape(PAIRS, PAIR_STATE)
    dt = jnp.repeat(jnp.exp(log_dt), SSM_STATE).reshape(PAIRS, PAIR_STATE)
    ks = jnp.arange(npow, dtype=_F32)[None, :, None]
    mag = jnp.exp((lr * dt)[:, None, :] * ks)
    ang = (li * dt)[:, None, :] * ks
    ak_re, ak_im = mag * jnp.cos(ang), mag * jnp.sin(ang)
    nr, ni = ak_re[:, 1] - 1.0, ak_im[:, 1]
    den = lr * lr + li * li
    z_re = (nr * lr + ni * li) / den
    z_im = (ni * lr - nr * li) / den
    bt_re = jnp.swapaxes(b_re.reshape(PAIRS, PAIR_STATE, gh), 1, 2)
    bt_im = jnp.swapaxes(b_im.reshape(PAIRS, PAIR_STATE, gh), 1, 2)
    bb_re = z_re[:, None, :] * bt_re - z_im[:, None, :] * bt_im
    bb_im = z_re[:, None, :] * bt_im + z_im[:, None, :] * bt_re
    as_hn = lambda c: c.reshape(PAIRS, 2, gh, SSM_STATE).transpose(0, 2, 1, 3).reshape(PAIRS, gh, PAIR_STATE)
    ct_re, ct_im = as_hn(c_re), as_hn(c_im)
    same_group = (jnp.arange(2)[:, None] == (jnp.arange(PAIR_STATE) // SSM_STATE)[None, :]).astype(_F32)
    same_group = same_group[None, None, :, None, :]

    def times_powers(m_re, m_im, pw_re, pw_im):
        pr, pi = pw_re[:, :, None, None, :], pw_im[:, :, None, None, :]
        mr, mi = m_re[:, None, None, :, :], m_im[:, None, None, :, :]
        shape = (PAIRS, npow * PAIR_IN, PAIR_STATE)
        return (((pr * mr - pi * mi) * same_group).reshape(shape),
                ((pr * mi + pi * mr) * same_group).reshape(shape))

    wk_re, wk_im = times_powers(bb_re, bb_im, ak_re[:, ::-1], ak_im[:, ::-1])
    bs = jnp.concatenate([wk_re[:, PAIR_IN:], wk_im[:, PAIR_IN:]], axis=-1)

    ca_re, ca_im = times_powers(ct_re, ct_im, ak_re, ak_im)
    cs = jnp.concatenate([ca_re[:, PAIR_IN:], -ca_im[:, PAIR_IN:]], axis=-1)

    bbm_re, bbm_im = wk_re[:, SS * PAIR_IN:], wk_im[:, SS * PAIR_IN:]
    kt = (jnp.einsum('qrn,qcn->qrc', bbm_re, ca_re[:, :PAIR_W], precision=hp)
          - jnp.einsum('qrn,qcn->qrc', bbm_im, ca_im[:, :PAIR_W], precision=hp))
    ds = jnp.stack([jnp.pad(kt[..., :PAIR_W - PAIR_IN * i], ((0, 0), (0, 0), (PAIR_IN * i, 0)))
                    for i in range(SS)], axis=1).reshape(PAIRS, PAIR_W, PAIR_W)
    skip = jnp.tile(d.reshape(PAIRS, PAIR_IN), (1, SS))
    ds = ds + skip[:, None, :] * jnp.eye(PAIR_W, dtype=_F32)[None]

    ks8 = SS * jnp.arange(SUBLANES + 1, dtype=_F32)
    mag8 = jnp.exp((lr * dt).reshape(HALF)[None, :] * ks8[:, None])
    ang8 = (li * dt).reshape(HALF)[None, :] * ks8[:, None]
    apow = jnp.concatenate([mag8 * jnp.cos(ang8), mag8 * jnp.sin(ang8)], axis=1)
    sub = jnp.arange(SUBLANES)
    shifted = [jnp.where((sub >= k)[:, None], apow[k][None, :], 0.0) for k in (1, 2, 4)]
    a8 = jnp.stack(shifted + [apow[:SUBLANES], jnp.broadcast_to(apow[SUBLANES][None], (SUBLANES, 2 * HALF))])
    return bs.astype(_BF16), cs.astype(_BF16), ds.astype(_BF16), a8


def kernel(x, meta_tokens, g_mix_pre, g_mix_post, w_in, b_gates, ssm_lambda_re, ssm_lambda_im, ssm_b_re,
           ssm_b_im, ssm_c_re, ssm_c_im, ssm_d, ssm_log_dt, w_ssm_glu, w_qk_conv, b_qk_conv, g_head_norm,
           w_branch_ssm, w_branch_mlstm, w_out, g_ffn_pre, g_ffn_post, w_ffn_gate, w_ffn_up, w_ffn_conv,
           b_ffn_conv, w_ffn_down):
    bsz, seq, _ = x.shape
    assert bsz == 1
    depth = w_in.shape[0]
    assert seq % TL == 0 and FRONT_PAD + N_META == TL
    lp = TL + seq
    assert (lp // SS) % SUBLANES == 0

    head = jnp.concatenate([jnp.zeros((FRONT_PAD, D_MODEL), _F32), meta_tokens.astype(_F32)], axis=0)
    stream = (head, x[0])

    nh2 = 2 * MLSTM_HEADS
    c0, w = SSM_WIDTH, MLSTM_WIDTH
    g0 = c0 + 4 * w
    row = lambda a: a[:, None, :]
    w_u, w_mix, w_gr = _regroup_w_in(w_in)
    b_gc = row(jnp.pad(b_gates, ((0, 0), (0, GATE_PAD - nh2))))
    b_gr = jnp.broadcast_to(b_gates[:, :, None], (depth, nh2, LANES))
    w_cv = jnp.concatenate([w_qk_conv, row(b_qk_conv),
                            jnp.zeros((depth, SUBLANES - QK_CONV - 1, 2 * w), _F32)], axis=1)
    mixer_consts = (row(g_mix_pre), row(g_mix_post), w_mix, w_gr, b_gc, b_gr, w_cv,
                    row(g_head_norm), w_ssm_glu.astype(_BF16), w_branch_ssm.astype(_BF16),
                    w_branch_mlstm.astype(_BF16), w_out.astype(_BF16))
    f_cv = jnp.concatenate([w_ffn_conv, row(b_ffn_conv),
                            jnp.zeros((depth, SUBLANES - FFN_CONV - 1, FFN_DIM), _F32)], axis=1)
    ffn_consts = (row(g_ffn_pre), row(g_ffn_post), w_ffn_gate.astype(_BF16), w_ffn_up.astype(_BF16),
                  w_ffn_down.astype(_BF16), f_cv)
    bs, cs, ds, a8 = jax.vmap(_s5_prep)(ssm_lambda_re, ssm_lambda_im, ssm_b_re, ssm_b_im,
                                        ssm_c_re, ssm_c_im, ssm_d, ssm_log_dt)
    g_pre = row(g_mix_pre)

    u = _s5_in(stream, g_pre, w_u, 0, lp)
    for l in range(depth):
        y5 = _s5_core(u, bs, cs, ds, a8, l)
        h = _mixer(stream, y5, mixer_consts, l, lp)
        if l + 1 < depth:
            h, u = _ffn(h, ffn_consts, l, next_s5=(g_pre, w_u))
        else:
            h = _ffn(h, ffn_consts, l)
        stream = (h,)
    return h[None]
```

```python
import functools
import math

import jax
import jax.numpy as jnp
from jax import lax
from jax.experimental import pallas as pl
from jax.experimental.pallas import tpu as pltpu

D_MODEL = 1024
N_META = 16
SSM_WIDTH = 512
SSM_GROUP = 16
SSM_GROUPS = 32
SSM_STATE = 64
MLSTM_WIDTH = 512
MLSTM_HEADS = 4
MLSTM_HEAD_DIM = 128
QK_CONV = 4
FFN_DIM = 2816
FFN_CONV = 3
NORM_EPS = 1e-6
PAD_LOG_INPUT_GATE = -1e4

LANES = 128
SUBLANES = 8
TL = 512
CH = 128
SS = 8
PAIRS = SSM_GROUPS // 2
PAIR_IN = 2 * SSM_GROUP
PAIR_W = SS * PAIR_IN
PAIR_STATE = 2 * SSM_STATE
PAIRS_PER_TILE = LANES // PAIR_IN
HALF = PAIRS * PAIR_STATE
FRONT_PAD = TL - N_META
MIX_BLOCK = 256
MLSTM_STAGE_CHUNKS = 1
S5_ROW_BLOCK = 144
FFN_CHUNK = 256
N_FFN_CHUNKS = FFN_DIM // FFN_CHUNK
FFN_DOWN_GROUP = 11
GATE_PAD = LANES
WIN_V = 2 * MLSTM_WIDTH
WIN_O = WIN_V + MLSTM_WIDTH
WIN_GA = WIN_O + MLSTM_WIDTH
WIN_GB = WIN_GA + D_MODEL
WIN_GATES = WIN_GB + D_MODEL
WIN_COLS = WIN_GATES + GATE_PAD
W_IN_COLS = SSM_WIDTH + 4 * MLSTM_WIDTH + 2 * MLSTM_HEADS + 2 * D_MODEL
W_IN_GATES = SSM_WIDTH + 4 * MLSTM_WIDTH
REGROUP_ROWS = 256
VMEM_LIMIT = 56 * 1024 * 1024

_BF16 = jnp.bfloat16
_F32 = jnp.float32


def _rms(x, g):
    return x * lax.rsqrt(jnp.mean(x * x, axis=-1, keepdims=True) + NORM_EPS) * g


def _gelu_tanh(x):
    return 0.5 * x * (1.0 + jnp.tanh(math.sqrt(2.0 / math.pi) * (x + 0.044715 * (x * x * x))))


def _sigmoid(x):
    return 0.5 + 0.5 * jnp.tanh(0.5 * x)


def _log_sigmoid(x):
    return jnp.minimum(x, 0.0) - jnp.log(1.0 + jnp.exp(-jnp.abs(x)))


def _dot(a, b):
    return jnp.dot(a, b, preferred_element_type=_F32)


def _layer_spec(shape, l):
    nd = len(shape) - 1
    return pl.BlockSpec((None,) + tuple(shape[1:]), lambda i, _l=l, _nd=nd: (_l,) + (0,) * _nd)


def _stream_specs(first):
    if first:
        return [pl.BlockSpec((TL, D_MODEL), lambda i: (0, 0)),
                pl.BlockSpec((TL, D_MODEL), lambda i: (jnp.maximum(i - 1, 0), 0))]
    return [pl.BlockSpec((TL, D_MODEL), lambda i: (i, 0))]


def _stream_tile(first, refs):
    if first:
        head_ref, x_ref, *rest = refs
        return jnp.where(pl.program_id(0) == 0, head_ref[...], x_ref[...]), rest
    x_ref, *rest = refs
    return x_ref[...], rest


def _s5_in_kernel(first, *refs):
    x, (g_ref, w_ref, u_ref, us_ref) = _stream_tile(first, refs)
    _s5_project(x, g_ref, w_ref, u_ref, us_ref)


def _s5_project(x, g_ref, w_ref, u_ref, us_ref):
    hn = _rms(x, g_ref[...]).astype(_BF16)
    u = _dot(hn, w_ref[...])
    for t in range(SSM_WIDTH // LANES):
        us_ref[t] = u[:, t * LANES:(t + 1) * LANES]
    for j in range(SS):
        for t in range(SSM_WIDTH // LANES):
            rows = us_ref[t, pl.ds(j, TL // SS, stride=SS), :].astype(_BF16)
            for k in range(PAIRS_PER_TILE):
                u_ref[t * PAIRS_PER_TILE + k, :, j * PAIR_IN:(j + 1) * PAIR_IN] = (
                    rows[:, k * PAIR_IN:(k + 1) * PAIR_IN])


def _s5_in(stream, g, w, l, lp):
    first = len(stream) == 2
    return pl.pallas_call(
        functools.partial(_s5_in_kernel, first),
        grid=(lp // TL,),
        in_specs=_stream_specs(first) + [_layer_spec(g.shape, l), _layer_spec(w.shape, l)],
        out_specs=pl.BlockSpec((PAIRS, TL // SS, PAIR_W), lambda i: (0, i, 0)),
        out_shape=jax.ShapeDtypeStruct((PAIRS, lp // SS, PAIR_W), _BF16),
        scratch_shapes=[pltpu.VMEM((SSM_WIDTH // LANES, TL, LANES), _F32)],
        compiler_params=pltpu.CompilerParams(dimension_semantics=("parallel",),
                                             vmem_limit_bytes=VMEM_LIMIT),
        name="s5_in",
    )(*stream, g, w)


def _s5_core_kernel(rt, u_ref, bs_ref, cs_ref, ds_ref, a_ref, y_ref, vx_ref, st_ref):
    @pl.when(pl.program_id(0) == 0)
    def _():
        st_ref[...] = jnp.zeros_like(st_ref)

    bounds = list(range(0, rt, S5_ROW_BLOCK)) + [rt]
    blocks = [(bounds[k], bounds[k + 1]) for k in range(len(bounds) - 1)]

    def state_input(r0, r1):
        g0, g1 = r0 // SUBLANES, r1 // SUBLANES
        for p in range(PAIRS):
            v = _dot(u_ref[p, r0:r1, :], bs_ref[p])
            vx_ref[g0:g1, :, p * PAIR_STATE:(p + 1) * PAIR_STATE] = (
                v[:, :PAIR_STATE].reshape(g1 - g0, SUBLANES, PAIR_STATE))
            vx_ref[g0:g1, :, HALF + p * PAIR_STATE:HALF + (p + 1) * PAIR_STATE] = (
                v[:, PAIR_STATE:].reshape(g1 - g0, SUBLANES, PAIR_STATE))

    def output(r0, r1):
        g0, g1 = r0 // SUBLANES, r1 // SUBLANES
        for p in range(PAIRS):
            x_re = vx_ref[g0:g1, :, p * PAIR_STATE:(p + 1) * PAIR_STATE].reshape(r1 - r0, PAIR_STATE)
            x_im = vx_ref[g0:g1, :, HALF + p * PAIR_STATE:HALF + (p + 1) * PAIR_STATE].reshape(r1 - r0, PAIR_STATE)
            xp = jnp.concatenate([x_re, x_im], axis=-1).astype(_BF16)
            y_ref[p, r0:r1, :] = (
                lax.dot_general(xp, cs_ref[p], (((1,), (1,)), ((), ())), preferred_element_type=_F32)
                + _dot(u_ref[p, r0:r1, :], ds_ref[p]))

    def cmul(t, z_re, z_im):
        t_re, t_im = a_ref[t, :, :HALF], a_ref[t, :, HALF:]
        return t_re * z_re - t_im * z_im, t_re * z_im + t_im * z_re

    first = lax.broadcasted_iota(jnp.int32, (SUBLANES, HALF), 0) == 0

    def body(i, carry):
        s_re, s_im = carry
        w_re = vx_ref[i, :, :HALF]
        w_im = vx_ref[i, :, HALF:]
        for t, shift in enumerate((1, 2, 4)):
            d_re, d_im = cmul(t, pltpu.roll(w_re, shift, 0), pltpu.roll(w_im, shift, 0))
            w_re, w_im = w_re + d_re, w_im + d_im
        x_re, x_im = cmul(3, s_re, s_im)
        vx_ref[i, :, :HALF] = x_re + jnp.where(first, 0.0, pltpu.roll(w_re, 1, 0))
        vx_ref[i, :, HALF:] = x_im + jnp.where(first, 0.0, pltpu.roll(w_im, 1, 0))
        n_re, n_im = cmul(4, s_re, s_im)
        last = slice(SUBLANES - 1, SUBLANES)
        return (n_re + jnp.broadcast_to(w_re[last], (SUBLANES, HALF)),
                n_im + jnp.broadcast_to(w_im[last], (SUBLANES, HALF)))

    carry = (st_ref[:, :HALF], st_ref[:, HALF:])
    state_input(*blocks[0])
    for k, (r0, r1) in enumerate(blocks):
        if k + 1 < len(blocks):
            state_input(*blocks[k + 1])
        for i in range(r0 // SUBLANES, r1 // SUBLANES):
            carry = body(i, carry)
        output(r0, r1)
    st_ref[:, :HALF] = carry[0]
    st_ref[:, HALF:] = carry[1]


def _s5_core(u, bs, cs, ds, a8, l):
    rows = u.shape[1]
    rt = rows // 8 if (rows // 8) % SUBLANES == 0 else rows
    return pl.pallas_call(
        functools.partial(_s5_core_kernel, rt),
        grid=(rows // rt,),
        in_specs=[
            pl.BlockSpec((PAIRS, rt, PAIR_W), lambda i: (0, i, 0)),
            _layer_spec(bs.shape, l), _layer_spec(cs.shape, l), _layer_spec(ds.shape, l),
            _layer_spec(a8.shape, l),
        ],
        out_specs=pl.BlockSpec((PAIRS, rt, PAIR_W), lambda i: (0, i, 0)),
        out_shape=jax.ShapeDtypeStruct((PAIRS, rows, PAIR_W), _F32),
        scratch_shapes=[
            pltpu.VMEM((rt // SUBLANES, SUBLANES, 2 * HALF), _F32),
            pltpu.VMEM((SUBLANES, 2 * HALF), _F32),
        ],
        compiler_params=pltpu.CompilerParams(dimension_semantics=("arbitrary",),
                                             vmem_limit_bytes=VMEM_LIMIT),
        name="s5_core",
    )(u, bs, cs, ds, a8)


def _mixer_kernel(first, *refs):
    x, rest = _stream_tile(first, refs)
    (y5_ref, gpre_ref, gpost_ref, win_ref, wgr_ref, bgc_ref, bgr_ref, wcv_ref,
     ghead_ref, wglu_ref, wa_ref, wb_ref,
     wout_ref, o_ref, st_ref, m_ref, yb_ref, y5s_ref, q_ref, k_ref, v_ref,
     so_ref, sga_ref, sgb_ref, *cv_refs) = rest
    pid = pl.program_id(0)

    @pl.when(pid == 0)
    def _():
        for cv_ref in cv_refs:
            cv_ref[0:SUBLANES, :] = jnp.zeros((SUBLANES, MIX_BLOCK), _F32)
        st_ref[...] = jnp.zeros_like(st_ref)
        m_ref[...] = jnp.zeros_like(m_ref)
        for h in range(MLSTM_HEADS):
            v_ref[h, :, MLSTM_HEAD_DIM:] = jnp.ones((TL, MLSTM_HEAD_DIM), _BF16)

    hn = _rms(x, gpre_ref[...]).astype(_BF16)

    row_c = pid * TL + lax.broadcasted_iota(jnp.int32, (TL, 1), 0)
    valid_c = row_c >= FRONT_PAD
    row_r = pid * TL + lax.broadcasted_iota(jnp.int32, (1, TL), 1)
    valid_r = row_r >= FRONT_PAD

    n_blocks = 2 * MLSTM_WIDTH // MIX_BLOCK

    def proj_block(b):
        cols = slice(b * MIX_BLOCK, (b + 1) * MIX_BLOCK)
        cv_refs[b][SUBLANES:SUBLANES + TL, :] = _dot(hn, win_ref[:, cols])

    def conv_block(b):
        cols = slice(b * MIX_BLOCK, (b + 1) * MIX_BLOCK)
        cv_ref = cv_refs[b]
        acc = wcv_ref[QK_CONV:QK_CONV + 1, cols]
        for j in range(QK_CONV):
            off = SUBLANES - (QK_CONV - 1) + j
            acc = acc + wcv_ref[j:j + 1, cols] * cv_ref[off:off + TL, :]
        cv_ref[0:SUBLANES, :] = cv_ref[TL:TL + SUBLANES, :]
        qk = jnp.where(valid_c, acc * _sigmoid(acc), 0.0)
        if b < MLSTM_WIDTH // MIX_BLOCK:
            q_ref[:, cols] = qk.astype(_BF16)
        else:
            kcols = slice(b * MIX_BLOCK - MLSTM_WIDTH, (b + 1) * MIX_BLOCK - MLSTM_WIDTH)
            k_ref[:, kcols] = qk * (MLSTM_HEAD_DIM ** -0.5)

    gate_w = D_MODEL // n_blocks

    def branch_gate_block(dst_ref, base, b):
        dst_ref[:, b * gate_w:(b + 1) * gate_w] = _sigmoid(
            _dot(hn, win_ref[:, base + b * gate_w:base + (b + 1) * gate_w]))

    q_blocks = MLSTM_WIDTH // MIX_BLOCK
    for b in range(q_blocks, n_blocks):
        proj_block(b)
    v_all = jnp.where(valid_c, _dot(hn, win_ref[:, WIN_V:WIN_O]), 0.0).astype(_BF16)
    for h in range(MLSTM_HEADS):
        v_ref[h, :, :MLSTM_HEAD_DIM] = v_all[:, h * MLSTM_HEAD_DIM:(h + 1) * MLSTM_HEAD_DIM]
    lane = lax.broadcasted_iota(jnp.int32, (1, GATE_PAD), 1)
    gc = _dot(hn, win_ref[:, WIN_GATES:]) + bgc_ref[...]
    gc = jnp.where(lane < MLSTM_HEADS, gc, _log_sigmoid(gc))
    gc = jnp.where(valid_c, gc, jnp.where(lane < MLSTM_HEADS, PAD_LOG_INPUT_GATE, 0.0))
    sub = lax.broadcasted_iota(jnp.int32, (2 * MLSTM_HEADS, 1), 0)
    gr = lax.dot_general(wgr_ref[...], hn, (((1,), (1,)), ((), ())),
                         preferred_element_type=_F32) + bgr_ref[:, 0:1]
    gr = jnp.where(sub < MLSTM_HEADS, gr, _log_sigmoid(gr))
    gr = jnp.where(valid_r, gr, jnp.where(sub < MLSTM_HEADS, PAD_LOG_INPUT_GATE, 0.0))
    for b in range(q_blocks):
        conv_block(q_blocks + b)
        proj_block(b)

    it = lax.broadcasted_iota(jnp.int32, (CH, CH), 0)
    js = lax.broadcasted_iota(jnp.int32, (CH, CH), 1)
    causal = it >= js
    tri_l = causal.astype(_F32)
    tri_u = (it <= js).astype(_F32)

    def regroup_s5(j):
        for t in range(SSM_WIDTH // LANES):
            yj = jnp.concatenate([y5_ref[t * PAIRS_PER_TILE + k, :, j * PAIR_IN:(j + 1) * PAIR_IN]
                                  for k in range(PAIRS_PER_TILE)], axis=-1)
            y5s_ref[t, pl.ds(j, TL // SS, stride=SS), :] = yj

    n_chunks = TL // CH
    hp = lax.Precision.HIGHEST
    rows = [slice(c * CH, (c + 1) * CH) for c in range(n_chunks)]
    heads = [slice(h * MLSTM_HEAD_DIM, (h + 1) * MLSTM_HEAD_DIM) for h in range(MLSTM_HEADS)]

    gcc = [gc[rows[c], :] for c in range(n_chunks)]
    b_c = [jnp.dot(tri_l, gcc[c], preferred_element_type=_F32, precision=hp) for c in range(n_chunks)]
    b_r = [jnp.dot(gr[:, rows[c]], tri_u, preferred_element_type=_F32, precision=hp) for c in range(n_chunks)]

    rrow = [[None] * MLSTM_HEADS for _ in range(n_chunks)]
    m_in = [[None] * MLSTM_HEADS for _ in range(n_chunks)]
    m_out = [[None] * MLSTM_HEADS for _ in range(n_chunks)]
    for h in range(MLSTM_HEADS):
        fl = MLSTM_HEADS + h
        m = m_ref[h:h + 1, :]
        for c in range(n_chunks):
            rrow[c][h] = gr[h:h + 1, rows[c]] - b_r[c][fl:fl + 1, :]
            m_in[c][h] = m
            m_out[c][h] = jnp.maximum(m, jnp.max(rrow[c][h], axis=-1, keepdims=True))
            m = b_r[c][fl:fl + 1, CH - 1:CH] + m_out[c][h]
        m_ref[h:h + 1, :] = m

    st_in = [[None] * MLSTM_HEADS for _ in range(n_chunks)]
    for h in range(MLSTM_HEADS):
        fl = MLSTM_HEADS + h
        st = st_ref[h]
        for c in range(n_chunks):
            st_in[c][h] = st
            rcol = gcc[c][:, h:h + 1] - b_c[c][:, fl:fl + 1]
            kw = k_ref[rows[c], heads[h]] * jnp.exp(rcol - m_out[c][h][:, 0:1])
            decay = jnp.exp(m_in[c][h] - m_out[c][h])
            st = (jnp.concatenate([decay, decay], axis=-1) * st
                  + _dot(kw.T.astype(_BF16), v_ref[h, rows[c], :]))
        st_ref[h] = st

    for b in range(q_blocks):
        conv_block(b)
    so_ref[...] = _sigmoid(_dot(hn, win_ref[:, WIN_O:WIN_GA]))

    for c0 in range(0, n_chunks, MLSTM_STAGE_CHUNKS):
        group = range(c0, c0 + MLSTM_STAGE_CHUNKS)
        for c in group:
            branch_gate_block(sga_ref, WIN_GA, c)
            branch_gate_block(sgb_ref, WIN_GB, c)
            for j in range(c * SS // n_chunks, (c + 1) * SS // n_chunks):
                regroup_s5(j)
        items = [(c, h) for c in group for h in range(MLSTM_HEADS)]
        qc = [q_ref[rows[c], heads[h]] for c, h in items]
        qk = [lax.dot_general(qc[i], k_ref[rows[c], heads[h]].astype(_BF16), (((1,), (1,)), ((), ())),
                              preferred_element_type=_F32) for i, (c, h) in enumerate(items)]
        qs = [_dot(qc[i], st_in[c][h].astype(_BF16)) for i, (c, h) in enumerate(items)]
        mc1, mcol, s = [], [], []
        for i, (c, h) in enumerate(items):
            rmat = jnp.where(causal, rrow[c][h], -jnp.inf)
            mc1.append(jnp.maximum(m_in[c][h][:, 0:1], jnp.max(rmat, axis=-1, keepdims=True)))
            mcol.append(jnp.broadcast_to(mc1[i], (CH, LANES)))
            s.append((qk[i] * jnp.exp(rmat - mcol[i])).astype(_BF16))
        sv = [_dot(s[i], v_ref[h, rows[c], :]) for i, (c, h) in enumerate(items)]
        for i, (c, h) in enumerate(items):
            hs = heads[h]
            bcol = b_c[c][:, MLSTM_HEADS + h:MLSTM_HEADS + h + 1]
            floor = jnp.broadcast_to(jnp.exp(-(bcol + mc1[i])), (CH, LANES))
            w_inter = jnp.exp(m_in[c][h] - mcol[i])
            num = sv[i][:, :MLSTM_HEAD_DIM] + w_inter * qs[i][:, :MLSTM_HEAD_DIM]
            den = sv[i][:, MLSTM_HEAD_DIM:] + w_inter * qs[i][:, MLSTM_HEAD_DIM:]
            hout = num / jnp.maximum(jnp.abs(den), floor)
            hout = hout * lax.rsqrt(jnp.mean(hout * hout, axis=-1, keepdims=True) + NORM_EPS)
            hout = hout * ghead_ref[:, hs] * so_ref[rows[c], hs]
            yb_ref[rows[c], hs] = hout.astype(_BF16)

    ya = _gelu_tanh(jnp.concatenate([y5s_ref[t] for t in range(SSM_WIDTH // LANES)], axis=-1))
    ya = ya * _sigmoid(_dot(ya.astype(_BF16), wglu_ref[...]))

    merged = (sga_ref[...] * _dot(ya.astype(_BF16), wa_ref[...])
              + sgb_ref[...] * _dot(yb_ref[...], wb_ref[...]))
    o_ref[...] = x + _rms(_dot(merged.astype(_BF16), wout_ref[...]), gpost_ref[...])


def _mixer(stream, y5, consts, l, lp):
    first = len(stream) == 2
    return pl.pallas_call(
        functools.partial(_mixer_kernel, first),
        grid=(lp // TL,),
        in_specs=_stream_specs(first) + [
            pl.BlockSpec((PAIRS, TL // SS, PAIR_W), lambda i: (0, i, 0)),
        ] + [_layer_spec(c.shape, l) for c in consts],
        out_specs=pl.BlockSpec((TL, D_MODEL), lambda i: (i, 0)),
        out_shape=jax.ShapeDtypeStruct((lp, D_MODEL), _F32),
        scratch_shapes=[
            pltpu.VMEM((MLSTM_HEADS, MLSTM_HEAD_DIM, 2 * MLSTM_HEAD_DIM), _F32),
            pltpu.VMEM((SUBLANES, LANES), _F32),
            pltpu.VMEM((TL, MLSTM_WIDTH), _BF16),
            pltpu.VMEM((SSM_WIDTH // LANES, TL, LANES), _F32),
            pltpu.VMEM((TL, MLSTM_WIDTH), _BF16),
            pltpu.VMEM((TL, MLSTM_WIDTH), _F32),
            pltpu.VMEM((MLSTM_HEADS, TL, 2 * MLSTM_HEAD_DIM), _BF16),
            pltpu.VMEM((TL, MLSTM_WIDTH), _F32),
            pltpu.VMEM((TL, D_MODEL), _F32),
            pltpu.VMEM((TL, D_MODEL), _F32),
        ] + [pltpu.VMEM((TL + SUBLANES, MIX_BLOCK), _F32)
             for _ in range(2 * MLSTM_WIDTH // MIX_BLOCK)],
        compiler_params=pltpu.CompilerParams(dimension_semantics=("arbitrary",),
                                             vmem_limit_bytes=VMEM_LIMIT),
        name="mixer",
    )(*stream, y5, *consts)


def _ffn_kernel(fuse_next, n_tiles, *refs):
    if fuse_next:
        (x_ref, gpre_ref, gpost_ref, wg_ref, wu_ref, wd_ref, wcv_ref, gnext_ref, wnext_ref,
         o_ref, u_ref, cv_ref, act_ref, prev_ref, us_ref) = refs
    else:
        x_ref, gpre_ref, gpost_ref, wg_ref, wu_ref, wd_ref, wcv_ref, o_ref, cv_ref, act_ref = refs
    pid = pl.program_id(0)

    @pl.when(pid == 0)
    def _():
        cv_ref[0:SUBLANES, :] = jnp.zeros((SUBLANES, FFN_DIM), _F32)
        if fuse_next:
            prev_ref[...] = jnp.zeros_like(prev_ref)

    def tile():
        if fuse_next:
            _s5_project(prev_ref[...], gnext_ref, wnext_ref, u_ref, us_ref)
        out = _ffn_tile(x_ref[...], gpre_ref, gpost_ref, wg_ref, wu_ref, wd_ref, wcv_ref, cv_ref, act_ref)
        o_ref[...] = out
        if fuse_next:
            prev_ref[...] = out

    if fuse_next:
        pl.when(pid < n_tiles)(tile)

        @pl.when(pid == n_tiles)
        def _():
            _s5_project(prev_ref[...], gnext_ref, wnext_ref, u_ref, us_ref)
    else:
        tile()


def _ffn_tile(x, gpre_ref, gpost_ref, wg_ref, wu_ref, wd_ref, wcv_ref, cv_ref, act_ref):
    hn = _rms(x, gpre_ref[...]).astype(_BF16)
    chunk = lambda c: slice(c * FFN_CHUNK, (c + 1) * FFN_CHUNK)
    acc = None
    group_start = 0
    cv_ref[SUBLANES:SUBLANES + TL, chunk(0)] = _dot(hn, wg_ref[:, chunk(0)])
    up_next = _dot(hn, wu_ref[:, chunk(0)])
    for c in range(N_FFN_CHUNKS):
        cs = chunk(c)
        up = up_next
        if c + 1 < N_FFN_CHUNKS:
            cv_ref[SUBLANES:SUBLANES + TL, chunk(c + 1)] = _dot(hn, wg_ref[:, chunk(c + 1)])
            up_next = _dot(hn, wu_ref[:, chunk(c + 1)])
        conv = wcv_ref[FFN_CONV:FFN_CONV + 1, cs]
        for j in range(FFN_CONV):
            off = SUBLANES - (FFN_CONV - 1) + j
            conv = conv + wcv_ref[j:j + 1, cs] * cv_ref[off:off + TL, cs]
        cv_ref[0:SUBLANES, cs] = cv_ref[TL:TL + SUBLANES, cs]
        act_ref[:, cs] = (_gelu_tanh(conv) * up).astype(_BF16)
        if (c + 1) % FFN_DOWN_GROUP == 0 or c + 1 == N_FFN_CHUNKS:
            gs = slice(group_start * FFN_CHUNK, (c + 1) * FFN_CHUNK)
            part = _dot(act_ref[:, gs], wd_ref[gs, :])
            acc = part if acc is None else acc + part
            group_start = c + 1
    return x + _rms(acc, gpost_ref[...])


def _ffn(x, consts, l, next_s5=None):
    lp = x.shape[0]
    n_tiles = lp // TL
    scratch = [pltpu.VMEM((TL + SUBLANES, FFN_DIM), _F32),
               pltpu.VMEM((TL, FFN_DIM), _BF16)]
    in_specs = [_layer_spec(c.shape, l) for c in consts]
    if next_s5 is None:
        grid = n_tiles
        x_spec = pl.BlockSpec((TL, D_MODEL), lambda i: (i, 0))
        out_specs = pl.BlockSpec((TL, D_MODEL), lambda i: (jnp.maximum(i - 1, 0), 0))
        out_shape = jax.ShapeDtypeStruct((lp - TL, D_MODEL), _F32)
        args = (x, *consts)
    else:
        grid = n_tiles + 1
        x_spec = pl.BlockSpec((TL, D_MODEL), lambda i: (jnp.minimum(i, n_tiles - 1), 0))
        out_specs = [pl.BlockSpec((TL, D_MODEL), lambda i: (jnp.minimum(i, n_tiles - 1), 0)),
                     pl.BlockSpec((PAIRS, TL // SS, PAIR_W), lambda i: (0, jnp.maximum(i - 1, 0), 0))]
        out_shape = [jax.ShapeDtypeStruct((lp, D_MODEL), _F32),
                     jax.ShapeDtypeStruct((PAIRS, lp // SS, PAIR_W), _BF16)]
        in_specs += [_layer_spec(a.shape, l + 1) for a in next_s5]
        scratch += [pltpu.VMEM((TL, D_MODEL), _F32),
                    pltpu.VMEM((SSM_WIDTH // LANES, TL, LANES), _F32)]
        args = (x, *consts, *next_s5)
    return pl.pallas_call(
        functools.partial(_ffn_kernel, next_s5 is not None, n_tiles),
        grid=(grid,),
        in_specs=[x_spec] + in_specs,
        out_specs=out_specs,
        out_shape=out_shape,
        scratch_shapes=scratch,
        compiler_params=pltpu.CompilerParams(dimension_semantics=("arbitrary",),
                                             vmem_limit_bytes=VMEM_LIMIT),
        name="ffn",
    )(*args)


def _regroup_w_in_kernel(wt_ref, wu_ref, wmix_ref, wgr_ref):
    piece = SSM_WIDTH

    def rows_to_cols(src):
        return wt_ref[src:src + piece, :].T.astype(_BF16)

    wu_ref[...] = rows_to_cols(0)
    tail = W_IN_GATES + 2 * MLSTM_HEADS
    for k in range(WIN_GA // piece):
        wmix_ref[:, k * piece:(k + 1) * piece] = rows_to_cols(SSM_WIDTH + k * piece)
    for k in range(2 * D_MODEL // piece):
        wmix_ref[:, WIN_GA + k * piece:WIN_GA + (k + 1) * piece] = rows_to_cols(tail + k * piece)
    rb = wt_ref.shape[1]
    lane = lax.broadcasted_iota(jnp.int32, (rb, GATE_PAD), 1)
    gates = wt_ref[W_IN_GATES:W_IN_GATES + GATE_PAD, :].T
    wmix_ref[:, WIN_GATES:] = jnp.where(lane < 2 * MLSTM_HEADS, gates, 0.0).astype(_BF16)
    wgr_ref[...] = wt_ref[W_IN_GATES:W_IN_GATES + 2 * MLSTM_HEADS, :].astype(_BF16)


def _regroup_w_in(w_in):
    depth = w_in.shape[0]
    assert w_in.shape[1:] == (D_MODEL, W_IN_COLS)
    return pl.pallas_call(
        _regroup_w_in_kernel,
        grid=(depth, D_MODEL // REGROUP_ROWS),
        in_specs=[pl.BlockSpec((None, W_IN_COLS, REGROUP_ROWS), lambda l, r: (l, 0, r))],
        out_specs=[pl.BlockSpec((None, REGROUP_ROWS, SSM_WIDTH), lambda l, r: (l, r, 0)),
                   pl.BlockSpec((None, REGROUP_ROWS, WIN_COLS), lambda l, r: (l, r, 0)),
                   pl.BlockSpec((None, 2 * MLSTM_HEADS, REGROUP_ROWS), lambda l, r: (l, 0, r))],
        out_shape=[jax.ShapeDtypeStruct((depth, D_MODEL, SSM_WIDTH), _BF16),
                   jax.ShapeDtypeStruct((depth, D_MODEL, WIN_COLS), _BF16),
                   jax.ShapeDtypeStruct((depth, 2 * MLSTM_HEADS, D_MODEL), _BF16)],
        compiler_params=pltpu.CompilerParams(dimension_semantics=("parallel", "parallel"),
                                             vmem_limit_bytes=VMEM_LIMIT),
        name="regroup_w_in",
    )(jnp.swapaxes(w_in, 1, 2))


def _s5_prep(lam_re, lam_im, b_re, b_im, c_re, c_im, d, log_dt):
    hp = lax.Precision.HIGHEST
    gh, npow = SSM_GROUP, SS + 1
    lr = lam_re.reshape(PAIRS, PAIR_STATE)
    li = lam_im.reshape(PAIRS, PAIR_STATE)
    dt = jnp.repeat(jnp.exp(log_dt), SSM_STATE).reshape(PAIRS, PAIR_STATE)
    ks = jnp.arange(npow, dtype=_F32)[None, :, None]
    mag = jnp.exp((lr * dt)[:, None, :] * ks)
    ang = (li * dt)[:, None, :] * ks
    ak_re, ak_im = mag * jnp.cos(ang), mag * jnp.sin(ang)
    nr, ni = ak_re[:, 1] - 1.0, ak_im[:, 1]
    den = lr * lr + li * li
    z_re = (nr * lr + ni * li) / den
    z_im = (ni * lr - nr * li) / den
    bt_re = jnp.swapaxes(b_re.reshape(PAIRS, PAIR_STATE, gh), 1, 2)
    bt_im = jnp.swapaxes(b_im.reshape(PAIRS, PAIR_STATE, gh), 1, 2)
    bb_re = z_re[:, None, :] * bt_re - z_im[:, None, :] * bt_im
    bb_im = z_re[:, None, :] * bt_im + z_im[:, None, :] * bt_re
    as_hn = lambda c: c.reshape(PAIRS, 2, gh, SSM_STATE).transpose(0, 2, 1, 3).reshape(PAIRS, gh, PAIR_STATE)
    ct_re, ct_im = as_hn(c_re), as_hn(c_im)
    same_group = (jnp.arange(2)[:, None] == (jnp.arange(PAIR_STATE) // SSM_STATE)[None, :]).astype(_F32)
    same_group = same_group[None, None, :, None, :]

    def times_powers(m_re, m_im, pw_re, pw_im):
        pr, pi = pw_re[:, :, None, None, :], pw_im[:, :, None, None, :]
        mr, mi = m_re[:, None, None, :, :], m_im[:, None, None, :, :]
        shape = (PAIRS, npow * PAIR_IN, PAIR_STATE)
        return (((pr * mr - pi * mi) * same_group).reshape(shape),
                ((pr * mi + pi * mr) * same_group).reshape(shape))

    wk_re, wk_im = times_powers(bb_re, bb_im, ak_re[:, ::-1], ak_im[:, ::-1])
    bs = jnp.concatenate([wk_re[:, PAIR_IN:], wk_im[:, PAIR_IN:]], axis=-1)

    ca_re, ca_im = times_powers(ct_re, ct_im, ak_re, ak_im)
    cs = jnp.concatenate([ca_re[:, PAIR_IN:], -ca_im[:, PAIR_IN:]], axis=-1)

    bbm_re, bbm_im = wk_re[:, SS * PAIR_IN:], wk_im[:, SS * PAIR_IN:]
    kt = (jnp.einsum('qrn,qcn->qrc', bbm_re, ca_re[:, :PAIR_W], precision=hp)
          - jnp.einsum('qrn,qcn->qrc', bbm_im, ca_im[:, :PAIR_W], precision=hp))
    ds = jnp.stack([jnp.pad(kt[..., :PAIR_W - PAIR_IN * i], ((0, 0), (0, 0), (PAIR_IN * i, 0)))
                    for i in range(SS)], axis=1).reshape(PAIRS, PAIR_W, PAIR_W)
    skip = jnp.tile(d.reshape(PAIRS, PAIR_IN), (1, SS))
    ds = ds + skip[:, None, :] * jnp.eye(PAIR_W, dtype=_F32)[None]

    ks8 = SS * jnp.arange(SUBLANES + 1, dtype=_F32)
    mag8 = jnp.exp((lr * dt).reshape(HALF)[None, :] * ks8[:, None])
    ang8 = (li * dt).reshape(HALF)[None, :] * ks8[:, None]
    apow = jnp.concatenate([mag8 * jnp.cos(ang8), mag8 * jnp.sin(ang8)], axis=1)
    sub = jnp.arange(SUBLANES)
    shifted = [jnp.where((sub >= k)[:, None], apow[k][None, :], 0.0) for k in (1, 2, 4)]
    a8 = jnp.stack(shifted + [apow[:SUBLANES], jnp.broadcast_to(apow[SUBLANES][None], (SUBLANES, 2 * HALF))])
    return bs.astype(_BF16), cs.astype(_BF16), ds.astype(_BF16), a8


def kernel(x, meta_tokens, g_mix_pre, g_mix_post, w_in, b_gates, ssm_lambda_re, ssm_lambda_im, ssm_b_re,
           ssm_b_im, ssm_c_re, ssm_c_im, ssm_d, ssm_log_dt, w_ssm_glu, w_qk_conv, b_qk_conv, g_head_norm,
           w_branch_ssm, w_branch_mlstm, w_out, g_ffn_pre, g_ffn_post, w_ffn_gate, w_ffn_up, w_ffn_conv,
           b_ffn_conv, w_ffn_down):
    bsz, seq, _ = x.shape
    assert bsz == 1
    depth = w_in.shape[0]
    assert seq % TL == 0 and FRONT_PAD + N_META == TL
    lp = TL + seq
    assert (lp // SS) % SUBLANES == 0

    head = jnp.concatenate([jnp.zeros((FRONT_PAD, D_MODEL), _F32), meta_tokens.astype(_F32)], axis=0)
    stream = (head, x[0])

    nh2 = 2 * MLSTM_HEADS
    row = lambda a: a[:, None, :]
    w_u, w_mix, w_gr = _regroup_w_in(w_in)
    b_gc = row(jnp.pad(b_gates, ((0, 0), (0, GATE_PAD - nh2))))
    b_gr = jnp.broadcast_to(b_gates[:, :, None], (depth, nh2, LANES))
    w_cv = jnp.concatenate([w_qk_conv, row(b_qk_conv),
                            jnp.zeros((depth, SUBLANES - QK_CONV - 1, 2 * MLSTM_WIDTH), _F32)], axis=1)
    mixer_consts = (row(g_mix_pre), row(g_mix_post), w_mix, w_gr, b_gc, b_gr, w_cv,
                    row(g_head_norm), w_ssm_glu.astype(_BF16), w_branch_ssm.astype(_BF16),
                    w_branch_mlstm.astype(_BF16), w_out.astype(_BF16))
    f_cv = jnp.concatenate([w_ffn_conv, row(b_ffn_conv),
                            jnp.zeros((depth, SUBLANES - FFN_CONV - 1, FFN_DIM), _F32)], axis=1)
    ffn_consts = (row(g_ffn_pre), row(g_ffn_post), w_ffn_gate.astype(_BF16), w_ffn_up.astype(_BF16),
                  w_ffn_down.astype(_BF16), f_cv)
    bs, cs, ds, a8 = jax.vmap(_s5_prep)(ssm_lambda_re, ssm_lambda_im, ssm_b_re, ssm_b_im,
                                        ssm_c_re, ssm_c_im, ssm_d, ssm_log_dt)
    g_pre = row(g_mix_pre)

    u = _s5_in(stream, g_pre, w_u, 0, lp)
    for l in range(depth):
        y5 = _s5_core(u, bs, cs, ds, a8, l)
        h = _mixer(stream, y5, mixer_consts, l, lp)
        if l + 1 < depth:
            h, u = _ffn(h, ffn_consts, l, next_s5=(g_pre, w_u))
        else:
            h = _ffn(h, ffn_consts, l)
        stream = (h,)
    return h[None]
```

```python
import functools
import math

import jax
import jax.numpy as jnp
from jax import lax
from jax.experimental import pallas as pl
from jax.experimental.pallas import tpu as pltpu

D_MODEL = 1024
N_META = 16
SSM_WIDTH = 512
SSM_GROUP = 16
SSM_GROUPS = 32
SSM_STATE = 64
MLSTM_WIDTH = 512
MLSTM_HEADS = 4
MLSTM_HEAD_DIM = 128
QK_CONV = 4
FFN_DIM = 2816
FFN_CONV = 3
NORM_EPS = 1e-6
PAD_LOG_INPUT_GATE = -1e4

LANES = 128
SUBLANES = 8
TL = 512
CH = 128
SS = 8
PAIRS = SSM_GROUPS // 2
PAIR_IN = 2 * SSM_GROUP
PAIR_W = SS * PAIR_IN
PAIR_STATE = 2 * SSM_STATE
PAIRS_PER_TILE = LANES // PAIR_IN
HALF = PAIRS * PAIR_STATE
FRONT_PAD = TL - N_META
MIX_BLOCK = 256
MLSTM_STAGE_CHUNKS = 1
S5_ROW_BLOCK = 144
FFN_CHUNK = 256
N_FFN_CHUNKS = FFN_DIM // FFN_CHUNK
FFN_DOWN_GROUP = 11
GATE_PAD = LANES
WIN_V = 2 * MLSTM_WIDTH
WIN_O = WIN_V + MLSTM_WIDTH
WIN_GA = WIN_O + MLSTM_WIDTH
WIN_GB = WIN_GA + D_MODEL
WIN_GATES = WIN_GB + D_MODEL
WIN_COLS = WIN_GATES + GATE_PAD
W_IN_COLS = SSM_WIDTH + 4 * MLSTM_WIDTH + 2 * MLSTM_HEADS + 2 * D_MODEL
W_IN_GATES = SSM_WIDTH + 4 * MLSTM_WIDTH
REGROUP_ROWS = 256
VMEM_LIMIT = 56 * 1024 * 1024

_BF16 = jnp.bfloat16
_F32 = jnp.float32


def _rms(x, g):
    return x * lax.rsqrt(jnp.mean(x * x, axis=-1, keepdims=True) + NORM_EPS) * g


def _gelu_tanh(x):
    return 0.5 * x * (1.0 + jnp.tanh(math.sqrt(2.0 / math.pi) * (x + 0.044715 * (x * x * x))))


def _sigmoid(x):
    return 0.5 + 0.5 * jnp.tanh(0.5 * x)


def _log_sigmoid(x):
    return jnp.minimum(x, 0.0) - jnp.log(1.0 + jnp.exp(-jnp.abs(x)))


def _dot(a, b):
    return jnp.dot(a, b, preferred_element_type=_F32)


def _layer_spec(shape, l):
    nd = len(shape) - 1
    return pl.BlockSpec((None,) + tuple(shape[1:]), lambda i, _l=l, _nd=nd: (_l,) + (0,) * _nd)


def _stream_specs(first):
    if first:
        return [pl.BlockSpec((TL, D_MODEL), lambda i: (0, 0)),
                pl.BlockSpec((TL, D_MODEL), lambda i: (jnp.maximum(i - 1, 0), 0))]
    return [pl.BlockSpec((TL, D_MODEL), lambda i: (i, 0))]


def _stream_tile(first, refs):
    if first:
        head_ref, x_ref, *rest = refs
        return jnp.where(pl.program_id(0) == 0, head_ref[...], x_ref[...]), rest
    x_ref, *rest = refs
    return x_ref[...], rest


def _s5_in_kernel(first, *refs):
    x, (g_ref, w_ref, u_ref, us_ref) = _stream_tile(first, refs)
    _s5_project(x, g_ref, w_ref, u_ref, us_ref)


def _s5_project(x, g_ref, w_ref, u_ref, us_ref):
    u = _dot(_rms(x, g_ref[...]).astype(_BF16), w_ref[...])
    for t in range(SSM_WIDTH // LANES):
        us_ref[t] = u[:, t * LANES:(t + 1) * LANES]
    for j in range(SS):
        for t in range(SSM_WIDTH // LANES):
            rows = us_ref[t, pl.ds(j, TL // SS, stride=SS), :].astype(_BF16)
            for k in range(PAIRS_PER_TILE):
                u_ref[t * PAIRS_PER_TILE + k, :, j * PAIR_IN:(j + 1) * PAIR_IN] = (
                    rows[:, k * PAIR_IN:(k + 1) * PAIR_IN])


def _s5_in(stream, g, w, l, lp):
    first = len(stream) == 2
    return pl.pallas_call(
        functools.partial(_s5_in_kernel, first),
        grid=(lp // TL,),
        in_specs=_stream_specs(first) + [_layer_spec(g.shape, l), _layer_spec(w.shape, l)],
        out_specs=pl.BlockSpec((PAIRS, TL // SS, PAIR_W), lambda i: (0, i, 0)),
        out_shape=jax.ShapeDtypeStruct((PAIRS, lp // SS, PAIR_W), _BF16),
        scratch_shapes=[pltpu.VMEM((SSM_WIDTH // LANES, TL, LANES), _F32)],
        compiler_params=pltpu.CompilerParams(dimension_semantics=("parallel",),
                                             vmem_limit_bytes=VMEM_LIMIT),
        name="s5_in",
    )(*stream, g, w)


def _s5_core_kernel(rt, u_ref, bs_ref, cs_ref, ds_ref, a_ref, y_ref, vx_ref, st_ref):
    @pl.when(pl.program_id(0) == 0)
    def _():
        st_ref[...] = jnp.zeros_like(st_ref)

    bounds = list(range(0, rt, S5_ROW_BLOCK)) + [rt]
    blocks = [(bounds[k], bounds[k + 1]) for k in range(len(bounds) - 1)]

    def state_input(r0, r1):
        g0, g1 = r0 // SUBLANES, r1 // SUBLANES
        for p in range(PAIRS):
            v = _dot(u_ref[p, r0:r1, :], bs_ref[p])
            vx_ref[g0:g1, :, p * PAIR_STATE:(p + 1) * PAIR_STATE] = (
                v[:, :PAIR_STATE].reshape(g1 - g0, SUBLANES, PAIR_STATE))
            vx_ref[g0:g1, :, HALF + p * PAIR_STATE:HALF + (p + 1) * PAIR_STATE] = (
                v[:, PAIR_STATE:].reshape(g1 - g0, SUBLANES, PAIR_STATE))

    def output(r0, r1):
        g0, g1 = r0 // SUBLANES, r1 // SUBLANES
        for p in range(PAIRS):
            x_re = vx_ref[g0:g1, :, p * PAIR_STATE:(p + 1) * PAIR_STATE].reshape(r1 - r0, PAIR_STATE)
            x_im = vx_ref[g0:g1, :, HALF + p * PAIR_STATE:HALF + (p + 1) * PAIR_STATE].reshape(r1 - r0, PAIR_STATE)
            xp = jnp.concatenate([x_re, x_im], axis=-1).astype(_BF16)
            y_ref[p, r0:r1, :] = (
                lax.dot_general(xp, cs_ref[p], (((1,), (1,)), ((), ())), preferred_element_type=_F32)
                + _dot(u_ref[p, r0:r1, :], ds_ref[p]))

    def cmul(t, z_re, z_im):
        t_re, t_im = a_ref[t, :, :HALF], a_ref[t, :, HALF:]
        return t_re * z_re - t_im * z_im, t_re * z_im + t_im * z_re

    first = lax.broadcasted_iota(jnp.int32, (SUBLANES, HALF), 0) == 0

    def body(i, carry):
        s_re, s_im = carry
        w_re = vx_ref[i, :, :HALF]
        w_im = vx_ref[i, :, HALF:]
        for t, shift in enumerate((1, 2, 4)):
            d_re, d_im = cmul(t, pltpu.roll(w_re, shift, 0), pltpu.roll(w_im, shift, 0))
            w_re, w_im = w_re + d_re, w_im + d_im
        x_re, x_im = cmul(3, s_re, s_im)
        vx_ref[i, :, :HALF] = x_re + jnp.where(first, 0.0, pltpu.roll(w_re, 1, 0))
        vx_ref[i, :, HALF:] = x_im + jnp.where(first, 0.0, pltpu.roll(w_im, 1, 0))
        n_re, n_im = cmul(4, s_re, s_im)
        last = slice(SUBLANES - 1, SUBLANES)
        return (n_re + jnp.broadcast_to(w_re[last], (SUBLANES, HALF)),
                n_im + jnp.broadcast_to(w_im[last], (SUBLANES, HALF)))

    carry = (st_ref[:, :HALF], st_ref[:, HALF:])
    state_input(*blocks[0])
    for k, (r0, r1) in enumerate(blocks):
        if k + 1 < len(blocks):
            state_input(*blocks[k + 1])
        for i in range(r0 // SUBLANES, r1 // SUBLANES):
            carry = body(i, carry)
        output(r0, r1)
    st_ref[:, :HALF] = carry[0]
    st_ref[:, HALF:] = carry[1]


def _s5_core(u, bs, cs, ds, a8, l):
    rows = u.shape[1]
    rt = rows // 8 if (rows // 8) % SUBLANES == 0 else rows
    return pl.pallas_call(
        functools.partial(_s5_core_kernel, rt),
        grid=(rows // rt,),
        in_specs=[
            pl.BlockSpec((PAIRS, rt, PAIR_W), lambda i: (0, i, 0)),
            _layer_spec(bs.shape, l), _layer_spec(cs.shape, l), _layer_spec(ds.shape, l),
            _layer_spec(a8.shape, l),
        ],
        out_specs=pl.BlockSpec((PAIRS, rt, PAIR_W), lambda i: (0, i, 0)),
        out_shape=jax.ShapeDtypeStruct((PAIRS, rows, PAIR_W), _F32),
        scratch_shapes=[
            pltpu.VMEM((rt // SUBLANES, SUBLANES, 2 * HALF), _F32),
            pltpu.VMEM((SUBLANES, 2 * HALF), _F32),
        ],
        compiler_params=pltpu.CompilerParams(dimension_semantics=("arbitrary",),
                                             vmem_limit_bytes=VMEM_LIMIT),
        name="s5_core",
    )(u, bs, cs, ds, a8)


def _mixer_kernel(first, *refs):
    x, rest = _stream_tile(first, refs)
    (y5_ref, gpre_ref, gpost_ref, win_ref, wgr_ref, bgc_ref, bgr_ref, wcv_ref,
     ghead_ref, wglu_ref, wa_ref, wb_ref,
     wout_ref, o_ref, st_ref, m_ref, yb_ref, y5s_ref, q_ref, k_ref, v_ref,
     so_ref, sga_ref, sgb_ref, hn_ref, xs_ref, *cv_refs) = rest
    pid = pl.program_id(0)

    @pl.when(pid == 0)
    def _():
        for cv_ref in cv_refs:
            cv_ref[0:SUBLANES, :] = jnp.zeros((SUBLANES, MIX_BLOCK), _F32)
        st_ref[...] = jnp.zeros_like(st_ref)
        m_ref[...] = jnp.zeros_like(m_ref)
        for h in range(MLSTM_HEADS):
            v_ref[h, :, MLSTM_HEAD_DIM:] = jnp.ones((TL, MLSTM_HEAD_DIM), _BF16)

    xs_ref[...] = x
    hn_ref[...] = _rms(x, gpre_ref[...]).astype(_BF16)

    row_c = pid * TL + lax.broadcasted_iota(jnp.int32, (TL, 1), 0)
    valid_c = row_c >= FRONT_PAD
    row_r = pid * TL + lax.broadcasted_iota(jnp.int32, (1, TL), 1)
    valid_r = row_r >= FRONT_PAD

    n_blocks = 2 * MLSTM_WIDTH // MIX_BLOCK

    def proj_block(b):
        cols = slice(b * MIX_BLOCK, (b + 1) * MIX_BLOCK)
        cv_refs[b][SUBLANES:SUBLANES + TL, :] = _dot(hn_ref[...],win_ref[:, cols])

    def conv_block(b):
        cols = slice(b * MIX_BLOCK, (b + 1) * MIX_BLOCK)
        cv_ref = cv_refs[b]
        acc = wcv_ref[QK_CONV:QK_CONV + 1, cols]
        for j in range(QK_CONV):
            off = SUBLANES - (QK_CONV - 1) + j
            acc = acc + wcv_ref[j:j + 1, cols] * cv_ref[off:off + TL, :]
        cv_ref[0:SUBLANES, :] = cv_ref[TL:TL + SUBLANES, :]
        qk = jnp.where(valid_c, acc * _sigmoid(acc), 0.0)
        if b < MLSTM_WIDTH // MIX_BLOCK:
            q_ref[:, cols] = qk.astype(_BF16)
        else:
            kcols = slice(b * MIX_BLOCK - MLSTM_WIDTH, (b + 1) * MIX_BLOCK - MLSTM_WIDTH)
            k_ref[:, kcols] = qk * (MLSTM_HEAD_DIM ** -0.5)

    gate_w = D_MODEL // n_blocks

    def branch_gate_block(dst_ref, base, b):
        dst_ref[:, b * gate_w:(b + 1) * gate_w] = _sigmoid(
            _dot(hn_ref[...],win_ref[:, base + b * gate_w:base + (b + 1) * gate_w]))

    q_blocks = MLSTM_WIDTH // MIX_BLOCK
    for b in range(q_blocks, n_blocks):
        proj_block(b)
    v_all = jnp.where(valid_c, _dot(hn_ref[...],win_ref[:, WIN_V:WIN_O]), 0.0).astype(_BF16)
    for h in range(MLSTM_HEADS):
        v_ref[h, :, :MLSTM_HEAD_DIM] = v_all[:, h * MLSTM_HEAD_DIM:(h + 1) * MLSTM_HEAD_DIM]
    lane = lax.broadcasted_iota(jnp.int32, (1, GATE_PAD), 1)
    gc = _dot(hn_ref[...],win_ref[:, WIN_GATES:]) + bgc_ref[...]
    gc = jnp.where(lane < MLSTM_HEADS, gc, _log_sigmoid(gc))
    gc = jnp.where(valid_c, gc, jnp.where(lane < MLSTM_HEADS, PAD_LOG_INPUT_GATE, 0.0))
    sub = lax.broadcasted_iota(jnp.int32, (2 * MLSTM_HEADS, 1), 0)
    gr = lax.dot_general(wgr_ref[...], hn_ref[...], (((1,), (1,)), ((), ())),
                         preferred_element_type=_F32) + bgr_ref[:, 0:1]
    gr = jnp.where(sub < MLSTM_HEADS, gr, _log_sigmoid(gr))
    gr = jnp.where(valid_r, gr, jnp.where(sub < MLSTM_HEADS, PAD_LOG_INPUT_GATE, 0.0))
    for b in range(q_blocks):
        conv_block(q_blocks + b)
        proj_block(b)

    it = lax.broadcasted_iota(jnp.int32, (CH, CH), 0)
    js = lax.broadcasted_iota(jnp.int32, (CH, CH), 1)
    causal = it >= js
    tri_l = causal.astype(_F32)
    tri_u = (it <= js).astype(_F32)

    def regroup_s5(j):
        for t in range(SSM_WIDTH // LANES):
            yj = jnp.concatenate([y5_ref[t * PAIRS_PER_TILE + k, :, j * PAIR_IN:(j + 1) * PAIR_IN]
                                  for k in range(PAIRS_PER_TILE)], axis=-1)
            y5s_ref[t, pl.ds(j, TL // SS, stride=SS), :] = yj

    n_chunks = TL // CH
    hp = lax.Precision.HIGHEST
    rows = [slice(c * CH, (c + 1) * CH) for c in range(n_chunks)]
    heads = [slice(h * MLSTM_HEAD_DIM, (h + 1) * MLSTM_HEAD_DIM) for h in range(MLSTM_HEADS)]

    gcc = [gc[rows[c], :] for c in range(n_chunks)]
    b_c = [jnp.dot(tri_l, gcc[c], preferred_element_type=_F32, precision=hp) for c in range(n_chunks)]
    b_r = [jnp.dot(gr[:, rows[c]], tri_u, preferred_element_type=_F32, precision=hp) for c in range(n_chunks)]

    rrow = [[None] * MLSTM_HEADS for _ in range(n_chunks)]
    m_in = [[None] * MLSTM_HEADS for _ in range(n_chunks)]
    m_out = [[None] * MLSTM_HEADS for _ in range(n_chunks)]
    for h in range(MLSTM_HEADS):
        fl = MLSTM_HEADS + h
        m = m_ref[h:h + 1, :]
        for c in range(n_chunks):
            rrow[c][h] = gr[h:h + 1, rows[c]] - b_r[c][fl:fl + 1, :]
            m_in[c][h] = m
            m_out[c][h] = jnp.maximum(m, jnp.max(rrow[c][h], axis=-1, keepdims=True))
            m = b_r[c][fl:fl + 1, CH - 1:CH] + m_out[c][h]
        m_ref[h:h + 1, :] = m

    st_in = [[None] * MLSTM_HEADS for _ in range(n_chunks)]
    for h in range(MLSTM_HEADS):
        fl = MLSTM_HEADS + h
        st = st_ref[h]
        for c in range(n_chunks):
            st_in[c][h] = st
            rcol = gcc[c][:, h:h + 1] - b_c[c][:, fl:fl + 1]
            kw = k_ref[rows[c], heads[h]] * jnp.exp(rcol - m_out[c][h][:, 0:1])
            decay = jnp.exp(m_in[c][h] - m_out[c][h])
            st = (jnp.concatenate([decay, decay], axis=-1) * st
                  + _dot(kw.T.astype(_BF16), v_ref[h, rows[c], :]))
        st_ref[h] = st

    for b in range(q_blocks):
        conv_block(b)
    so_ref[...] = _sigmoid(_dot(hn_ref[...],win_ref[:, WIN_O:WIN_GA]))

    for c0 in range(0, n_chunks, MLSTM_STAGE_CHUNKS):
        group = range(c0, c0 + MLSTM_STAGE_CHUNKS)
        for c in group:
            branch_gate_block(sga_ref, WIN_GA, c)
            branch_gate_block(sgb_ref, WIN_GB, c)
            for j in range(c * SS // n_chunks, (c + 1) * SS // n_chunks):
                regroup_s5(j)
        items = [(c, h) for c in group for h in range(MLSTM_HEADS)]
        qc = [q_ref[rows[c], heads[h]] for c, h in items]
        qk = [lax.dot_general(qc[i], k_ref[rows[c], heads[h]].astype(_BF16), (((1,), (1,)), ((), ())),
                              preferred_element_type=_F32) for i, (c, h) in enumerate(items)]
        qs = [_dot(qc[i], st_in[c][h].astype(_BF16)) for i, (c, h) in enumerate(items)]
        mc1, mcol, s = [], [], []
        for i, (c, h) in enumerate(items):
            rmat = jnp.where(causal, rrow[c][h], -jnp.inf)
            mc1.append(jnp.maximum(m_in[c][h][:, 0:1], jnp.max(rmat, axis=-1, keepdims=True)))
            mcol.append(jnp.broadcast_to(mc1[i], (CH, LANES)))
            s.append((qk[i] * jnp.exp(rmat - mcol[i])).astype(_BF16))
        sv = [_dot(s[i], v_ref[h, rows[c], :]) for i, (c, h) in enumerate(items)]
        for i, (c, h) in enumerate(items):
            hs = heads[h]
            bcol = b_c[c][:, MLSTM_HEADS + h:MLSTM_HEADS + h + 1]
            floor = jnp.broadcast_to(jnp.exp(-(bcol + mc1[i])), (CH, LANES))
            w_inter = jnp.exp(m_in[c][h] - mcol[i])
            num = sv[i][:, :MLSTM_HEAD_DIM] + w_inter * qs[i][:, :MLSTM_HEAD_DIM]
            den = sv[i][:, MLSTM_HEAD_DIM:] + w_inter * qs[i][:, MLSTM_HEAD_DIM:]
            hout = num / jnp.maximum(jnp.abs(den), floor)
            hout = hout * lax.rsqrt(jnp.mean(hout * hout, axis=-1, keepdims=True) + NORM_EPS)
            hout = hout * ghead_ref[:, hs] * so_ref[rows[c], hs]
            yb_ref[rows[c], hs] = hout.astype(_BF16)

    ya = _gelu_tanh(jnp.concatenate([y5s_ref[t] for t in range(SSM_WIDTH // LANES)], axis=-1))
    ya = ya * _sigmoid(_dot(ya.astype(_BF16), wglu_ref[...]))

    merged = (sga_ref[...] * _dot(ya.astype(_BF16), wa_ref[...])
              + sgb_ref[...] * _dot(yb_ref[...], wb_ref[...]))
    o_ref[...] = xs_ref[...] + _rms(_dot(merged.astype(_BF16), wout_ref[...]), gpost_ref[...])


def _mixer(stream, y5, consts, l, lp):
    first = len(stream) == 2
    return pl.pallas_call(
        functools.partial(_mixer_kernel, first),
        grid=(lp // TL,),
        in_specs=_stream_specs(first) + [
            pl.BlockSpec((PAIRS, TL // SS, PAIR_W), lambda i: (0, i, 0)),
        ] + [_layer_spec(c.shape, l) for c in consts],
        out_specs=pl.BlockSpec((TL, D_MODEL), lambda i: (i, 0)),
        out_shape=jax.ShapeDtypeStruct((lp, D_MODEL), _F32),
        scratch_shapes=[
            pltpu.VMEM((MLSTM_HEADS, MLSTM_HEAD_DIM, 2 * MLSTM_HEAD_DIM), _F32),
            pltpu.VMEM((SUBLANES, LANES), _F32),
            pltpu.VMEM((TL, MLSTM_WIDTH), _BF16),
            pltpu.VMEM((SSM_WIDTH // LANES, TL, LANES), _F32),
            pltpu.VMEM((TL, MLSTM_WIDTH), _BF16),
            pltpu.VMEM((TL, MLSTM_WIDTH), _F32),
            pltpu.VMEM((MLSTM_HEADS, TL, 2 * MLSTM_HEAD_DIM), _BF16),
            pltpu.VMEM((TL, MLSTM_WIDTH), _F32),
            pltpu.VMEM((TL, D_MODEL), _F32),
            pltpu.VMEM((TL, D_MODEL), _F32),
            pltpu.VMEM((TL, D_MODEL), _BF16),
            pltpu.VMEM((TL, D_MODEL), _F32),
        ] + [pltpu.VMEM((TL + SUBLANES, MIX_BLOCK), _F32)
             for _ in range(2 * MLSTM_WIDTH // MIX_BLOCK)],
        compiler_params=pltpu.CompilerParams(dimension_semantics=("arbitrary",),
                                             vmem_limit_bytes=VMEM_LIMIT),
        name="mixer",
    )(*stream, y5, *consts)


def _ffn_kernel(fuse_next, n_tiles, *refs):
    if fuse_next:
        (x_ref, gpre_ref, gpost_ref, wg_ref, wu_ref, wd_ref, wcv_ref, gnext_ref, wnext_ref,
         o_ref, u_ref, cv_ref, act_ref, hn_ref, prev_ref, us_ref) = refs
    else:
        x_ref, gpre_ref, gpost_ref, wg_ref, wu_ref, wd_ref, wcv_ref, o_ref, cv_ref, act_ref, hn_ref = refs
    pid = pl.program_id(0)

    @pl.when(pid == 0)
    def _():
        cv_ref[0:SUBLANES, :] = jnp.zeros((SUBLANES, FFN_DIM), _F32)
        if fuse_next:
            prev_ref[...] = jnp.zeros_like(prev_ref)

    def tile():
        if fuse_next:
            _s5_project(prev_ref[...], gnext_ref, wnext_ref, u_ref, us_ref)
        out = _ffn_tile(x_ref, gpre_ref, gpost_ref, wg_ref, wu_ref, wd_ref, wcv_ref, cv_ref, act_ref, hn_ref)
        o_ref[...] = out
        if fuse_next:
            prev_ref[...] = out

    if fuse_next:
        pl.when(pid < n_tiles)(tile)

        @pl.when(pid == n_tiles)
        def _():
            _s5_project(prev_ref[...], gnext_ref, wnext_ref, u_ref, us_ref)
    else:
        tile()


def _ffn_tile(x_ref, gpre_ref, gpost_ref, wg_ref, wu_ref, wd_ref, wcv_ref, cv_ref, act_ref, hn_ref):
    hn_ref[...] = _rms(x_ref[...], gpre_ref[...]).astype(_BF16)
    chunk = lambda c: slice(c * FFN_CHUNK, (c + 1) * FFN_CHUNK)
    acc = None
    group_start = 0
    cv_ref[SUBLANES:SUBLANES + TL, chunk(0)] = _dot(hn_ref[...],wg_ref[:, chunk(0)])
    up_next = _dot(hn_ref[...],wu_ref[:, chunk(0)])
    for c in range(N_FFN_CHUNKS):
        cs = chunk(c)
        up = up_next
        if c + 1 < N_FFN_CHUNKS:
            cv_ref[SUBLANES:SUBLANES + TL, chunk(c + 1)] = _dot(hn_ref[...],wg_ref[:, chunk(c + 1)])
            up_next = _dot(hn_ref[...],wu_ref[:, chunk(c + 1)])
        conv = wcv_ref[FFN_CONV:FFN_CONV + 1, cs]
        for j in range(FFN_CONV):
            off = SUBLANES - (FFN_CONV - 1) + j
            conv = conv + wcv_ref[j:j + 1, cs] * cv_ref[off:off + TL, cs]
        cv_ref[0:SUBLANES, cs] = cv_ref[TL:TL + SUBLANES, cs]
        act_ref[:, cs] = (_gelu_tanh(conv) * up).astype(_BF16)
        if (c + 1) % FFN_DOWN_GROUP == 0 or c + 1 == N_FFN_CHUNKS:
            gs = slice(group_start * FFN_CHUNK, (c + 1) * FFN_CHUNK)
            part = _dot(act_ref[:, gs], wd_ref[gs, :])
            acc = part if acc is None else acc + part
            group_start = c + 1
    return x_ref[...] + _rms(acc, gpost_ref[...])


def _ffn(x, consts, l, next_s5=None):
    lp = x.shape[0]
    n_tiles = lp // TL
    scratch = [pltpu.VMEM((TL + SUBLANES, FFN_DIM), _F32),
               pltpu.VMEM((TL, FFN_DIM), _BF16),
               pltpu.VMEM((TL, D_MODEL), _BF16)]
    in_specs = [_layer_spec(c.shape, l) for c in consts]
    if next_s5 is None:
        grid = n_tiles
        x_spec = pl.BlockSpec((TL, D_MODEL), lambda i: (i, 0))
        out_specs = pl.BlockSpec((TL, D_MODEL), lambda i: (jnp.maximum(i - 1, 0), 0))
        out_shape = jax.ShapeDtypeStruct((lp - TL, D_MODEL), _F32)
        args = (x, *consts)
    else:
        grid = n_tiles + 1
        x_spec = pl.BlockSpec((TL, D_MODEL), lambda i: (jnp.minimum(i, n_tiles - 1), 0))
        out_specs = [pl.BlockSpec((TL, D_MODEL), lambda i: (jnp.minimum(i, n_tiles - 1), 0)),
                     pl.BlockSpec((PAIRS, TL // SS, PAIR_W), lambda i: (0, jnp.maximum(i - 1, 0), 0))]
        out_shape = [jax.ShapeDtypeStruct((lp, D_MODEL), _F32),
                     jax.ShapeDtypeStruct((PAIRS, lp // SS, PAIR_W), _BF16)]
        in_specs += [_layer_spec(a.shape, l + 1) for a in next_s5]
        scratch += [pltpu.VMEM((TL, D_MODEL), _F32),
                    pltpu.VMEM((SSM_WIDTH // LANES, TL, LANES), _F32)]
        args = (x, *consts, *next_s5)
    return pl.pallas_call(
        functools.partial(_ffn_kernel, next_s5 is not None, n_tiles),
        grid=(grid,),
        in_specs=[x_spec] + in_specs,
        out_specs=out_specs,
        out_shape=out_shape,
        scratch_shapes=scratch,
        compiler_params=pltpu.CompilerParams(dimension_semantics=("arbitrary",),
                                             vmem_limit_bytes=VMEM_LIMIT),
        name="ffn",
    )(*args)


def _regroup_w_in_kernel(wt_ref, wu_ref, wmix_ref, wgr_ref):
    piece = SSM_WIDTH

    def rows_to_cols(src):
        return wt_ref[src:src + piece, :].T.astype(_BF16)

    wu_ref[...] = rows_to_cols(0)
    tail = W_IN_GATES + 2 * MLSTM_HEADS
    for k in range(WIN_GA // piece):
        wmix_ref[:, k * piece:(k + 1) * piece] = rows_to_cols(SSM_WIDTH + k * piece)
    for k in range(2 * D_MODEL // piece):
        wmix_ref[:, WIN_GA + k * piece:WIN_GA + (k + 1) * piece] = rows_to_cols(tail + k * piece)
    rb = wt_ref.shape[1]
    lane = lax.broadcasted_iota(jnp.int32, (rb, GATE_PAD), 1)
    gates = wt_ref[W_IN_GATES:W_IN_GATES + GATE_PAD, :].T
    wmix_ref[:, WIN_GATES:] = jnp.where(lane < 2 * MLSTM_HEADS, gates, 0.0).astype(_BF16)
    wgr_ref[...] = wt_ref[W_IN_GATES:W_IN_GATES + 2 * MLSTM_HEADS, :].astype(_BF16)


def _regroup_w_in(w_in):
    depth = w_in.shape[0]
    assert w_in.shape[1:] == (D_MODEL, W_IN_COLS)
    return pl.pallas_call(
        _regroup_w_in_kernel,
        grid=(depth, D_MODEL // REGROUP_ROWS),
        in_specs=[pl.BlockSpec((None, W_IN_COLS, REGROUP_ROWS), lambda l, r: (l, 0, r))],
        out_specs=[pl.BlockSpec((None, REGROUP_ROWS, SSM_WIDTH), lambda l, r: (l, r, 0)),
                   pl.BlockSpec((None, REGROUP_ROWS, WIN_COLS), lambda l, r: (l, r, 0)),
                   pl.BlockSpec((None, 2 * MLSTM_HEADS, REGROUP_ROWS), lambda l, r: (l, 0, r))],
        out_shape=[jax.ShapeDtypeStruct((depth, D_MODEL, SSM_WIDTH), _BF16),
                   jax.ShapeDtypeStruct((depth, D_MODEL, WIN_COLS), _BF16),
                   jax.ShapeDtypeStruct((depth, 2 * MLSTM_HEADS, D_MODEL), _BF16)],
        compiler_params=pltpu.CompilerParams(dimension_semantics=("parallel", "parallel"),
                                             vmem_limit_bytes=VMEM_LIMIT),
        name="regroup_w_in",
    )(jnp.swapaxes(w_in, 1, 2))


def _s5_prep(lam_re, lam_im, b_re, b_im, c_re, c_im, d, log_dt):
    hp = lax.Precision.HIGHEST
    gh, npow = SSM_GROUP, SS + 1
    lr = lam_re.reshape(PAIRS, PAIR_STATE)
    li = lam_im.reshape(PAIRS, PAIR_STATE)
    dt = jnp.repeat(jnp.exp(log_dt), SSM_STATE).reshape(PAIRS, PAIR_STATE)
    ks = jnp.arange(npow, dtype=_F32)[None, :, None]
    mag = jnp.exp((lr * dt)[:, None, :] * ks)
    ang = (li * dt)[:, None, :] * ks
    ak_re, ak_im = mag * jnp.cos(ang), mag * jnp.sin(ang)
    nr, ni = ak_re[:, 1] - 1.0, ak_im[:, 1]
    den = lr * lr + li * li
    z_re = (nr * lr + ni * li) / den
    z_im = (ni * lr - nr * li) / den
    bt_re = jnp.swapaxes(b_re.reshape(PAIRS, PAIR_STATE, gh), 1, 2)
    bt_im = jnp.swapaxes(b_im.reshape(PAIRS, PAIR_STATE, gh), 1, 2)
    bb_re = z_re[:, None, :] * bt_re - z_im[:, None, :] * bt_im
    bb_im = z_re[:, None, :] * bt_im + z_im[:, None, :] * bt_re
    as_hn = lambda c: c.reshape(PAIRS, 2, gh, SSM_STATE).transpose(0, 2, 1, 3).reshape(PAIRS, gh, PAIR_STATE)
    ct_re, ct_im = as_hn(c_re), as_hn(c_im)
    same_group = (jnp.arange(2)[:, None] == (jnp.arange(PAIR_STATE) // SSM_STATE)[None, :]).astype(_F32)
    same_group = same_group[None, None, :, None, :]

    def times_powers(m_re, m_im, pw_re, pw_im):
        pr, pi = pw_re[:, :, None, None, :], pw_im[:, :, None, None, :]
        mr, mi = m_re[:, None, None, :, :], m_im[:, None, None, :, :]
        shape = (PAIRS, npow * PAIR_IN, PAIR_STATE)
        return (((pr * mr - pi * mi) * same_group).reshape(shape),
                ((pr * mi + pi * mr) * same_group).reshape(shape))

    wk_re, wk_im = times_powers(bb_re, bb_im, ak_re[:, ::-1], ak_im[:, ::-1])
    bs = jnp.concatenate([wk_re[:, PAIR_IN:], wk_im[:, PAIR_IN:]], axis=-1)

    ca_re, ca_im = times_powers(ct_re, ct_im, ak_re, ak_im)
    cs = jnp.concatenate([ca_re[:, PAIR_IN:], -ca_im[:, PAIR_IN:]], axis=-1)

    bbm_re, bbm_im = wk_re[:, SS * PAIR_IN:], wk_im[:, SS * PAIR_IN:]
    kt = (jnp.einsum('qrn,qcn->qrc', bbm_re, ca_re[:, :PAIR_W], precision=hp)
          - jnp.einsum('qrn,qcn->qrc', bbm_im, ca_im[:, :PAIR_W], precision=hp))
    ds = jnp.stack([jnp.pad(kt[..., :PAIR_W - PAIR_IN * i], ((0, 0), (0, 0), (PAIR_IN * i, 0)))
                    for i in range(SS)], axis=1).reshape(PAIRS, PAIR_W, PAIR_W)
    skip = jnp.tile(d.reshape(PAIRS, PAIR_IN), (1, SS))
    ds = ds + skip[:, None, :] * jnp.eye(PAIR_W, dtype=_F32)[None]

    ks8 = SS * jnp.arange(SUBLANES + 1, dtype=_F32)
    mag8 = jnp.exp((lr * dt).reshape(HALF)[None, :] * ks8[:, None])
    ang8 = (li * dt).reshape(HALF)[None, :] * ks8[:, None]
    apow = jnp.concatenate([mag8 * jnp.cos(ang8), mag8 * jnp.sin(ang8)], axis=1)
    sub = jnp.arange(SUBLANES)
    shifted = [jnp.where((sub >= k)[:, None], apow[k][None, :], 0.0) for k in (1, 2, 4)]
    a8 = jnp.stack(shifted + [apow[:SUBLANES], jnp.broadcast_to(apow[SUBLANES][None], (SUBLANES, 2 * HALF))])
    return bs.astype(_BF16), cs.astype(_BF16), ds.astype(_BF16), a8


def kernel(x, meta_tokens, g_mix_pre, g_mix_post, w_in, b_gates, ssm_lambda_re, ssm_lambda_im, ssm_b_re,
           ssm_b_im, ssm_c_re, ssm_c_im, ssm_d, ssm_log_dt, w_ssm_glu, w_qk_conv, b_qk_conv, g_head_norm,
           w_branch_ssm, w_branch_mlstm, w_out, g_ffn_pre, g_ffn_post, w_ffn_gate, w_ffn_up, w_ffn_conv,
           b_ffn_conv, w_ffn_down):
    bsz, seq, _ = x.shape
    assert bsz == 1
    depth = w_in.shape[0]
    assert seq % TL == 0 and FRONT_PAD + N_META == TL
    lp = TL + seq
    assert (lp // SS) % SUBLANES == 0

    head = jnp.concatenate([jnp.zeros((FRONT_PAD, D_MODEL), _F32), meta_tokens.astype(_F32)], axis=0)
    stream = (head, x[0])

    nh2 = 2 * MLSTM_HEADS
    row = lambda a: a[:, None, :]
    w_u, w_mix, w_gr = _regroup_w_in(w_in)
    b_gc = row(jnp.pad(b_gates, ((0, 0), (0, GATE_PAD - nh2))))
    b_gr = jnp.broadcast_to(b_gates[:, :, None], (depth, nh2, LANES))
    w_cv = jnp.concatenate([w_qk_conv, row(b_qk_conv),
                            jnp.zeros((depth, SUBLANES - QK_CONV - 1, 2 * MLSTM_WIDTH), _F32)], axis=1)
    mixer_consts = (row(g_mix_pre), row(g_mix_post), w_mix, w_gr, b_gc, b_gr, w_cv,
                    row(g_head_norm), w_ssm_glu.astype(_BF16), w_branch_ssm.astype(_BF16),
                    w_branch_mlstm.astype(_BF16), w_out.astype(_BF16))
    f_cv = jnp.concatenate([w_ffn_conv, row(b_ffn_conv),
                            jnp.zeros((depth, SUBLANES - FFN_CONV - 1, FFN_DIM), _F32)], axis=1)
    ffn_consts = (row(g_ffn_pre), row(g_ffn_post), w_ffn_gate.astype(_BF16), w_ffn_up.astype(_BF16),
                  w_ffn_down.astype(_BF16), f_cv)
    bs, cs, ds, a8 = jax.vmap(_s5_prep)(ssm_lambda_re, ssm_lambda_im, ssm_b_re, ssm_b_im,
                                        ssm_c_re, ssm_c_im, ssm_d, ssm_log_dt)
    g_pre = row(g_mix_pre)

    u = _s5_in(stream, g_pre, w_u, 0, lp)
    for l in range(depth):
        y5 = _s5_core(u, bs, cs, ds, a8, l)
        h = _mixer(stream, y5, mixer_consts, l, lp)
        if l + 1 < depth:
            h, u = _ffn(h, ffn_consts, l, next_s5=(g_pre, w_u))
        else:
            h = _ffn(h, ffn_consts, l)
        stream = (h,)
    return h[None]
```

```python
import functools
import math

import jax
import jax.numpy as jnp
from jax import lax
from jax.experimental import pallas as pl
from jax.experimental.pallas import tpu as pltpu

D_MODEL = 1024
N_META = 16
SSM_WIDTH = 512
SSM_GROUP = 16
SSM_GROUPS = 32
SSM_STATE = 64
MLSTM_WIDTH = 512
MLSTM_HEADS = 4
MLSTM_HEAD_DIM = 128
QK_CONV = 4
FFN_DIM = 2816
FFN_CONV = 3
NORM_EPS = 1e-6
PAD_LOG_INPUT_GATE = -1e4

LANES = 128
SUBLANES = 8
TL = 512
CH = 128
SS = 8
PAIRS = SSM_GROUPS // 2
PAIR_IN = 2 * SSM_GROUP
PAIR_W = SS * PAIR_IN
PAIR_STATE = 2 * SSM_STATE
PAIRS_PER_TILE = LANES // PAIR_IN
HALF = PAIRS * PAIR_STATE
FRONT_PAD = TL - N_META
MIX_BLOCK = 256
MLSTM_STAGE_CHUNKS = 1
S5_ROW_BLOCK = 144
FFN_CHUNK = 256
N_FFN_CHUNKS = FFN_DIM // FFN_CHUNK
FFN_DOWN_GROUP = 11
GATE_PAD = LANES
WIN_V = 2 * MLSTM_WIDTH
WIN_O = WIN_V + MLSTM_WIDTH
WIN_GA = WIN_O + MLSTM_WIDTH
WIN_GB = WIN_GA + D_MODEL
WIN_GATES = WIN_GB + D_MODEL
WIN_COLS = WIN_GATES + GATE_PAD
W_IN_COLS = SSM_WIDTH + 4 * MLSTM_WIDTH + 2 * MLSTM_HEADS + 2 * D_MODEL
W_IN_GATES = SSM_WIDTH + 4 * MLSTM_WIDTH
REGROUP_ROWS = 256
VMEM_LIMIT = 56 * 1024 * 1024

_BF16 = jnp.bfloat16
_F32 = jnp.float32


def _rms(x, g):
    return x * lax.rsqrt(jnp.mean(x * x, axis=-1, keepdims=True) + NORM_EPS) * g


def _gelu_tanh(x):
    c = math.sqrt(2.0 / math.pi)
    half = 0.5 * x
    return half + half * jnp.tanh(x * (c + (c * 0.044715) * (x * x)))


def _sigmoid(x):
    return 0.5 + 0.5 * jnp.tanh(0.5 * x)


def _log_sigmoid(x):
    return jnp.minimum(x, 0.0) - jnp.log(1.0 + jnp.exp(-jnp.abs(x)))


def _dot(a, b):
    return jnp.dot(a, b, preferred_element_type=_F32)


def _layer_spec(shape, l):
    nd = len(shape) - 1
    return pl.BlockSpec((None,) + tuple(shape[1:]), lambda i, _l=l, _nd=nd: (_l,) + (0,) * _nd)


def _stream_specs(first):
    if first:
        return [pl.BlockSpec((TL, D_MODEL), lambda i: (0, 0)),
                pl.BlockSpec((TL, D_MODEL), lambda i: (jnp.maximum(i - 1, 0), 0))]
    return [pl.BlockSpec((TL, D_MODEL), lambda i: (i, 0))]


def _stream_tile(first, refs):
    if first:
        head_ref, x_ref, *rest = refs
        return jnp.where(pl.program_id(0) == 0, head_ref[...], x_ref[...]), rest
    x_ref, *rest = refs
    return x_ref[...], rest


def _s5_in_kernel(first, *refs):
    x, (g_ref, w_ref, u_ref, us_ref) = _stream_tile(first, refs)
    _s5_project(x, g_ref, w_ref, u_ref, us_ref)


def _s5_project(x, g_ref, w_ref, u_ref, us_ref):
    u = _dot(_rms(x, g_ref[...]).astype(_BF16), w_ref[...])
    for t in range(SSM_WIDTH // LANES):
        us_ref[t] = u[:, t * LANES:(t + 1) * LANES]
    for j in range(SS):
        for t in range(SSM_WIDTH // LANES):
            rows = us_ref[t, pl.ds(j, TL // SS, stride=SS), :].astype(_BF16)
            for k in range(PAIRS_PER_TILE):
                u_ref[t * PAIRS_PER_TILE + k, :, j * PAIR_IN:(j + 1) * PAIR_IN] = (
                    rows[:, k * PAIR_IN:(k + 1) * PAIR_IN])


def _s5_in(stream, g, w, l, lp):
    first = len(stream) == 2
    return pl.pallas_call(
        functools.partial(_s5_in_kernel, first),
        grid=(lp // TL,),
        in_specs=_stream_specs(first) + [_layer_spec(g.shape, l), _layer_spec(w.shape, l)],
        out_specs=pl.BlockSpec((PAIRS, TL // SS, PAIR_W), lambda i: (0, i, 0)),
        out_shape=jax.ShapeDtypeStruct((PAIRS, lp // SS, PAIR_W), _BF16),
        scratch_shapes=[pltpu.VMEM((SSM_WIDTH // LANES, TL, LANES), _F32)],
        compiler_params=pltpu.CompilerParams(dimension_semantics=("parallel",),
                                             vmem_limit_bytes=VMEM_LIMIT),
        name="s5_in",
    )(*stream, g, w)


def _s5_core_kernel(rt, u_ref, bs_ref, cs_ref, ds_ref, a_ref, y_ref, vx_ref, st_ref):
    @pl.when(pl.program_id(0) == 0)
    def _():
        st_ref[...] = jnp.zeros_like(st_ref)

    bounds = list(range(0, rt, S5_ROW_BLOCK)) + [rt]
    blocks = [(bounds[k], bounds[k + 1]) for k in range(len(bounds) - 1)]

    def state_input(r0, r1):
        g0, g1 = r0 // SUBLANES, r1 // SUBLANES
        for p in range(PAIRS):
            v = _dot(u_ref[p, r0:r1, :], bs_ref[p])
            vx_ref[g0:g1, :, p * PAIR_STATE:(p + 1) * PAIR_STATE] = (
                v[:, :PAIR_STATE].reshape(g1 - g0, SUBLANES, PAIR_STATE))
            vx_ref[g0:g1, :, HALF + p * PAIR_STATE:HALF + (p + 1) * PAIR_STATE] = (
                v[:, PAIR_STATE:].reshape(g1 - g0, SUBLANES, PAIR_STATE))

    def output(r0, r1):
        g0, g1 = r0 // SUBLANES, r1 // SUBLANES
        for p in range(PAIRS):
            x_re = vx_ref[g0:g1, :, p * PAIR_STATE:(p + 1) * PAIR_STATE].reshape(r1 - r0, PAIR_STATE)
            x_im = vx_ref[g0:g1, :, HALF + p * PAIR_STATE:HALF + (p + 1) * PAIR_STATE].reshape(r1 - r0, PAIR_STATE)
            xp = jnp.concatenate([x_re, x_im], axis=-1).astype(_BF16)
            y_ref[p, r0:r1, :] = (
                lax.dot_general(xp, cs_ref[p], (((1,), (1,)), ((), ())), preferred_element_type=_F32)
                + _dot(u_ref[p, r0:r1, :], ds_ref[p]))

    def cmul(t, z_re, z_im):
        t_re, t_im = a_ref[t, :, :HALF], a_ref[t, :, HALF:]
        return t_re * z_re - t_im * z_im, t_re * z_im + t_im * z_re

    first = lax.broadcasted_iota(jnp.int32, (SUBLANES, HALF), 0) == 0

    def body(i, carry):
        s_re, s_im = carry
        w_re = vx_ref[i, :, :HALF]
        w_im = vx_ref[i, :, HALF:]
        for t, shift in enumerate((1, 2, 4)):
            d_re, d_im = cmul(t, pltpu.roll(w_re, shift, 0), pltpu.roll(w_im, shift, 0))
            w_re, w_im = w_re + d_re, w_im + d_im
        x_re, x_im = cmul(3, s_re, s_im)
        vx_ref[i, :, :HALF] = x_re + jnp.where(first, 0.0, pltpu.roll(w_re, 1, 0))
        vx_ref[i, :, HALF:] = x_im + jnp.where(first, 0.0, pltpu.roll(w_im, 1, 0))
        n_re, n_im = cmul(4, s_re, s_im)
        last = slice(SUBLANES - 1, SUBLANES)
        return (n_re + jnp.broadcast_to(w_re[last], (SUBLANES, HALF)),
                n_im + jnp.broadcast_to(w_im[last], (SUBLANES, HALF)))

    carry = (st_ref[:, :HALF], st_ref[:, HALF:])
    state_input(*blocks[0])
    for k, (r0, r1) in enumerate(blocks):
        if k + 1 < len(blocks):
            state_input(*blocks[k + 1])
        for i in range(r0 // SUBLANES, r1 // SUBLANES):
            carry = body(i, carry)
        output(r0, r1)
    st_ref[:, :HALF] = carry[0]
    st_ref[:, HALF:] = carry[1]


def _s5_core(u, bs, cs, ds, a8, l):
    rows = u.shape[1]
    rt = rows // 8 if (rows // 8) % SUBLANES == 0 else rows
    return pl.pallas_call(
        functools.partial(_s5_core_kernel, rt),
        grid=(rows // rt,),
        in_specs=[
            pl.BlockSpec((PAIRS, rt, PAIR_W), lambda i: (0, i, 0)),
            _layer_spec(bs.shape, l), _layer_spec(cs.shape, l), _layer_spec(ds.shape, l),
            _layer_spec(a8.shape, l),
        ],
        out_specs=pl.BlockSpec((PAIRS, rt, PAIR_W), lambda i: (0, i, 0)),
        out_shape=jax.ShapeDtypeStruct((PAIRS, rows, PAIR_W), _F32),
        scratch_shapes=[
            pltpu.VMEM((rt // SUBLANES, SUBLANES, 2 * HALF), _F32),
            pltpu.VMEM((SUBLANES, 2 * HALF), _F32),
        ],
        compiler_params=pltpu.CompilerParams(dimension_semantics=("arbitrary",),
                                             vmem_limit_bytes=VMEM_LIMIT),
        name="s5_core",
    )(u, bs, cs, ds, a8)


def _mixer_kernel(first, *refs):
    x, rest = _stream_tile(first, refs)
    (y5_ref, gpre_ref, gpost_ref, win_ref, wgr_ref, bgc_ref, bgr_ref, wcv_ref,
     ghead_ref, wglu_ref, wa_ref, wb_ref,
     wout_ref, o_ref, st_ref, m_ref, yb_ref, y5s_ref, q_ref, k_ref, v_ref,
     so_ref, sga_ref, sgb_ref, hn_ref, xs_ref, *cv_refs) = rest
    pid = pl.program_id(0)

    @pl.when(pid == 0)
    def _():
        for cv_ref in cv_refs:
            cv_ref[0:SUBLANES, :] = jnp.zeros((SUBLANES, MIX_BLOCK), _F32)
        st_ref[...] = jnp.zeros_like(st_ref)
        m_ref[...] = jnp.zeros_like(m_ref)
        for h in range(MLSTM_HEADS):
            v_ref[h, :, MLSTM_HEAD_DIM:] = jnp.ones((TL, MLSTM_HEAD_DIM), _BF16)

    xs_ref[...] = x
    hn_ref[...] = _rms(x, gpre_ref[...]).astype(_BF16)

    row_c = pid * TL + lax.broadcasted_iota(jnp.int32, (TL, 1), 0)
    valid_c = row_c >= FRONT_PAD
    row_r = pid * TL + lax.broadcasted_iota(jnp.int32, (1, TL), 1)
    valid_r = row_r >= FRONT_PAD

    n_blocks = 2 * MLSTM_WIDTH // MIX_BLOCK

    def proj_block(b):
        cols = slice(b * MIX_BLOCK, (b + 1) * MIX_BLOCK)
        cv_refs[b][SUBLANES:SUBLANES + TL, :] = _dot(hn_ref[...],win_ref[:, cols])

    def conv_block(b):
        cols = slice(b * MIX_BLOCK, (b + 1) * MIX_BLOCK)
        cv_ref = cv_refs[b]
        acc = wcv_ref[QK_CONV:QK_CONV + 1, cols]
        for j in range(QK_CONV):
            off = SUBLANES - (QK_CONV - 1) + j
            acc = acc + wcv_ref[j:j + 1, cols] * cv_ref[off:off + TL, :]
        cv_ref[0:SUBLANES, :] = cv_ref[TL:TL + SUBLANES, :]
        qk = jnp.where(valid_c, acc * _sigmoid(acc), 0.0)
        if b < MLSTM_WIDTH // MIX_BLOCK:
            q_ref[:, cols] = qk.astype(_BF16)
        else:
            kcols = slice(b * MIX_BLOCK - MLSTM_WIDTH, (b + 1) * MIX_BLOCK - MLSTM_WIDTH)
            k_ref[:, kcols] = qk * (MLSTM_HEAD_DIM ** -0.5)

    gate_w = D_MODEL // n_blocks

    def branch_gate_block(dst_ref, base, b):
        dst_ref[:, b * gate_w:(b + 1) * gate_w] = _sigmoid(
            _dot(hn_ref[...],win_ref[:, base + b * gate_w:base + (b + 1) * gate_w]))

    q_blocks = MLSTM_WIDTH // MIX_BLOCK
    for b in range(q_blocks, n_blocks):
        proj_block(b)
    v_all = jnp.where(valid_c, _dot(hn_ref[...],win_ref[:, WIN_V:WIN_O]), 0.0).astype(_BF16)
    for h in range(MLSTM_HEADS):
        v_ref[h, :, :MLSTM_HEAD_DIM] = v_all[:, h * MLSTM_HEAD_DIM:(h + 1) * MLSTM_HEAD_DIM]
    lane = lax.broadcasted_iota(jnp.int32, (1, GATE_PAD), 1)
    gc = _dot(hn_ref[...],win_ref[:, WIN_GATES:]) + bgc_ref[...]
    gc = jnp.where(lane < MLSTM_HEADS, gc, _log_sigmoid(gc))
    gc = jnp.where(valid_c, gc, jnp.where(lane < MLSTM_HEADS, PAD_LOG_INPUT_GATE, 0.0))
    sub = lax.broadcasted_iota(jnp.int32, (2 * MLSTM_HEADS, 1), 0)
    gr = lax.dot_general(wgr_ref[...], hn_ref[...], (((1,), (1,)), ((), ())),
                         preferred_element_type=_F32) + bgr_ref[:, 0:1]
    gr = jnp.where(sub < MLSTM_HEADS, gr, _log_sigmoid(gr))
    gr = jnp.where(valid_r, gr, jnp.where(sub < MLSTM_HEADS, PAD_LOG_INPUT_GATE, 0.0))
    for b in range(q_blocks):
        conv_block(q_blocks + b)
        proj_block(b)

    it = lax.broadcasted_iota(jnp.int32, (CH, CH), 0)
    js = lax.broadcasted_iota(jnp.int32, (CH, CH), 1)
    causal = it >= js
    tri_l = causal.astype(_F32)
    tri_u = (it <= js).astype(_F32)

    def regroup_s5(j):
        for t in range(SSM_WIDTH // LANES):
            yj = jnp.concatenate([y5_ref[t * PAIRS_PER_TILE + k, :, j * PAIR_IN:(j + 1) * PAIR_IN]
                                  for k in range(PAIRS_PER_TILE)], axis=-1)
            y5s_ref[t, pl.ds(j, TL // SS, stride=SS), :] = yj

    n_chunks = TL // CH
    hp = lax.Precision.HIGHEST
    rows = [slice(c * CH, (c + 1) * CH) for c in range(n_chunks)]
    heads = [slice(h * MLSTM_HEAD_DIM, (h + 1) * MLSTM_HEAD_DIM) for h in range(MLSTM_HEADS)]

    gcc = [gc[rows[c], :] for c in range(n_chunks)]
    b_c = [jnp.dot(tri_l, gcc[c], preferred_element_type=_F32, precision=hp) for c in range(n_chunks)]
    b_r = [jnp.dot(gr[:, rows[c]], tri_u, preferred_element_type=_F32, precision=hp) for c in range(n_chunks)]

    rrow = [[None] * MLSTM_HEADS for _ in range(n_chunks)]
    m_in = [[None] * MLSTM_HEADS for _ in range(n_chunks)]
    m_out = [[None] * MLSTM_HEADS for _ in range(n_chunks)]
    for h in range(MLSTM_HEADS):
        fl = MLSTM_HEADS + h
        m = m_ref[h:h + 1, :]
        for c in range(n_chunks):
            rrow[c][h] = gr[h:h + 1, rows[c]] - b_r[c][fl:fl + 1, :]
            m_in[c][h] = m
            m_out[c][h] = jnp.maximum(m, jnp.max(rrow[c][h], axis=-1, keepdims=True))
            m = b_r[c][fl:fl + 1, CH - 1:CH] + m_out[c][h]
        m_ref[h:h + 1, :] = m

    st_in = [[None] * MLSTM_HEADS for _ in range(n_chunks)]
    for h in range(MLSTM_HEADS):
        fl = MLSTM_HEADS + h
        st = st_ref[h]
        for c in range(n_chunks):
            st_in[c][h] = st
            rcol = gcc[c][:, h:h + 1] - b_c[c][:, fl:fl + 1]
            kw = k_ref[rows[c], heads[h]] * jnp.exp(rcol - m_out[c][h][:, 0:1])
            decay = jnp.exp(m_in[c][h] - m_out[c][h])
            st = (jnp.concatenate([decay, decay], axis=-1) * st
                  + _dot(kw.T.astype(_BF16), v_ref[h, rows[c], :]))
        st_ref[h] = st

    for b in range(q_blocks):
        conv_block(b)
    so_ref[...] = _sigmoid(_dot(hn_ref[...],win_ref[:, WIN_O:WIN_GA]))

    for c0 in range(0, n_chunks, MLSTM_STAGE_CHUNKS):
        group = range(c0, c0 + MLSTM_STAGE_CHUNKS)
        for c in group:
            branch_gate_block(sga_ref, WIN_GA, c)
            branch_gate_block(sgb_ref, WIN_GB, c)
            for j in range(c * SS // n_chunks, (c + 1) * SS // n_chunks):
                regroup_s5(j)
        items = [(c, h) for c in group for h in range(MLSTM_HEADS)]
        qc = [q_ref[rows[c], heads[h]] for c, h in items]
        qk = [lax.dot_general(qc[i], k_ref[rows[c], heads[h]].astype(_BF16), (((1,), (1,)), ((), ())),
                              preferred_element_type=_F32) for i, (c, h) in enumerate(items)]
        qs = [_dot(qc[i], st_in[c][h].astype(_BF16)) for i, (c, h) in enumerate(items)]
        mc1, mcol, s = [], [], []
        for i, (c, h) in enumerate(items):
            rmat = jnp.where(causal, rrow[c][h], -jnp.inf)
            mc1.append(jnp.maximum(m_in[c][h][:, 0:1], jnp.max(rmat, axis=-1, keepdims=True)))
            mcol.append(jnp.broadcast_to(mc1[i], (CH, LANES)))
            s.append((qk[i] * jnp.exp(rmat - mcol[i])).astype(_BF16))
        sv = [_dot(s[i], v_ref[h, rows[c], :]) for i, (c, h) in enumerate(items)]
        for i, (c, h) in enumerate(items):
            hs = heads[h]
            bcol = b_c[c][:, MLSTM_HEADS + h:MLSTM_HEADS + h + 1]
            floor = jnp.broadcast_to(jnp.exp(-(bcol + mc1[i])), (CH, LANES))
            w_inter = jnp.exp(m_in[c][h] - mcol[i])
            num = sv[i][:, :MLSTM_HEAD_DIM] + w_inter * qs[i][:, :MLSTM_HEAD_DIM]
            den = sv[i][:, MLSTM_HEAD_DIM:] + w_inter * qs[i][:, MLSTM_HEAD_DIM:]
            hout = num / jnp.maximum(jnp.abs(den), floor)
            hout = hout * lax.rsqrt(jnp.mean(hout * hout, axis=-1, keepdims=True) + NORM_EPS)
            hout = hout * ghead_ref[:, hs] * so_ref[rows[c], hs]
            yb_ref[rows[c], hs] = hout.astype(_BF16)

    ya = _gelu_tanh(jnp.concatenate([y5s_ref[t] for t in range(SSM_WIDTH // LANES)], axis=-1))
    ya = ya * _sigmoid(_dot(ya.astype(_BF16), wglu_ref[...]))

    merged = (sga_ref[...] * _dot(ya.astype(_BF16), wa_ref[...])
              + sgb_ref[...] * _dot(yb_ref[...], wb_ref[...]))
    o_ref[...] = xs_ref[...] + _rms(_dot(merged.astype(_BF16), wout_ref[...]), gpost_ref[...])


def _mixer(stream, y5, consts, l, lp):
    first = len(stream) == 2
    return pl.pallas_call(
        functools.partial(_mixer_kernel, first),
        grid=(lp // TL,),
        in_specs=_stream_specs(first) + [
            pl.BlockSpec((PAIRS, TL // SS, PAIR_W), lambda i: (0, i, 0)),
        ] + [_layer_spec(c.shape, l) for c in consts],
        out_specs=pl.BlockSpec((TL, D_MODEL), lambda i: (i, 0)),
        out_shape=jax.ShapeDtypeStruct((lp, D_MODEL), _F32),
        scratch_shapes=[
            pltpu.VMEM((MLSTM_HEADS, MLSTM_HEAD_DIM, 2 * MLSTM_HEAD_DIM), _F32),
            pltpu.VMEM((SUBLANES, LANES), _F32),
            pltpu.VMEM((TL, MLSTM_WIDTH), _BF16),
            pltpu.VMEM((SSM_WIDTH // LANES, TL, LANES), _F32),
            pltpu.VMEM((TL, MLSTM_WIDTH), _BF16),
            pltpu.VMEM((TL, MLSTM_WIDTH), _F32),
            pltpu.VMEM((MLSTM_HEADS, TL, 2 * MLSTM_HEAD_DIM), _BF16),
            pltpu.VMEM((TL, MLSTM_WIDTH), _F32),
            pltpu.VMEM((TL, D_MODEL), _F32),
            pltpu.VMEM((TL, D_MODEL), _F32),
            pltpu.VMEM((TL, D_MODEL), _BF16),
            pltpu.VMEM((TL, D_MODEL), _F32),
        ] + [pltpu.VMEM((TL + SUBLANES, MIX_BLOCK), _F32)
             for _ in range(2 * MLSTM_WIDTH // MIX_BLOCK)],
        compiler_params=pltpu.CompilerParams(dimension_semantics=("arbitrary",),
                                             vmem_limit_bytes=VMEM_LIMIT),
        name="mixer",
    )(*stream, y5, *consts)


def _ffn_kernel(fuse_next, n_tiles, *refs):
    if fuse_next:
        (x_ref, gpre_ref, gpost_ref, wg_ref, wu_ref, wd_ref, wcv_ref, gnext_ref, wnext_ref,
         o_ref, u_ref, cv_ref, act_ref, hn_ref, prev_ref, us_ref) = refs
    else:
        x_ref, gpre_ref, gpost_ref, wg_ref, wu_ref, wd_ref, wcv_ref, o_ref, cv_ref, act_ref, hn_ref = refs
    pid = pl.program_id(0)

    @pl.when(pid == 0)
    def _():
        cv_ref[0:SUBLANES, :] = jnp.zeros((SUBLANES, FFN_DIM), _F32)
        if fuse_next:
            prev_ref[...] = jnp.zeros_like(prev_ref)

    def tile():
        if fuse_next:
            _s5_project(prev_ref[...], gnext_ref, wnext_ref, u_ref, us_ref)
        out = _ffn_tile(x_ref, gpre_ref, gpost_ref, wg_ref, wu_ref, wd_ref, wcv_ref, cv_ref, act_ref, hn_ref)
        o_ref[...] = out
        if fuse_next:
            prev_ref[...] = out

    if fuse_next:
        pl.when(pid < n_tiles)(tile)

        @pl.when(pid == n_tiles)
        def _():
            _s5_project(prev_ref[...], gnext_ref, wnext_ref, u_ref, us_ref)
    else:
        tile()


def _ffn_tile(x_ref, gpre_ref, gpost_ref, wg_ref, wu_ref, wd_ref, wcv_ref, cv_ref, act_ref, hn_ref):
    hn_ref[...] = _rms(x_ref[...], gpre_ref[...]).astype(_BF16)
    chunk = lambda c: slice(c * FFN_CHUNK, (c + 1) * FFN_CHUNK)
    acc = None
    group_start = 0
    cv_ref[SUBLANES:SUBLANES + TL, chunk(0)] = _dot(hn_ref[...],wg_ref[:, chunk(0)])
    up_next = _dot(hn_ref[...],wu_ref[:, chunk(0)])
    for c in range(N_FFN_CHUNKS):
        cs = chunk(c)
        up = up_next
        if c + 1 < N_FFN_CHUNKS:
            cv_ref[SUBLANES:SUBLANES + TL, chunk(c + 1)] = _dot(hn_ref[...],wg_ref[:, chunk(c + 1)])
            up_next = _dot(hn_ref[...],wu_ref[:, chunk(c + 1)])
        conv = wcv_ref[FFN_CONV:FFN_CONV + 1, cs]
        for j in range(FFN_CONV):
            off = SUBLANES - (FFN_CONV - 1) + j
            conv = conv + wcv_ref[j:j + 1, cs] * cv_ref[off:off + TL, cs]
        cv_ref[0:SUBLANES, cs] = cv_ref[TL:TL + SUBLANES, cs]
        act_ref[:, cs] = (_gelu_tanh(conv) * up).astype(_BF16)
        if (c + 1) % FFN_DOWN_GROUP == 0 or c + 1 == N_FFN_CHUNKS:
            gs = slice(group_start * FFN_CHUNK, (c + 1) * FFN_CHUNK)
            part = _dot(act_ref[:, gs], wd_ref[gs, :])
            acc = part if acc is None else acc + part
            group_start = c + 1
    return x_ref[...] + _rms(acc, gpost_ref[...])


def _ffn(x, consts, l, next_s5=None):
    lp = x.shape[0]
    n_tiles = lp // TL
    scratch = [pltpu.VMEM((TL + SUBLANES, FFN_DIM), _F32),
               pltpu.VMEM((TL, FFN_DIM), _BF16),
               pltpu.VMEM((TL, D_MODEL), _BF16)]
    in_specs = [_layer_spec(c.shape, l) for c in consts]
    if next_s5 is None:
        grid = n_tiles
        x_spec = pl.BlockSpec((TL, D_MODEL), lambda i: (i, 0))
        out_specs = pl.BlockSpec((TL, D_MODEL), lambda i: (jnp.maximum(i - 1, 0), 0))
        out_shape = jax.ShapeDtypeStruct((lp - TL, D_MODEL), _F32)
        args = (x, *consts)
    else:
        grid = n_tiles + 1
        x_spec = pl.BlockSpec((TL, D_MODEL), lambda i: (jnp.minimum(i, n_tiles - 1), 0))
        out_specs = [pl.BlockSpec((TL, D_MODEL), lambda i: (jnp.minimum(i, n_tiles - 1), 0)),
                     pl.BlockSpec((PAIRS, TL // SS, PAIR_W), lambda i: (0, jnp.maximum(i - 1, 0), 0))]
        out_shape = [jax.ShapeDtypeStruct((lp, D_MODEL), _F32),
                     jax.ShapeDtypeStruct((PAIRS, lp // SS, PAIR_W), _BF16)]
        in_specs += [_layer_spec(a.shape, l + 1) for a in next_s5]
        scratch += [pltpu.VMEM((TL, D_MODEL), _F32),
                    pltpu.VMEM((SSM_WIDTH // LANES, TL, LANES), _F32)]
        args = (x, *consts, *next_s5)
    return pl.pallas_call(
        functools.partial(_ffn_kernel, next_s5 is not None, n_tiles),
        grid=(grid,),
        in_specs=[x_spec] + in_specs,
        out_specs=out_specs,
        out_shape=out_shape,
        scratch_shapes=scratch,
        compiler_params=pltpu.CompilerParams(dimension_semantics=("arbitrary",),
                                             vmem_limit_bytes=VMEM_LIMIT),
        name="ffn",
    )(*args)


def _regroup_w_in_kernel(wt_ref, wu_ref, wmix_ref, wgr_ref):
    piece = SSM_WIDTH

    def rows_to_cols(src):
        return wt_ref[src:src + piece, :].T.astype(_BF16)

    wu_ref[...] = rows_to_cols(0)
    tail = W_IN_GATES + 2 * MLSTM_HEADS
    for k in range(WIN_GA // piece):
        wmix_ref[:, k * piece:(k + 1) * piece] = rows_to_cols(SSM_WIDTH + k * piece)
    for k in range(2 * D_MODEL // piece):
        wmix_ref[:, WIN_GA + k * piece:WIN_GA + (k + 1) * piece] = rows_to_cols(tail + k * piece)
    rb = wt_ref.shape[1]
    lane = lax.broadcasted_iota(jnp.int32, (rb, GATE_PAD), 1)
    gates = wt_ref[W_IN_GATES:W_IN_GATES + GATE_PAD, :].T
    wmix_ref[:, WIN_GATES:] = jnp.where(lane < 2 * MLSTM_HEADS, gates, 0.0).astype(_BF16)
    wgr_ref[...] = wt_ref[W_IN_GATES:W_IN_GATES + 2 * MLSTM_HEADS, :].astype(_BF16)


def _regroup_w_in(w_in):
    depth = w_in.shape[0]
    assert w_in.shape[1:] == (D_MODEL, W_IN_COLS)
    return pl.pallas_call(
        _regroup_w_in_kernel,
        grid=(depth, D_MODEL // REGROUP_ROWS),
        in_specs=[pl.BlockSpec((None, W_IN_COLS, REGROUP_ROWS), lambda l, r: (l, 0, r))],
        out_specs=[pl.BlockSpec((None, REGROUP_ROWS, SSM_WIDTH), lambda l, r: (l, r, 0)),
                   pl.BlockSpec((None, REGROUP_ROWS, WIN_COLS), lambda l, r: (l, r, 0)),
                   pl.BlockSpec((None, 2 * MLSTM_HEADS, REGROUP_ROWS), lambda l, r: (l, 0, r))],
        out_shape=[jax.ShapeDtypeStruct((depth, D_MODEL, SSM_WIDTH), _BF16),
                   jax.ShapeDtypeStruct((depth, D_MODEL, WIN_COLS), _BF16),
                   jax.ShapeDtypeStruct((depth, 2 * MLSTM_HEADS, D_MODEL), _BF16)],
        compiler_params=pltpu.CompilerParams(dimension_semantics=("parallel", "parallel"),
                                             vmem_limit_bytes=VMEM_LIMIT),
        name="regroup_w_in",
    )(jnp.swapaxes(w_in, 1, 2))


def _s5_prep(lam_re, lam_im, b_re, b_im, c_re, c_im, d, log_dt):
    hp = lax.Precision.HIGHEST
    gh, npow = SSM_GROUP, SS + 1
    lr = lam_re.reshape(PAIRS, PAIR_STATE)
    li = lam_im.reshape(PAIRS, PAIR_STATE)
    dt = jnp.repeat(jnp.exp(log_dt), SSM_STATE).reshape(PAIRS, PAIR_STATE)
    ks = jnp.arange(npow, dtype=_F32)[None, :, None]
    mag = jnp.exp((lr * dt)[:, None, :] * ks)
    ang = (li * dt)[:, None, :] * ks
    ak_re, ak_im = mag * jnp.cos(ang), mag * jnp.sin(ang)
    nr, ni = ak_re[:, 1] - 1.0, ak_im[:, 1]
    den = lr * lr + li * li
    z_re = (nr * lr + ni * li) / den
    z_im = (ni * lr - nr * li) / den
    bt_re = jnp.swapaxes(b_re.reshape(PAIRS, PAIR_STATE, gh), 1, 2)
    bt_im = jnp.swapaxes(b_im.reshape(PAIRS, PAIR_STATE, gh), 1, 2)
    bb_re = z_re[:, None, :] * bt_re - z_im[:, None, :] * bt_im
    bb_im = z_re[:, None, :] * bt_im + z_im[:, None, :] * bt_re
    as_hn = lambda c: c.reshape(PAIRS, 2, gh, SSM_STATE).transpose(0, 2, 1, 3).reshape(PAIRS, gh, PAIR_STATE)
    ct_re, ct_im = as_hn(c_re), as_hn(c_im)
    same_group = (jnp.arange(2)[:, None] == (jnp.arange(PAIR_STATE) // SSM_STATE)[None, :]).astype(_F32)
    same_group = same_group[None, None, :, None, :]

    def times_powers(m_re, m_im, pw_re, pw_im):
        pr, pi = pw_re[:, :, None, None, :], pw_im[:, :, None, None, :]
        mr, mi = m_re[:, None, None, :, :], m_im[:, None, None, :, :]
        shape = (PAIRS, npow * PAIR_IN, PAIR_STATE)
        return (((pr * mr - pi * mi) * same_group).reshape(shape),
                ((pr * mi + pi * mr) * same_group).reshape(shape))

    wk_re, wk_im = times_powers(bb_re, bb_im, ak_re[:, ::-1], ak_im[:, ::-1])
    bs = jnp.concatenate([wk_re[:, PAIR_IN:], wk_im[:, PAIR_IN:]], axis=-1)

    ca_re, ca_im = times_powers(ct_re, ct_im, ak_re, ak_im)
    cs = jnp.concatenate([ca_re[:, PAIR_IN:], -ca_im[:, PAIR_IN:]], axis=-1)

    bbm_re, bbm_im = wk_re[:, SS * PAIR_IN:], wk_im[:, SS * PAIR_IN:]
    kt = (jnp.einsum('qrn,qcn->qrc', bbm_re, ca_re[:, :PAIR_W], precision=hp)
          - jnp.einsum('qrn,qcn->qrc', bbm_im, ca_im[:, :PAIR_W], precision=hp))
    ds = jnp.stack([jnp.pad(kt[..., :PAIR_W - PAIR_IN * i], ((0, 0), (0, 0), (PAIR_IN * i, 0)))
                    for i in range(SS)], axis=1).reshape(PAIRS, PAIR_W, PAIR_W)
    skip = jnp.tile(d.reshape(PAIRS, PAIR_IN), (1, SS))
    ds = ds + skip[:, None, :] * jnp.eye(PAIR_W, dtype=_F32)[None]

    ks8 = SS * jnp.arange(SUBLANES + 1, dtype=_F32)
    mag8 = jnp.exp((lr * dt).reshape(HALF)[None, :] * ks8[:, None])
    ang8 = (li * dt).reshape(HALF)[None, :] * ks8[:, None]
    apow = jnp.concatenate([mag8 * jnp.cos(ang8), mag8 * jnp.sin(ang8)], axis=1)
    sub = jnp.arange(SUBLANES)
    shifted = [jnp.where((sub >= k)[:, None], apow[k][None, :], 0.0) for k in (1, 2, 4)]
    a8 = jnp.stack(shifted + [apow[:SUBLANES], jnp.broadcast_to(apow[SUBLANES][None], (SUBLANES, 2 * HALF))])
    return bs.astype(_BF16), cs.astype(_BF16), ds.astype(_BF16), a8


def kernel(x, meta_tokens, g_mix_pre, g_mix_post, w_in, b_gates, ssm_lambda_re, ssm_lambda_im, ssm_b_re,
           ssm_b_im, ssm_c_re, ssm_c_im, ssm_d, ssm_log_dt, w_ssm_glu, w_qk_conv, b_qk_conv, g_head_norm,
           w_branch_ssm, w_branch_mlstm, w_out, g_ffn_pre, g_ffn_post, w_ffn_gate, w_ffn_up, w_ffn_conv,
           b_ffn_conv, w_ffn_down):
    bsz, seq, _ = x.shape
    assert bsz == 1
    depth = w_in.shape[0]
    assert seq % TL == 0 and FRONT_PAD + N_META == TL
    lp = TL + seq
    assert (lp // SS) % SUBLANES == 0

    head = jnp.concatenate([jnp.zeros((FRONT_PAD, D_MODEL), _F32), meta_tokens.astype(_F32)], axis=0)
    stream = (head, x[0])

    nh2 = 2 * MLSTM_HEADS
    row = lambda a: a[:, None, :]
    w_u, w_mix, w_gr = _regroup_w_in(w_in)
    b_gc = row(jnp.pad(b_gates, ((0, 0), (0, GATE_PAD - nh2))))
    b_gr = jnp.broadcast_to(b_gates[:, :, None], (depth, nh2, LANES))
    w_cv = jnp.concatenate([w_qk_conv, row(b_qk_conv),
                            jnp.zeros((depth, SUBLANES - QK_CONV - 1, 2 * MLSTM_WIDTH), _F32)], axis=1)
    mixer_consts = (row(g_mix_pre), row(g_mix_post), w_mix, w_gr, b_gc, b_gr, w_cv,
                    row(g_head_norm), w_ssm_glu.astype(_BF16), w_branch_ssm.astype(_BF16),
                    w_branch_mlstm.astype(_BF16), w_out.astype(_BF16))
    f_cv = jnp.concatenate([w_ffn_conv, row(b_ffn_conv),
                            jnp.zeros((depth, SUBLANES - FFN_CONV - 1, FFN_DIM), _F32)], axis=1)
    ffn_consts = (row(g_ffn_pre), row(g_ffn_post), w_ffn_gate.astype(_BF16), w_ffn_up.astype(_BF16),
                  w_ffn_down.astype(_BF16), f_cv)
    bs, cs, ds, a8 = jax.vmap(_s5_prep)(ssm_lambda_re, ssm_lambda_im, ssm_b_re, ssm_b_im,
                                        ssm_c_re, ssm_c_im, ssm_d, ssm_log_dt)
    g_pre = row(g_mix_pre)

    u = _s5_in(stream, g_pre, w_u, 0, lp)
    for l in range(depth):
        y5 = _s5_core(u, bs, cs, ds, a8, l)
        h = _mixer(stream, y5, mixer_consts, l, lp)
        if l + 1 < depth:
            h, u = _ffn(h, ffn_consts, l, next_s5=(g_pre, w_u))
        else:
            h = _ffn(h, ffn_consts, l)
        stream = (h,)
    return h[None]
```
